```python
import math
import jax, jax.numpy as jnp
from jax import lax
import numpy as np

D_MODEL = 4096
BATCH = 4
SEQ = 2048
DEPTH = 2
DEC_BATCH = 128
DEC_SEQ = 1
PAST_LEN = 16384
PAGE_SIZE = 128

RET_WIDTH = D_MODEL // 2
RET_HEADS = 8
RET_HEAD_DIM = RET_WIDTH // RET_HEADS
RET_CHUNK = 128
ROPE_BASE = 10000.0
POOL_WIDTH = D_MODEL - RET_WIDTH
POOL_WINDOWS = (2, 4, 8, 16)
POOL_GROUPS = len(POOL_WINDOWS)
POOL_GROUP_DIM = POOL_WIDTH // POOL_GROUPS
POOL_BUF = max(POOL_WINDOWS) - 1
IN_WIDTH = 4 * RET_WIDTH + POOL_WIDTH
D_FF = ((8 * D_MODEL // 3) + 255) // 256 * 256
CONV_WIDTH = 3
DN_ALPHA = (2.0 * DEPTH) ** 0.25
DN_BETA = (8.0 * DEPTH) ** -0.25
LN_EPS = 1e-5

kernel_name = "hybrid_retention_pool_convffn_step"


def layer_norm(x, g, b):
    xf = x.astype(jnp.float32)
    mu = jnp.mean(xf, -1, keepdims=True)
    var = jnp.mean(jnp.square(xf - mu), -1, keepdims=True)
    y = (xf - mu) * lax.rsqrt(var + LN_EPS) * g.astype(jnp.float32) + b.astype(jnp.float32)
    return y.astype(x.dtype)


def rotary(x, pos):
    half = x.shape[-1] // 2
    inv = ROPE_BASE ** (-jnp.arange(half, dtype=jnp.float32) / half)
    ang = pos.astype(jnp.float32)[:, None] * inv[None, :]
    cos, sin = jnp.cos(ang), jnp.sin(ang)
    x1, x2 = x[..., :half], x[..., half:]
    return jnp.concatenate([x1 * cos - x2 * sin, x2 * cos + x1 * sin], -1)


def retention(q, k, v, s0):
    n, h, t, d = q.shape
    l = RET_CHUNK if t % RET_CHUNK == 0 else t
    nc = t // l
    log_g = jnp.log1p(-(2.0 ** (-5.0 - jnp.arange(RET_HEADS, dtype=jnp.float32))))
    i = jnp.arange(l)
    diff = i[:, None] - i[None, :]
    decay = jnp.where(diff[None] >= 0,
                      jnp.exp(jnp.maximum(diff, 0)[None].astype(jnp.float32) * log_g[:, None, None]),
                      0.0)
    xi = jnp.exp((i + 1)[None].astype(jnp.float32) * log_g[:, None])
    zeta = jnp.exp((l - 1 - i)[None].astype(jnp.float32) * log_g[:, None])
    g_l = jnp.exp(l * log_g)

    def to_chunks(a):
        return a.reshape(n, h, nc, l, d).transpose(2, 0, 1, 3, 4)

    def step(s, c):
        qc, kc, vc = c
        scores = jnp.einsum('nhid,nhjd->nhij', qc, kc) * decay
        o = jnp.einsum('nhij,nhjv->nhiv', scores, vc) \
            + jnp.einsum('nhid,nhdv->nhiv', qc * xi[:, :, None], s)
        s = g_l[:, None, None] * s + jnp.einsum('nhjd,nhjv->nhdv', kc * zeta[:, :, None], vc)
        return s, o

    s, o = lax.scan(step, s0, (to_chunks(q), to_chunks(k), to_chunks(v)))
    o = o.transpose(1, 2, 0, 3, 4).reshape(n, h, t, d)
    return o, s


def pool_mix(u, buf, pos0, w_pool, scale):
    n, t, p = u.shape
    uf = u.astype(jnp.float32)
    ext = jnp.concatenate([buf.astype(jnp.float32), uf], 1)
    csum = jnp.concatenate([jnp.zeros((n, 1, p), jnp.float32), jnp.cumsum(ext, 1)], 1)
    pos = pos0 + jnp.arange(t)
    outs = []
    for gi, w in enumerate(POOL_WINDOWS):
        sl = slice(gi * POOL_GROUP_DIM, (gi + 1) * POOL_GROUP_DIM)
        c = csum[:, :, sl]
        s = c[:, POOL_BUF + 1:POOL_BUF + 1 + t] - c[:, POOL_BUF + 1 - w:POOL_BUF + 1 - w + t]
        cnt = jnp.minimum(w, pos + 1).astype(jnp.float32)[None, :, None]
        outs.append(s / cnt - uf[:, :, sl])
    pooled = jnp.stack(outs, 2)
    mixed = jnp.einsum('ntgc,gcd->ntgd', pooled, w_pool.astype(jnp.float32)).reshape(n, t, p)
    mixed = mixed * scale.astype(jnp.float32)
    return mixed.astype(u.dtype), ext[:, -POOL_BUF:]


def conv_ffn(x, buf, w_up, conv_w, conv_b, w_down):
    t = x.shape[1]
    hu = x @ w_up
    a, b = hu[..., :D_FF], hu[..., D_FF:]
    ext = jnp.concatenate([buf.astype(a.dtype), a], 1)
    conv = conv_b
    for kk in range(CONV_WIDTH):
        conv = conv + ext[:, kk:kk + t] * conv_w[kk]
    h = jax.nn.gelu(conv, approximate=False) * b
    return h @ w_down, ext[:, -(CONV_WIDTH - 1):]


def layer(x, s_ret, p_buf, c_buf, pos0, w_in, w_pool, pool_scale, w_o,
          ln1_g, ln1_b, w_up, conv_w, conv_b, w_down, ln2_g, ln2_b):
    n, t, _ = x.shape
    proj = x @ w_in
    q = proj[..., 0 * RET_WIDTH:1 * RET_WIDTH]
    k = proj[..., 1 * RET_WIDTH:2 * RET_WIDTH]
    v = proj[..., 2 * RET_WIDTH:3 * RET_WIDTH]
    g = proj[..., 3 * RET_WIDTH:4 * RET_WIDTH]
    u = proj[..., 4 * RET_WIDTH:]

    def heads(a):
        return a.reshape(n, t, RET_HEADS, RET_HEAD_DIM).transpose(0, 2, 1, 3).astype(jnp.float32)

    pos = pos0 + jnp.arange(t)
    qf = rotary(heads(q), pos)
    kf = rotary(heads(k), pos) * (RET_HEAD_DIM ** -0.5)
    o, s_new = retention(qf, kf, heads(v), s_ret.astype(jnp.float32))
    mu = jnp.mean(o, -1, keepdims=True)
    var = jnp.mean(jnp.square(o - mu), -1, keepdims=True)
    o = (o - mu) * lax.rsqrt(var + LN_EPS)
    ret_out = o.transpose(0, 2, 1, 3).reshape(n, t, RET_WIDTH).astype(x.dtype) * jax.nn.silu(g)

    pool_out, p_new = pool_mix(u, p_buf, pos0, w_pool, pool_scale)
    mix = jnp.concatenate([ret_out, pool_out], -1) @ w_o
    x = layer_norm(DN_ALPHA * x + mix, ln1_g, ln1_b)

    f, c_new = conv_ffn(x, c_buf, w_up, conv_w, conv_b, w_down)
    x = layer_norm(DN_ALPHA * x + f, ln2_g, ln2_b)
    return x, s_new.astype(s_ret.dtype), p_new.astype(p_buf.dtype), c_new.astype(c_buf.dtype)


def setup_inputs(seed: int = 0) -> dict:
    key = jax.random.key(seed)
    ks = jax.random.split(key, 17)
    nrm = jax.random.normal
    f32 = jnp.float32
    return {
        "x_prompt": nrm(ks[0], (BATCH, SEQ, D_MODEL), f32),
        "x_sample": nrm(ks[1], (DEC_BATCH, DEC_SEQ, D_MODEL), f32),
        "state_ret": 0.1 * nrm(ks[2], (DEPTH, DEC_BATCH, RET_HEADS, RET_HEAD_DIM, RET_HEAD_DIM), f32),
        "state_pool": nrm(ks[3], (DEPTH, DEC_BATCH, POOL_BUF, POOL_WIDTH), f32),
        "state_conv": nrm(ks[4], (DEPTH, DEC_BATCH, CONV_WIDTH - 1, D_FF), f32),
        "w_in": nrm(ks[5], (DEPTH, D_MODEL, IN_WIDTH), f32) * D_MODEL ** -0.5,
        "w_pool": nrm(ks[6], (DEPTH, POOL_GROUPS, POOL_GROUP_DIM, POOL_GROUP_DIM), f32) * POOL_GROUP_DIM ** -0.5,
        "pool_scale": 1.0 + 0.1 * nrm(ks[7], (DEPTH, POOL_WIDTH), f32),
        "w_o": nrm(ks[8], (DEPTH, D_MODEL, D_MODEL), f32) * (D_MODEL ** -0.5 * DN_BETA),
        "ln1_g": 1.0 + 0.05 * nrm(ks[9], (DEPTH, D_MODEL), f32),
        "ln1_b": 0.02 * nrm(ks[10], (DEPTH, D_MODEL), f32),
        "w_up": nrm(ks[11], (DEPTH, D_MODEL, 2 * D_FF), f32) * D_MODEL ** -0.5,
        "conv_w": nrm(ks[12], (DEPTH, CONV_WIDTH, D_FF), f32) * CONV_WIDTH ** -0.5,
        "conv_b": 0.02 * nrm(ks[13], (DEPTH, D_FF), f32),
        "w_down": nrm(ks[14], (DEPTH, D_FF, D_MODEL), f32) * (D_FF ** -0.5 * DN_BETA),
        "ln2_g": 1.0 + 0.05 * nrm(ks[15], (DEPTH, D_MODEL), f32),
        "ln2_b": 0.02 * nrm(ks[16], (DEPTH, D_MODEL), f32),
    }


def reference(x_prompt, x_sample, state_ret, state_pool, state_conv, w_in, w_pool, pool_scale,
              w_o, ln1_g, ln1_b, w_up, conv_w, conv_b, w_down, ln2_g, ln2_b):
    yp, ys = x_prompt, x_sample
    ret_p, ret_s, pool_p, pool_s, conv_p, conv_s = [], [], [], [], [], []
    for li in range(DEPTH):
        params = (w_in[li], w_pool[li], pool_scale[li], w_o[li], ln1_g[li], ln1_b[li],
                  w_up[li], conv_w[li], conv_b[li], w_down[li], ln2_g[li], ln2_b[li])
        s0 = jnp.zeros((BATCH, RET_HEADS, RET_HEAD_DIM, RET_HEAD_DIM), state_ret.dtype)
        pb0 = jnp.zeros((BATCH, POOL_BUF, POOL_WIDTH), state_pool.dtype)
        cb0 = jnp.zeros((BATCH, CONV_WIDTH - 1, D_FF), state_conv.dtype)
        yp, s_p, pb_p, cb_p = layer(yp, s0, pb0, cb0, 0, *params)
        ys, s_s, pb_s, cb_s = layer(ys, state_ret[li], state_pool[li], state_conv[li], PAST_LEN, *params)
        ret_p.append(s_p); ret_s.append(s_s)
        pool_p.append(pb_p); pool_s.append(pb_s)
        conv_p.append(cb_p); conv_s.append(cb_s)
    new_ret_prompt = jnp.stack(ret_p)
    new_ret_sample = jnp.stack(ret_s)
    new_pool_prompt = jnp.stack(pool_p)
    new_pool_sample = jnp.stack(pool_s)
    new_conv_prompt = jnp.stack(conv_p)
    new_conv_sample = jnp.stack(conv_s)
    return (yp, ys, new_ret_prompt, new_ret_sample, new_pool_prompt, new_pool_sample,
            new_conv_prompt, new_conv_sample)
```

```python
import functools

import jax
import jax.numpy as jnp
from jax import lax
from jax.experimental import pallas as pl
from jax.experimental.pallas import tpu as pltpu

F32 = jnp.float32
BF16 = jnp.bfloat16

D_MODEL = 4096
DEPTH = 2
PAST_LEN = 16384
RET_WIDTH = D_MODEL // 2
RET_HEADS = 8
HEAD_DIM = RET_WIDTH // RET_HEADS
HALF_HEAD = HEAD_DIM // 2
RET_CHUNK = 128
ROPE_BASE = 10000.0
POOL_WIDTH = D_MODEL - RET_WIDTH
POOL_WINDOWS = (2, 4, 8, 16)
POOL_GROUP_DIM = POOL_WIDTH // len(POOL_WINDOWS)
POOL_BUF = max(POOL_WINDOWS) - 1
IN_WIDTH = 4 * RET_WIDTH + POOL_WIDTH
D_FF = ((8 * D_MODEL // 3) + 255) // 256 * 256
CONV_WIDTH = 3
DN_ALPHA = (2.0 * DEPTH) ** 0.25
LN_EPS = 1e-5
K_SCALE = HEAD_DIM ** -0.5

V7X_VMEM_BYTES = 64 * 1024 * 1024
VMEM_LIMIT = V7X_VMEM_BYTES - 8 * 1024 * 1024
SUBLANES = 8
HALO_ROWS = 2 * SUBLANES

PROJ_TN = 512
FFN_TN = 256
LN_TM = 512
OUT_TN = 512
LN_ROWS = 64
DOWN_TK = D_FF // 2
DOWN_TN = 512
POOL_TM = 512
SAMPLE_NB = 16


def _params(*semantics):
    return pltpu.CompilerParams(dimension_semantics=semantics, vmem_limit_bytes=VMEM_LIMIT)


def _layer_norm(y, g, b):
    mu = jnp.mean(y, axis=-1, keepdims=True)
    d = y - mu
    var = jnp.mean(d * d, axis=-1, keepdims=True)
    return d * lax.rsqrt(var + LN_EPS) * g + b


def _layer_norm_rows(y_ref, ybf_ref, g_ref, b_ref):
    g = g_ref[...]
    b = b_ref[...]

    def body(r, carry):
        rows = pl.ds(pl.multiple_of(r * LN_ROWS, LN_ROWS), LN_ROWS)
        y = _layer_norm(y_ref[rows, :], g, b)
        y_ref[rows, :] = y
        ybf_ref[rows, :] = y.astype(BF16)
        return carry

    lax.fori_loop(0, y_ref.shape[0] // LN_ROWS, body, 0)


def _gelu_exact(x):
    return 0.5 * x * (1.0 + lax.erf(x * (0.5 ** 0.5)))


def _rotate_heads(o_ref, cos_ref, sin_ref, scale):
    cos = cos_ref[...]
    sin = sin_ref[...]
    for h in range(o_ref.shape[1] // HEAD_DIM):
        lo = pl.ds(h * HEAD_DIM, HALF_HEAD)
        hi = pl.ds(h * HEAD_DIM + HALF_HEAD, HALF_HEAD)
        x1 = o_ref[:, lo]
        x2 = o_ref[:, hi]
        o_ref[:, lo] = (x1 * cos - x2 * sin) * scale
        o_ref[:, hi] = (x2 * cos + x1 * sin) * scale


def _proj_kernel(xp_ref, xs_ref, w_ref, cosp_ref, sinp_ref, coss_ref, sins_ref, op_ref, os_ref,
                 *, rot_tiles, q_tiles):
    i = pl.program_id(0)
    j = pl.program_id(1)
    w = w_ref[...].astype(BF16)
    scale = jnp.where(j >= q_tiles, K_SCALE, 1.0).astype(F32)
    op_ref[...] = jnp.dot(xp_ref[...], w, preferred_element_type=F32)

    @pl.when(j < rot_tiles)
    def _():
        _rotate_heads(op_ref, cosp_ref, sinp_ref, scale)

    @pl.when(i == 0)
    def _():
        os_ref[...] = jnp.dot(xs_ref[...], w, preferred_element_type=F32)

    @pl.when((i == 0) & (j < rot_tiles))
    def _():
        _rotate_heads(os_ref, coss_ref, sins_ref, scale)


def _proj(xp_bf, xs_bf, w_in, li, cos_p, sin_p, cos_s, sin_s, tm):
    mp = xp_bf.shape[0]
    ns = xs_bf.shape[0]
    nj = IN_WIDTH // PROJ_TN
    t_tiles = cos_p.shape[0] // tm
    kern = functools.partial(_proj_kernel, rot_tiles=2 * RET_WIDTH // PROJ_TN,
                             q_tiles=RET_WIDTH // PROJ_TN)
    whole = lambda a: pl.BlockSpec(a.shape, lambda i, j: (0, 0))
    return pl.pallas_call(
        kern,
        grid=(mp // tm, nj),
        in_specs=[
            pl.BlockSpec((tm, D_MODEL), lambda i, j: (i, 0), pipeline_mode=pl.Buffered(1)),
            whole(xs_bf),
            pl.BlockSpec((None, D_MODEL, PROJ_TN), lambda i, j: (li, 0, j)),
            pl.BlockSpec((tm, HALF_HEAD), lambda i, j: (i % t_tiles, 0)),
            pl.BlockSpec((tm, HALF_HEAD), lambda i, j: (i % t_tiles, 0)),
            whole(cos_s), whole(sin_s),
        ],
        out_specs=[
            pl.BlockSpec((tm, PROJ_TN), lambda i, j: (i, j)),
            pl.BlockSpec((ns, PROJ_TN), lambda i, j: (0, jnp.where(i == 0, j, nj - 1))),
        ],
        out_shape=[jax.ShapeDtypeStruct((mp, IN_WIDTH), F32),
                   jax.ShapeDtypeStruct((ns, IN_WIDTH), F32)],
        compiler_params=_params("arbitrary", "arbitrary"),
        name="proj",
    )(xp_bf, xs_bf, w_in, cos_p, sin_p, cos_s, sin_s)


def _group_norm_gate(o, g):
    mu = jnp.mean(o, axis=-1, keepdims=True)
    d = o - mu
    var = jnp.mean(d * d, axis=-1, keepdims=True)
    return d * lax.rsqrt(var + LN_EPS) * (g * jax.nn.sigmoid(g))


def _ret_prompt_kernel(q_ref, k_ref, v_ref, g_ref, decay_ref, xi_ref, zeta_ref, gl_ref,
                       o_ref, snew_ref, s_scr):
    c = pl.program_id(1)

    @pl.when(c == 0)
    def _():
        s_scr[...] = jnp.zeros_like(s_scr)

    for h in range(RET_HEADS):
        cols = pl.ds(h * HEAD_DIM, HEAD_DIM)
        q = q_ref[:, cols]
        k = k_ref[:, cols]
        vb = v_ref[:, cols].astype(BF16)
        s = s_scr[h]
        scores = lax.dot_general(q.astype(BF16), k.astype(BF16), (((1,), (1,)), ((), ())),
                                 preferred_element_type=F32) * decay_ref[h]
        o = jnp.dot(scores.astype(BF16), vb, preferred_element_type=F32)
        o = o + jnp.dot((q * xi_ref[h]).astype(BF16), s.astype(BF16), preferred_element_type=F32)
        kz_t = (k * zeta_ref[h]).T.astype(BF16)
        s_scr[h] = gl_ref[h] * s + jnp.dot(kz_t, vb, preferred_element_type=F32)
        o_ref[:, cols] = _group_norm_gate(o, g_ref[:, cols]).astype(BF16)

    @pl.when(c == pl.num_programs(1) - 1)
    def _():
        snew_ref[0] = s_scr[...]


def _ret_prompt(proj, decay, xi, zeta, gl, batch, seq):
    nc = seq // RET_CHUNK

    def col(cb):
        return pl.BlockSpec((RET_CHUNK, RET_WIDTH), lambda b, c: (b * nc + c, cb))

    def whole(a):
        return pl.BlockSpec(a.shape, lambda b, c: (0,) * a.ndim)

    return pl.pallas_call(
        _ret_prompt_kernel,
        grid=(batch, nc),
        in_specs=[col(0), col(1), col(2), col(3), whole(decay), whole(xi), whole(zeta), whole(gl)],
        out_specs=[
            pl.BlockSpec((RET_CHUNK, RET_WIDTH), lambda b, c: (b * nc + c, 0)),
            pl.BlockSpec((1, RET_HEADS, HEAD_DIM, HEAD_DIM), lambda b, c: (b, 0, 0, 0)),
        ],
        out_shape=[
            jax.ShapeDtypeStruct((batch * seq, RET_WIDTH), BF16),
            jax.ShapeDtypeStruct((batch, RET_HEADS, HEAD_DIM, HEAD_DIM), F32),
        ],
        scratch_shapes=[pltpu.VMEM((RET_HEADS, HEAD_DIM, HEAD_DIM), F32)],
        compiler_params=_params("parallel", "arbitrary"),
        name="ret_prompt",
    )(proj, proj, proj, proj, decay, xi, zeta, gl)


def _ret_sample_kernel(q_ref, k_ref, v_ref, g_ref, s0_ref, gam_ref, *rest):
    o_ref, snew_ref = rest[-2:]
    q = q_ref[...]
    k = k_ref[...]
    v = v_ref[...]
    vb = v.astype(BF16)
    gam = gam_ref[0]
    qg = (q * gam).astype(BF16)
    rows = lax.broadcasted_iota(jnp.int32, (SAMPLE_NB, 1), 0)
    o_state = jnp.zeros((SAMPLE_NB, HEAD_DIM), F32)
    for n in range(SAMPLE_NB):
        s = s0_ref[n, 0]
        r = jnp.dot(qg, s.astype(BF16), preferred_element_type=F32)
        o_state = jnp.where(rows == n, r, o_state)
        k_n_t = jnp.where(rows == n, k, 0.0).T.astype(BF16)
        snew_ref[n, 0] = gam * s + jnp.dot(k_n_t, vb, preferred_element_type=F32)
    qk = jnp.sum(q * k, axis=-1, keepdims=True)
    o = qk * v + o_state
    o_ref[...] = _group_norm_gate(o, g_ref[...]).astype(BF16)


def _ret_sample(proj, state_all, li, gam, new_state_all):
    n = proj.shape[0]

    def col(cb):
        return pl.BlockSpec((SAMPLE_NB, HEAD_DIM), lambda i, h: (i, cb * RET_HEADS + h))

    state_spec = pl.BlockSpec((None, SAMPLE_NB, 1, HEAD_DIM, HEAD_DIM),
                              lambda i, h: (li, i, h, 0, 0))
    in_specs = [col(0), col(1), col(2), col(3), state_spec,
                pl.BlockSpec((1, 1, HEAD_DIM), lambda i, h: (h, 0, 0))]
    args = [proj, proj, proj, proj, state_all, gam]
    aliases = {}
    if new_state_all is not None:
        in_specs.append(pl.BlockSpec(memory_space=pl.ANY))
        args.append(new_state_all)
        aliases = {len(args) - 1: 1}
    return pl.pallas_call(
        _ret_sample_kernel,
        grid=(n // SAMPLE_NB, RET_HEADS),
        in_specs=in_specs,
        out_specs=[pl.BlockSpec((SAMPLE_NB, HEAD_DIM), lambda i, h: (i, h)), state_spec],
        out_shape=[jax.ShapeDtypeStruct((n, RET_WIDTH), BF16),
                   jax.ShapeDtypeStruct(state_all.shape, F32)],
        input_output_aliases=aliases,
        compiler_params=_params("parallel", "arbitrary"),
        name="ret_sample",
    )(*args)


def _pool_prompt_kernel(u_ref, halo_ref, wp_ref, sc_ref, o_ref, ext_scr, *, tiles_per_seq):
    i = pl.program_id(0)
    tm = u_ref.shape[0]
    t0 = (i % tiles_per_seq) * tm
    ext_scr[0:HALO_ROWS, :] = jnp.where(t0 == 0, 0.0, halo_ref[...])
    ext_scr[HALO_ROWS:HALO_ROWS + tm, :] = u_ref[...]
    pos = t0 + lax.broadcasted_iota(jnp.int32, (tm, 1), 0)
    for gi, w in enumerate(POOL_WINDOWS):
        cols = pl.ds(gi * POOL_GROUP_DIM, POOL_GROUP_DIM)
        u = u_ref[:, cols]
        s = u
        for back in range(1, w):
            s = s + ext_scr[pl.ds(HALO_ROWS - back, tm), cols]
        cnt = jnp.minimum(w, pos + 1).astype(F32)
        pooled = s / cnt - u
        mixed = jnp.dot(pooled.astype(BF16), wp_ref[gi], preferred_element_type=F32)
        o_ref[:, cols] = (mixed * sc_ref[:, cols]).astype(BF16)


def _pool_prompt(proj, w_pool_bf, scale, seq):
    m = proj.shape[0]
    tiles_per_seq = seq // POOL_TM
    u_col = 4 * RET_WIDTH // POOL_WIDTH
    halo_per_tile = POOL_TM // HALO_ROWS
    kern = functools.partial(_pool_prompt_kernel, tiles_per_seq=tiles_per_seq)
    return pl.pallas_call(
        kern,
        grid=(m // POOL_TM,),
        in_specs=[
            pl.BlockSpec((POOL_TM, POOL_WIDTH), lambda i: (i, u_col)),
            pl.BlockSpec((HALO_ROWS, POOL_WIDTH),
                         lambda i: (jnp.maximum(i * halo_per_tile - 1, 0), u_col)),
            pl.BlockSpec(w_pool_bf.shape, lambda i: (0, 0, 0)),
            pl.BlockSpec((1, POOL_WIDTH), lambda i: (0, 0)),
        ],
        out_specs=pl.BlockSpec((POOL_TM, POOL_WIDTH), lambda i: (i, 0)),
        out_shape=jax.ShapeDtypeStruct((m, POOL_WIDTH), BF16),
        scratch_shapes=[pltpu.VMEM((HALO_ROWS + POOL_TM, POOL_WIDTH), F32)],
        compiler_params=_params("parallel"),
        name="pool_prompt",
    )(proj, proj, w_pool_bf, scale)


def _pool_sample_kernel(u_ref, buf_ref, wp_ref, sc_ref, o_ref):
    row = lax.broadcasted_iota(jnp.int32, (1, POOL_BUF, 1), 1)
    for gi, w in enumerate(POOL_WINDOWS):
        cols = pl.ds(gi * POOL_GROUP_DIM, POOL_GROUP_DIM)
        u = u_ref[:, cols]
        past = jnp.where(row >= POOL_BUF - (w - 1), buf_ref[:, :, cols], 0.0)
        s = u + jnp.sum(past, axis=1)
        pooled = s / float(w) - u
        mixed = jnp.dot(pooled.astype(BF16), wp_ref[gi], preferred_element_type=F32)
        o_ref[:, cols] = (mixed * sc_ref[:, cols]).astype(BF16)


def _pool_sample(proj, buf_all, li, w_pool_bf, scale):
    n = proj.shape[0]
    u_col = 4 * RET_WIDTH // POOL_WIDTH
    return pl.pallas_call(
        _pool_sample_kernel,
        grid=(n // SAMPLE_NB,),
        in_specs=[
            pl.BlockSpec((SAMPLE_NB, POOL_WIDTH), lambda i: (i, u_col)),
            pl.BlockSpec((None, SAMPLE_NB, POOL_BUF, POOL_WIDTH), lambda i: (li, i, 0, 0)),
            pl.BlockSpec(w_pool_bf.shape, lambda i: (0, 0, 0)),
            pl.BlockSpec((1, POOL_WIDTH), lambda i: (0, 0)),
        ],
        out_specs=pl.BlockSpec((SAMPLE_NB, POOL_WIDTH), lambda i: (i, 0)),
        out_shape=jax.ShapeDtypeStruct((n, POOL_WIDTH), BF16),
        compiler_params=_params("parallel"),
        name="pool_sample",
    )(proj, buf_all, w_pool_bf, scale)


def _out_proj_kernel(ret_ref, pool_ref, wr_ref, wp_ref, x_ref, g_ref, b_ref, y_ref, ybf_ref):
    j = pl.program_id(1)
    tn = wr_ref.shape[1]
    cols = pl.ds(pl.multiple_of(j * tn, tn), tn)
    mix = jnp.dot(ret_ref[...], wr_ref[...], preferred_element_type=F32)
    mix = mix + jnp.dot(pool_ref[...], wp_ref[...], preferred_element_type=F32)
    y_ref[:, cols] = DN_ALPHA * x_ref[...] + mix

    @pl.when(j == pl.num_programs(1) - 1)
    def _():
        _layer_norm_rows(y_ref, ybf_ref, g_ref, b_ref)


def _out_proj(ret, pool, w_o_bf, x, g, b, tm):
    m = x.shape[0]
    return pl.pallas_call(
        _out_proj_kernel,
        grid=(m // tm, D_MODEL // OUT_TN),
        in_specs=[
            pl.BlockSpec((tm, RET_WIDTH), lambda i, j: (i, 0)),
            pl.BlockSpec((tm, POOL_WIDTH), lambda i, j: (i, 0)),
            pl.BlockSpec((RET_WIDTH, OUT_TN), lambda i, j: (0, j)),
            pl.BlockSpec((POOL_WIDTH, OUT_TN), lambda i, j: (1, j)),
            pl.BlockSpec((tm, OUT_TN), lambda i, j: (i, j)),
            pl.BlockSpec((1, D_MODEL), lambda i, j: (0, 0)),
            pl.BlockSpec((1, D_MODEL), lambda i, j: (0, 0)),
        ],
        out_specs=[pl.BlockSpec((tm, D_MODEL), lambda i, j: (i, 0)),
                   pl.BlockSpec((tm, D_MODEL), lambda i, j: (i, 0))],
        out_shape=[jax.ShapeDtypeStruct((m, D_MODEL), F32),
                   jax.ShapeDtypeStruct((m, D_MODEL), BF16)],
        compiler_params=_params("parallel", "arbitrary"),
        name="out_proj",
    )(ret, pool, w_o_bf, w_o_bf, x, g, b)


def _ffn_up_kernel(xp_ref, xs_ref, wa_ref, wb_ref, cw_ref, cb_ref, s0_ref, s1_ref,
                   hp_ref, tail_ref, hs_ref, n0_ref, n1_ref, ext_scr):
    i = pl.program_id(0)
    tm = xp_ref.shape[0]
    wa = wa_ref[...].astype(BF16)
    wb = wb_ref[...].astype(BF16)
    cw0 = cw_ref[0:1, :]
    cw1 = cw_ref[1:2, :]
    cw2 = cw_ref[2:3, :]
    cb = cb_ref[...]

    xp = xp_ref[...]
    ext_scr[0:SUBLANES, :] = jnp.zeros((SUBLANES, ext_scr.shape[1]), F32)
    ext_scr[SUBLANES:SUBLANES + tm, :] = jnp.dot(xp, wa, preferred_element_type=F32)
    tail_ref[0] = ext_scr[tm:tm + SUBLANES, :]
    conv = cb + ext_scr[pl.ds(SUBLANES - 2, tm), :] * cw0
    conv = conv + ext_scr[pl.ds(SUBLANES - 1, tm), :] * cw1
    conv = conv + ext_scr[pl.ds(SUBLANES, tm), :] * cw2
    b = jnp.dot(xp, wb, preferred_element_type=F32)
    hp_ref[...] = (_gelu_exact(conv) * b).astype(BF16)

    @pl.when(i == 0)
    def _():
        xs = xs_ref[...]
        a_s = jnp.dot(xs, wa, preferred_element_type=F32)
        b_s = jnp.dot(xs, wb, preferred_element_type=F32)
        s1 = s1_ref[...]
        conv_s = cb + s0_ref[...] * cw0
        conv_s = conv_s + s1 * cw1
        conv_s = conv_s + a_s * cw2
        hs_ref[...] = (_gelu_exact(conv_s) * b_s).astype(BF16)
        n0_ref[...] = s1
        n1_ref[...] = a_s


def _ffn_up(xp_bf, xs_bf, w_up, conv_w, conv_b, state2d, li, seq):
    mp = xp_bf.shape[0]
    ns = xs_bf.shape[0]
    nj = D_FF // FFN_TN
    tm = seq
    first = lambda i, j: jnp.where(i == 0, j, nj - 1)
    s_tile = pl.BlockSpec((ns, FFN_TN), lambda i, j: (0, first(i, j)))
    return pl.pallas_call(
        _ffn_up_kernel,
        grid=(mp // tm, nj),
        in_specs=[
            pl.BlockSpec((tm, D_MODEL), lambda i, j: (i, 0), pipeline_mode=pl.Buffered(1)),
            pl.BlockSpec((ns, D_MODEL), lambda i, j: (0, 0)),
            pl.BlockSpec((None, D_MODEL, FFN_TN), lambda i, j: (li, 0, j)),
            pl.BlockSpec((None, D_MODEL, FFN_TN), lambda i, j: (li, 0, nj + j)),
            pl.BlockSpec((None, CONV_WIDTH, FFN_TN), lambda i, j: (li, 0, j)),
            pl.BlockSpec((None, 1, FFN_TN), lambda i, j: (li, 0, j)),
            pl.BlockSpec((None, ns, FFN_TN), lambda i, j: (li, 0, first(i, j))),
            pl.BlockSpec((None, ns, FFN_TN), lambda i, j: (li, 0, nj + first(i, j))),
        ],
        out_specs=[
            pl.BlockSpec((tm, FFN_TN), lambda i, j: (i, j)),
            pl.BlockSpec((1, SUBLANES, FFN_TN), lambda i, j: (i, 0, j)),
            s_tile, s_tile, s_tile,
        ],
        out_shape=[jax.ShapeDtypeStruct((mp, D_FF), BF16),
                   jax.ShapeDtypeStruct((mp // tm, SUBLANES, D_FF), F32),
                   jax.ShapeDtypeStruct((ns, D_FF), BF16),
                   jax.ShapeDtypeStruct((ns, D_FF), F32),
                   jax.ShapeDtypeStruct((ns, D_FF), F32)],
        scratch_shapes=[pltpu.VMEM((SUBLANES + tm, FFN_TN), F32)],
        compiler_params=_params("arbitrary", "arbitrary"),
        name="ffn_up",
    )(xp_bf, xs_bf, w_up, w_up, conv_w, conv_b, state2d, state2d)


def _ffn_down_kernel(h_ref, w_ref, x_ref, g_ref, b_ref, y_ref, ybf_ref):
    k = pl.program_id(1)
    j = pl.program_id(2)
    tn = w_ref.shape[1]
    cols = pl.ds(pl.multiple_of(j * tn, tn), tn)
    part = jnp.dot(h_ref[...], w_ref[...], preferred_element_type=F32)

    @pl.when(k == 0)
    def _():
        y_ref[:, cols] = DN_ALPHA * x_ref[...] + part

    @pl.when(k > 0)
    def _():
        y_ref[:, cols] = y_ref[:, cols] + part

    @pl.when((k == pl.num_programs(1) - 1) & (j == pl.num_programs(2) - 1))
    def _():
        _layer_norm_rows(y_ref, ybf_ref, g_ref, b_ref)


def _ffn_down(h, w_down_bf, x, g, b, tm):
    m = x.shape[0]
    nk = D_FF // DOWN_TK
    nj = D_MODEL // DOWN_TN
    return pl.pallas_call(
        _ffn_down_kernel,
        grid=(m // tm, nk, nj),
        in_specs=[
            pl.BlockSpec((tm, DOWN_TK), lambda i, k, j: (i, k)),
            pl.BlockSpec((DOWN_TK, DOWN_TN), lambda i, k, j: (k, j)),
            pl.BlockSpec((tm, DOWN_TN), lambda i, k, j: (i, jnp.where(k == 0, j, nj - 1))),
            pl.BlockSpec((1, D_MODEL), lambda i, k, j: (0, 0)),
            pl.BlockSpec((1, D_MODEL), lambda i, k, j: (0, 0)),
        ],
        out_specs=[pl.BlockSpec((tm, D_MODEL), lambda i, k, j: (i, 0)),
                   pl.BlockSpec((tm, D_MODEL), lambda i, k, j: (i, 0))],
        out_shape=[jax.ShapeDtypeStruct((m, D_MODEL), F32),
                   jax.ShapeDtypeStruct((m, D_MODEL), BF16)],
        compiler_params=_params("parallel", "arbitrary", "arbitrary"),
        name="ffn_down",
    )(h, w_down_bf, x, g, b)


def _rotary_tables(pos):
    inv = ROPE_BASE ** (-jnp.arange(HALF_HEAD, dtype=F32) / HALF_HEAD)
    ang = pos.astype(F32)[:, None] * inv[None, :]
    return jnp.cos(ang), jnp.sin(ang)


def _retention_tables(l):
    log_g = jnp.log1p(-(2.0 ** (-5.0 - jnp.arange(RET_HEADS, dtype=F32))))
    i = jnp.arange(l)
    diff = i[:, None] - i[None, :]
    decay = jnp.where(diff[None] >= 0,
                      jnp.exp(jnp.maximum(diff, 0)[None].astype(F32) * log_g[:, None, None]), 0.0)
    xi = jnp.exp((i + 1)[None].astype(F32) * log_g[:, None])
    zeta = jnp.exp((l - 1 - i)[None].astype(F32) * log_g[:, None])
    g_l = jnp.exp(l * log_g)
    return decay, xi, zeta, g_l


def kernel(x_prompt, x_sample, state_ret, state_pool, state_conv, w_in, w_pool, pool_scale, w_o,
           ln1_g, ln1_b, w_up, conv_w, conv_b, w_down, ln2_g, ln2_b):
    batch, seq, _ = x_prompt.shape
    n_s = x_sample.shape[0]
    assert x_sample.shape[1] == 1 and seq % RET_CHUNK == 0
    mp = batch * seq

    cos_p, sin_p = _rotary_tables(jnp.arange(seq))
    cos_s, sin_s = _rotary_tables(PAST_LEN + jnp.arange(1))
    cos_s = jnp.broadcast_to(cos_s, (n_s, HALF_HEAD))
    sin_s = jnp.broadcast_to(sin_s, (n_s, HALF_HEAD))
    decay, xi, zeta, g_l = _retention_tables(RET_CHUNK)
    xi = xi[:, :, None]
    zeta = zeta[:, :, None]
    gl_b = jnp.broadcast_to(g_l[:, None, None], (RET_HEADS, 1, HEAD_DIM))
    _, xi_s, _, _ = _retention_tables(1)
    gam_s = jnp.broadcast_to(xi_s[:, :, None], (RET_HEADS, 1, HEAD_DIM))

    xp = x_prompt.reshape(mp, D_MODEL)
    xs = x_sample.reshape(n_s, D_MODEL)
    xp_bf = xp.astype(BF16)
    xs_bf = xs.astype(BF16)
    conv_state2d = state_conv.reshape(DEPTH, n_s, (CONV_WIDTH - 1) * D_FF)
    conv_b3 = conv_b[:, None, :]

    ret_p, pool_p, pool_s, conv_p, conv_s = [], [], [], [], []
    new_ret_sample = None
    for li in range(DEPTH):
        w_pool_bf = w_pool[li].astype(BF16)
        w_o_bf = w_o[li].astype(BF16)
        w_down_bf = w_down[li].astype(BF16)
        scale = pool_scale[li][None, :]
        g1, b1 = ln1_g[li][None, :], ln1_b[li][None, :]
        g2, b2 = ln2_g[li][None, :], ln2_b[li][None, :]

        proj_p, proj_s = _proj(xp_bf, xs_bf, w_in, li, cos_p, sin_p, cos_s, sin_s, seq)

        ret_out, s_new = _ret_prompt(proj_p, decay, xi, zeta, gl_b, batch, seq)
        pool_out = _pool_prompt(proj_p, w_pool_bf, scale, seq)
        xp, xp_bf = _out_proj(ret_out, pool_out, w_o_bf, xp, g1, b1, LN_TM)
        ret_p.append(s_new)
        pool_p.append(proj_p.reshape(batch, seq, IN_WIDTH)[:, seq - POOL_BUF:, 4 * RET_WIDTH:])

        ret_out, new_ret_sample = _ret_sample(proj_s, state_ret, li, gam_s, new_ret_sample)
        pool_out = _pool_sample(proj_s, state_pool, li, w_pool_bf, scale)
        xs, xs_bf = _out_proj(ret_out, pool_out, w_o_bf, xs, g1, b1, n_s)
        pool_s.append(jnp.concatenate([state_pool[li][:, 1:], proj_s[:, None, 4 * RET_WIDTH:]], 1))

        h_p, tail, h_s, c0, c1 = _ffn_up(xp_bf, xs_bf, w_up, conv_w, conv_b3, conv_state2d, li, seq)
        xp, xp_bf = _ffn_down(h_p, w_down_bf, xp, g2, b2, LN_TM)
        xs, xs_bf = _ffn_down(h_s, w_down_bf, xs, g2, b2, n_s)
        conv_p.append(tail[:, SUBLANES - (CONV_WIDTH - 1):, :])
        conv_s.append(jnp.stack([c0, c1], 1))

    return (xp.reshape(batch, seq, D_MODEL), xs.reshape(n_s, 1, D_MODEL),
            jnp.stack(ret_p), new_ret_sample, jnp.stack(pool_p), jnp.stack(pool_s),
            jnp.stack(conv_p), jnp.stack(conv_s))
```

```python
import functools

import jax
import jax.numpy as jnp
from jax import lax
from jax.experimental import pallas as pl
from jax.experimental.pallas import tpu as pltpu

F32 = jnp.float32
BF16 = jnp.bfloat16

D_MODEL = 4096
DEPTH = 2
PAST_LEN = 16384
RET_WIDTH = D_MODEL // 2
RET_HEADS = 8
HEAD_DIM = RET_WIDTH // RET_HEADS
HALF_HEAD = HEAD_DIM // 2
RET_CHUNK = 128
ROPE_BASE = 10000.0
POOL_WIDTH = D_MODEL - RET_WIDTH
POOL_WINDOWS = (2, 4, 8, 16)
POOL_GROUP_DIM = POOL_WIDTH // len(POOL_WINDOWS)
POOL_BUF = max(POOL_WINDOWS) - 1
IN_WIDTH = 4 * RET_WIDTH + POOL_WIDTH
D_FF = ((8 * D_MODEL // 3) + 255) // 256 * 256
CONV_WIDTH = 3
DN_ALPHA = (2.0 * DEPTH) ** 0.25
LN_EPS = 1e-5
K_SCALE = HEAD_DIM ** -0.5

V7X_VMEM_BYTES = 64 * 1024 * 1024
VMEM_LIMIT = V7X_VMEM_BYTES - 8 * 1024 * 1024
SUBLANES = 8
HALO_ROWS = 2 * SUBLANES

PROJ_TN = 512
FFN_TN = 256
LN_TM = 512
OUT_TN = 512
LN_ROWS = 64
DOWN_TK = D_FF // 2
DOWN_TN = 512
POOL_TM = 512
SAMPLE_NB = 16
DOT_ROWS = 1024
CAST_ROWS = 64


def _params(*semantics):
    return pltpu.CompilerParams(dimension_semantics=semantics, vmem_limit_bytes=VMEM_LIMIT)


def _layer_norm(y, g, b):
    mu = jnp.mean(y, axis=-1, keepdims=True)
    d = y - mu
    var = jnp.mean(d * d, axis=-1, keepdims=True)
    return d * lax.rsqrt(var + LN_EPS) * g + b


def _layer_norm_rows(y_ref, ybf_ref, g_ref, b_ref):
    g = g_ref[...]
    b = b_ref[...]

    def body(r, carry):
        rows = pl.ds(pl.multiple_of(r * LN_ROWS, LN_ROWS), LN_ROWS)
        y = _layer_norm(y_ref[rows, :], g, b)
        y_ref[rows, :] = y
        ybf_ref[rows, :] = y.astype(BF16)
        return carry

    lax.fori_loop(0, y_ref.shape[0] // LN_ROWS, body, 0)


def _gelu_exact(x):
    return 0.5 * x * (1.0 + lax.erf(x * (0.5 ** 0.5)))


def _rotate_heads(o_ref, cos_ref, sin_ref, scale):
    cos = cos_ref[...]
    sin = sin_ref[...]
    for h in range(o_ref.shape[1] // HEAD_DIM):
        lo = pl.ds(h * HEAD_DIM, HALF_HEAD)
        hi = pl.ds(h * HEAD_DIM + HALF_HEAD, HALF_HEAD)
        x1 = o_ref[:, lo]
        x2 = o_ref[:, hi]
        o_ref[:, lo] = (x1 * cos - x2 * sin) * scale
        o_ref[:, hi] = (x2 * cos + x1 * sin) * scale


def _proj_kernel(xp_ref, xs_ref, w_ref, cosp_ref, sinp_ref, coss_ref, sins_ref, wo_ref,
                 op_ref, os_ref, wobf_ref, *, rot_tiles, q_tiles, cast_steps):
    i = pl.program_id(0)
    j = pl.program_id(1)
    w = w_ref[...].astype(BF16)
    scale = jnp.where(j >= q_tiles, K_SCALE, 1.0).astype(F32)
    for r in range(xp_ref.shape[0] // DOT_ROWS):
        rows = pl.ds(r * DOT_ROWS, DOT_ROWS)
        op_ref[rows, :] = jnp.dot(xp_ref[rows, :], w, preferred_element_type=F32)

    @pl.when(i * pl.num_programs(1) + j < cast_steps)
    def _():
        wobf_ref[...] = wo_ref[...].astype(BF16)

    @pl.when(j < rot_tiles)
    def _():
        _rotate_heads(op_ref, cosp_ref, sinp_ref, scale)

    @pl.when(i == 0)
    def _():
        os_ref[...] = jnp.dot(xs_ref[...], w, preferred_element_type=F32)

    @pl.when((i == 0) & (j < rot_tiles))
    def _():
        _rotate_heads(os_ref, coss_ref, sins_ref, scale)


def _proj(xp_bf, xs_bf, w_in, w_o, li, cos_p, sin_p, cos_s, sin_s, tm):
    mp = xp_bf.shape[0]
    ns = xs_bf.shape[0]
    nj = IN_WIDTH // PROJ_TN
    t_tiles = cos_p.shape[0] // tm
    cast_steps = D_MODEL // CAST_ROWS
    assert cast_steps <= (mp // tm) * nj
    kern = functools.partial(_proj_kernel, rot_tiles=2 * RET_WIDTH // PROJ_TN,
                             q_tiles=RET_WIDTH // PROJ_TN, cast_steps=cast_steps)
    whole = lambda a: pl.BlockSpec(a.shape, lambda i, j: (0, 0))
    slab = lambda i, j: jnp.minimum(i * nj + j, cast_steps - 1)
    return pl.pallas_call(
        kern,
        grid=(mp // tm, nj),
        in_specs=[
            pl.BlockSpec((tm, D_MODEL), lambda i, j: (i, 0), pipeline_mode=pl.Buffered(1)),
            whole(xs_bf),
            pl.BlockSpec((None, D_MODEL, PROJ_TN), lambda i, j: (li, 0, j)),
            pl.BlockSpec((tm, HALF_HEAD), lambda i, j: (i % t_tiles, 0)),
            pl.BlockSpec((tm, HALF_HEAD), lambda i, j: (i % t_tiles, 0)),
            whole(cos_s), whole(sin_s),
            pl.BlockSpec((None, CAST_ROWS, D_MODEL), lambda i, j: (li, slab(i, j), 0)),
        ],
        out_specs=[
            pl.BlockSpec((tm, PROJ_TN), lambda i, j: (i, j)),
            pl.BlockSpec((ns, PROJ_TN), lambda i, j: (0, jnp.where(i == 0, j, nj - 1))),
            pl.BlockSpec((CAST_ROWS, D_MODEL), lambda i, j: (slab(i, j), 0)),
        ],
        out_shape=[jax.ShapeDtypeStruct((mp, IN_WIDTH), F32),
                   jax.ShapeDtypeStruct((ns, IN_WIDTH), F32),
                   jax.ShapeDtypeStruct((D_MODEL, D_MODEL), BF16)],
        compiler_params=_params("arbitrary", "arbitrary"),
        name="proj",
    )(xp_bf, xs_bf, w_in, cos_p, sin_p, cos_s, sin_s, w_o)


def _group_norm_gate(o, g):
    mu = jnp.mean(o, axis=-1, keepdims=True)
    d = o - mu
    var = jnp.mean(d * d, axis=-1, keepdims=True)
    return d * lax.rsqrt(var + LN_EPS) * (g * jax.nn.sigmoid(g))


def _ret_prompt_kernel(q_ref, k_ref, v_ref, g_ref, decay_ref, xi_ref, zeta_ref, gl_ref,
                       o_ref, snew_ref, s_scr):
    c = pl.program_id(1)

    @pl.when(c == 0)
    def _():
        s_scr[...] = jnp.zeros_like(s_scr)

    for h in range(RET_HEADS):
        cols = pl.ds(h * HEAD_DIM, HEAD_DIM)
        q = q_ref[:, cols]
        k = k_ref[:, cols]
        vb = v_ref[:, cols].astype(BF16)
        s = s_scr[h]
        scores = lax.dot_general(q.astype(BF16), k.astype(BF16), (((1,), (1,)), ((), ())),
                                 preferred_element_type=F32) * decay_ref[h]
        o = jnp.dot(scores.astype(BF16), vb, preferred_element_type=F32)
        o = o + jnp.dot((q * xi_ref[h]).astype(BF16), s.astype(BF16), preferred_element_type=F32)
        kz_t = (k * zeta_ref[h]).T.astype(BF16)
        s_scr[h] = gl_ref[h] * s + jnp.dot(kz_t, vb, preferred_element_type=F32)
        o_ref[:, cols] = _group_norm_gate(o, g_ref[:, cols]).astype(BF16)

    @pl.when(c == pl.num_programs(1) - 1)
    def _():
        snew_ref[0] = s_scr[...]


def _ret_prompt(proj, decay, xi, zeta, gl, batch, seq):
    nc = seq // RET_CHUNK

    def col(cb):
        return pl.BlockSpec((RET_CHUNK, RET_WIDTH), lambda b, c: (b * nc + c, cb))

    def whole(a):
        return pl.BlockSpec(a.shape, lambda b, c: (0,) * a.ndim)

    return pl.pallas_call(
        _ret_prompt_kernel,
        grid=(batch, nc),
        in_specs=[col(0), col(1), col(2), col(3), whole(decay), whole(xi), whole(zeta), whole(gl)],
        out_specs=[
            pl.BlockSpec((RET_CHUNK, RET_WIDTH), lambda b, c: (b * nc + c, 0)),
            pl.BlockSpec((1, RET_HEADS, HEAD_DIM, HEAD_DIM), lambda b, c: (b, 0, 0, 0)),
        ],
        out_shape=[
            jax.ShapeDtypeStruct((batch * seq, RET_WIDTH), BF16),
            jax.ShapeDtypeStruct((batch, RET_HEADS, HEAD_DIM, HEAD_DIM), F32),
        ],
        scratch_shapes=[pltpu.VMEM((RET_HEADS, HEAD_DIM, HEAD_DIM), F32)],
        compiler_params=_params("parallel", "arbitrary"),
        name="ret_prompt",
    )(proj, proj, proj, proj, decay, xi, zeta, gl)


def _ret_sample_kernel(q_ref, k_ref, v_ref, g_ref, s0_ref, gam_ref, *rest):
    o_ref, snew_ref = rest[-2:]
    q = q_ref[...]
    k = k_ref[...]
    v = v_ref[...]
    vb = v.astype(BF16)
    gam = gam_ref[0]
    qg = (q * gam).astype(BF16)
    rows = lax.broadcasted_iota(jnp.int32, (SAMPLE_NB, 1), 0)
    o_state = jnp.zeros((SAMPLE_NB, HEAD_DIM), F32)
    for n in range(SAMPLE_NB):
        s = s0_ref[n, 0]
        r = jnp.dot(qg, s.astype(BF16), preferred_element_type=F32)
        o_state = jnp.where(rows == n, r, o_state)
        k_n_t = jnp.where(rows == n, k, 0.0).T.astype(BF16)
        snew_ref[n, 0] = gam * s + jnp.dot(k_n_t, vb, preferred_element_type=F32)
    qk = jnp.sum(q * k, axis=-1, keepdims=True)
    o = qk * v + o_state
    o_ref[...] = _group_norm_gate(o, g_ref[...]).astype(BF16)


def _ret_sample(proj, state_all, li, gam, new_state_all):
    n = proj.shape[0]

    def col(cb):
        return pl.BlockSpec((SAMPLE_NB, HEAD_DIM), lambda i, h: (i, cb * RET_HEADS + h))

    state_spec = pl.BlockSpec((None, SAMPLE_NB, 1, HEAD_DIM, HEAD_DIM),
                              lambda i, h: (li, i, h, 0, 0))
    in_specs = [col(0), col(1), col(2), col(3), state_spec,
                pl.BlockSpec((1, 1, HEAD_DIM), lambda i, h: (h, 0, 0))]
    args = [proj, proj, proj, proj, state_all, gam]
    aliases = {}
    if new_state_all is not None:
        in_specs.append(pl.BlockSpec(memory_space=pl.ANY))
        args.append(new_state_all)
        aliases = {len(args) - 1: 1}
    return pl.pallas_call(
        _ret_sample_kernel,
        grid=(n // SAMPLE_NB, RET_HEADS),
        in_specs=in_specs,
        out_specs=[pl.BlockSpec((SAMPLE_NB, HEAD_DIM), lambda i, h: (i, h)), state_spec],
        out_shape=[jax.ShapeDtypeStruct((n, RET_WIDTH), BF16),
                   jax.ShapeDtypeStruct(state_all.shape, F32)],
        input_output_aliases=aliases,
        compiler_params=_params("parallel", "arbitrary"),
        name="ret_sample",
    )(*args)


def _pool_prompt_kernel(u_ref, halo_ref, wp_ref, sc_ref, o_ref, ext_scr, *, tiles_per_seq):
    i = pl.program_id(0)
    tm = u_ref.shape[0]
    t0 = (i % tiles_per_seq) * tm
    ext_scr[0:HALO_ROWS, :] = jnp.where(t0 == 0, 0.0, halo_ref[...])
    ext_scr[HALO_ROWS:HALO_ROWS + tm, :] = u_ref[...]
    pos = t0 + lax.broadcasted_iota(jnp.int32, (tm, 1), 0)
    for gi, w in enumerate(POOL_WINDOWS):
        cols = pl.ds(gi * POOL_GROUP_DIM, POOL_GROUP_DIM)
        u = u_ref[:, cols]
        s = u
        for back in range(1, w):
            s = s + ext_scr[pl.ds(HALO_ROWS - back, tm), cols]
        cnt = jnp.minimum(w, pos + 1).astype(F32)
        pooled = s / cnt - u
        mixed = jnp.dot(pooled.astype(BF16), wp_ref[gi], preferred_element_type=F32)
        o_ref[:, cols] = (mixed * sc_ref[:, cols]).astype(BF16)


def _pool_prompt(proj, w_pool_bf, scale, seq):
    m = proj.shape[0]
    tiles_per_seq = seq // POOL_TM
    u_col = 4 * RET_WIDTH // POOL_WIDTH
    halo_per_tile = POOL_TM // HALO_ROWS
    kern = functools.partial(_pool_prompt_kernel, tiles_per_seq=tiles_per_seq)
    return pl.pallas_call(
        kern,
        grid=(m // POOL_TM,),
        in_specs=[
            pl.BlockSpec((POOL_TM, POOL_WIDTH), lambda i: (i, u_col)),
            pl.BlockSpec((HALO_ROWS, POOL_WIDTH),
                         lambda i: (jnp.maximum(i * halo_per_tile - 1, 0), u_col)),
            pl.BlockSpec(w_pool_bf.shape, lambda i: (0, 0, 0)),
            pl.BlockSpec((1, POOL_WIDTH), lambda i: (0, 0)),
        ],
        out_specs=pl.BlockSpec((POOL_TM, POOL_WIDTH), lambda i: (i, 0)),
        out_shape=jax.ShapeDtypeStruct((m, POOL_WIDTH), BF16),
        scratch_shapes=[pltpu.VMEM((HALO_ROWS + POOL_TM, POOL_WIDTH), F32)],
        compiler_params=_params("parallel"),
        name="pool_prompt",
    )(proj, proj, w_pool_bf, scale)


def _pool_sample_kernel(u_ref, buf_ref, wp_ref, sc_ref, o_ref):
    row = lax.broadcasted_iota(jnp.int32, (1, POOL_BUF, 1), 1)
    for gi, w in enumerate(POOL_WINDOWS):
        cols = pl.ds(gi * POOL_GROUP_DIM, POOL_GROUP_DIM)
        u = u_ref[:, cols]
        past = jnp.where(row >= POOL_BUF - (w - 1), buf_ref[:, :, cols], 0.0)
        s = u + jnp.sum(past, axis=1)
        pooled = s / float(w) - u
        mixed = jnp.dot(pooled.astype(BF16), wp_ref[gi], preferred_element_type=F32)
        o_ref[:, cols] = (mixed * sc_ref[:, cols]).astype(BF16)


def _pool_sample(proj, buf_all, li, w_pool_bf, scale):
    n = proj.shape[0]
    u_col = 4 * RET_WIDTH // POOL_WIDTH
    return pl.pallas_call(
        _pool_sample_kernel,
        grid=(n // SAMPLE_NB,),
        in_specs=[
            pl.BlockSpec((SAMPLE_NB, POOL_WIDTH), lambda i: (i, u_col)),
            pl.BlockSpec((None, SAMPLE_NB, POOL_BUF, POOL_WIDTH), lambda i: (li, i, 0, 0)),
            pl.BlockSpec(w_pool_bf.shape, lambda i: (0, 0, 0)),
            pl.BlockSpec((1, POOL_WIDTH), lambda i: (0, 0)),
        ],
        out_specs=pl.BlockSpec((SAMPLE_NB, POOL_WIDTH), lambda i: (i, 0)),
        out_shape=jax.ShapeDtypeStruct((n, POOL_WIDTH), BF16),
        compiler_params=_params("parallel"),
        name="pool_sample",
    )(proj, buf_all, w_pool_bf, scale)


def _out_proj_kernel(ret_ref, pool_ref, wr_ref, wp_ref, x_ref, g_ref, b_ref, y_ref, ybf_ref):
    j = pl.program_id(1)
    tn = wr_ref.shape[1]
    cols = pl.ds(pl.multiple_of(j * tn, tn), tn)
    mix = jnp.dot(ret_ref[...], wr_ref[...], preferred_element_type=F32)
    mix = mix + jnp.dot(pool_ref[...], wp_ref[...], preferred_element_type=F32)
    y_ref[:, cols] = DN_ALPHA * x_ref[...] + mix

    @pl.when(j == pl.num_programs(1) - 1)
    def _():
        _layer_norm_rows(y_ref, ybf_ref, g_ref, b_ref)


def _out_proj(ret, pool, w_o_bf, x, g, b, tm):
    m = x.shape[0]
    return pl.pallas_call(
        _out_proj_kernel,
        grid=(m // tm, D_MODEL // OUT_TN),
        in_specs=[
            pl.BlockSpec((tm, RET_WIDTH), lambda i, j: (i, 0)),
            pl.BlockSpec((tm, POOL_WIDTH), lambda i, j: (i, 0)),
            pl.BlockSpec((RET_WIDTH, OUT_TN), lambda i, j: (0, j)),
            pl.BlockSpec((POOL_WIDTH, OUT_TN), lambda i, j: (1, j)),
            pl.BlockSpec((tm, OUT_TN), lambda i, j: (i, j)),
            pl.BlockSpec((1, D_MODEL), lambda i, j: (0, 0)),
            pl.BlockSpec((1, D_MODEL), lambda i, j: (0, 0)),
        ],
        out_specs=[pl.BlockSpec((tm, D_MODEL), lambda i, j: (i, 0)),
                   pl.BlockSpec((tm, D_MODEL), lambda i, j: (i, 0))],
        out_shape=[jax.ShapeDtypeStruct((m, D_MODEL), F32),
                   jax.ShapeDtypeStruct((m, D_MODEL), BF16)],
        compiler_params=_params("parallel", "arbitrary"),
        name="out_proj",
    )(ret, pool, w_o_bf, w_o_bf, x, g, b)


def _ffn_up_kernel(xp_ref, xs_ref, wa_ref, wb_ref, cw_ref, cb_ref, s0_ref, s1_ref, wd_ref,
                   hp_ref, tail_ref, hs_ref, n0_ref, n1_ref, wdbf_ref, ext_scr, w_scr):
    i = pl.program_id(0)
    tm = xp_ref.shape[0]
    tn = wa_ref.shape[1]
    w_scr[:, 0:tn] = wa_ref[...].astype(BF16)
    w_scr[:, tn:2 * tn] = wb_ref[...].astype(BF16)
    w = w_scr[...]
    cw0 = cw_ref[0:1, :]
    cw1 = cw_ref[1:2, :]
    cw2 = cw_ref[2:3, :]
    cb = cb_ref[...]

    wdbf_ref[...] = wd_ref[...].astype(BF16)

    ext_scr[0:SUBLANES, :] = jnp.zeros((SUBLANES, ext_scr.shape[1]), F32)
    for r in range(tm // DOT_ROWS):
        r0 = r * DOT_ROWS
        ab = jnp.dot(xp_ref[pl.ds(r0, DOT_ROWS), :], w, preferred_element_type=F32)
        ext_scr[pl.ds(SUBLANES + r0, DOT_ROWS), :] = ab[:, 0:tn]
        conv = cb + ext_scr[pl.ds(SUBLANES + r0 - 2, DOT_ROWS), :] * cw0
        conv = conv + ext_scr[pl.ds(SUBLANES + r0 - 1, DOT_ROWS), :] * cw1
        conv = conv + ext_scr[pl.ds(SUBLANES + r0, DOT_ROWS), :] * cw2
        hp_ref[pl.ds(r0, DOT_ROWS), :] = (_gelu_exact(conv) * ab[:, tn:2 * tn]).astype(BF16)
    tail_ref[0] = ext_scr[tm:tm + SUBLANES, :]

    @pl.when(i == 0)
    def _():
        ab_s = jnp.dot(xs_ref[...], w_scr[...], preferred_element_type=F32)
        a_s = ab_s[:, 0:tn]
        b_s = ab_s[:, tn:2 * tn]
        s1 = s1_ref[...]
        conv_s = cb + s0_ref[...] * cw0
        conv_s = conv_s + s1 * cw1
        conv_s = conv_s + a_s * cw2
        hs_ref[...] = (_gelu_exact(conv_s) * b_s).astype(BF16)
        n0_ref[...] = s1
        n1_ref[...] = a_s


def _ffn_up(xp_bf, xs_bf, w_up, conv_w, conv_b, state2d, w_down, li, seq):
    mp = xp_bf.shape[0]
    ns = xs_bf.shape[0]
    nj = D_FF // FFN_TN
    tm = seq
    assert (mp // tm) * nj * CAST_ROWS == D_FF
    first = lambda i, j: jnp.where(i == 0, j, nj - 1)
    s_tile = pl.BlockSpec((ns, FFN_TN), lambda i, j: (0, first(i, j)))
    return pl.pallas_call(
        _ffn_up_kernel,
        grid=(mp // tm, nj),
        in_specs=[
            pl.BlockSpec((tm, D_MODEL), lambda i, j: (i, 0), pipeline_mode=pl.Buffered(1)),
            pl.BlockSpec((ns, D_MODEL), lambda i, j: (0, 0)),
            pl.BlockSpec((None, D_MODEL, FFN_TN), lambda i, j: (li, 0, j)),
            pl.BlockSpec((None, D_MODEL, FFN_TN), lambda i, j: (li, 0, nj + j)),
            pl.BlockSpec((None, CONV_WIDTH, FFN_TN), lambda i, j: (li, 0, j)),
            pl.BlockSpec((None, 1, FFN_TN), lambda i, j: (li, 0, j)),
            pl.BlockSpec((None, ns, FFN_TN), lambda i, j: (li, 0, first(i, j))),
            pl.BlockSpec((None, ns, FFN_TN), lambda i, j: (li, 0, nj + first(i, j))),
            pl.BlockSpec((None, CAST_ROWS, D_MODEL), lambda i, j: (li, i * nj + j, 0)),
        ],
        out_specs=[
            pl.BlockSpec((tm, FFN_TN), lambda i, j: (i, j)),
            pl.BlockSpec((1, SUBLANES, FFN_TN), lambda i, j: (i, 0, j)),
            s_tile, s_tile, s_tile,
            pl.BlockSpec((CAST_ROWS, D_MODEL), lambda i, j: (i * nj + j, 0)),
        ],
        out_shape=[jax.ShapeDtypeStruct((mp, D_FF), BF16),
                   jax.ShapeDtypeStruct((mp // tm, SUBLANES, D_FF), F32),
                   jax.ShapeDtypeStruct((ns, D_FF), BF16),
                   jax.ShapeDtypeStruct((ns, D_FF), F32),
                   jax.ShapeDtypeStruct((ns, D_FF), F32),
                   jax.ShapeDtypeStruct((D_FF, D_MODEL), BF16)],
        scratch_shapes=[pltpu.VMEM((SUBLANES + tm, FFN_TN), F32),
                        pltpu.VMEM((D_MODEL, 2 * FFN_TN), BF16)],
        compiler_params=_params("arbitrary", "arbitrary"),
        name="ffn_up",
    )(xp_bf, xs_bf, w_up, w_up, conv_w, conv_b, state2d, state2d, w_down)


def _ffn_down_kernel(h_ref, w_ref, x_ref, g_ref, b_ref, y_ref, ybf_ref):
    k = pl.program_id(1)
    j = pl.program_id(2)
    tn = w_ref.shape[1]
    cols = pl.ds(pl.multiple_of(j * tn, tn), tn)
    part = jnp.dot(h_ref[...], w_ref[...], preferred_element_type=F32)

    @pl.when(k == 0)
    def _():
        y_ref[:, cols] = DN_ALPHA * x_ref[...] + part

    @pl.when(k > 0)
    def _():
        y_ref[:, cols] = y_ref[:, cols] + part

    @pl.when((k == pl.num_programs(1) - 1) & (j == pl.num_programs(2) - 1))
    def _():
        _layer_norm_rows(y_ref, ybf_ref, g_ref, b_ref)


def _ffn_down(h, w_down_bf, x, g, b, tm):
    m = x.shape[0]
    nk = D_FF // DOWN_TK
    nj = D_MODEL // DOWN_TN
    return pl.pallas_call(
        _ffn_down_kernel,
        grid=(m // tm, nk, nj),
        in_specs=[
            pl.BlockSpec((tm, DOWN_TK), lambda i, k, j: (i, k)),
            pl.BlockSpec((DOWN_TK, DOWN_TN), lambda i, k, j: (k, j)),
            pl.BlockSpec((tm, DOWN_TN), lambda i, k, j: (i, jnp.where(k == 0, j, nj - 1))),
            pl.BlockSpec((1, D_MODEL), lambda i, k, j: (0, 0)),
            pl.BlockSpec((1, D_MODEL), lambda i, k, j: (0, 0)),
        ],
        out_specs=[pl.BlockSpec((tm, D_MODEL), lambda i, k, j: (i, 0)),
                   pl.BlockSpec((tm, D_MODEL), lambda i, k, j: (i, 0))],
        out_shape=[jax.ShapeDtypeStruct((m, D_MODEL), F32),
                   jax.ShapeDtypeStruct((m, D_MODEL), BF16)],
        compiler_params=_params("parallel", "arbitrary", "arbitrary"),
        name="ffn_down",
    )(h, w_down_bf, x, g, b)


def _rotary_tables(pos):
    inv = ROPE_BASE ** (-jnp.arange(HALF_HEAD, dtype=F32) / HALF_HEAD)
    ang = pos.astype(F32)[:, None] * inv[None, :]
    return jnp.cos(ang), jnp.sin(ang)


def _retention_tables(l):
    log_g = jnp.log1p(-(2.0 ** (-5.0 - jnp.arange(RET_HEADS, dtype=F32))))
    i = jnp.arange(l)
    diff = i[:, None] - i[None, :]
    decay = jnp.where(diff[None] >= 0,
                      jnp.exp(jnp.maximum(diff, 0)[None].astype(F32) * log_g[:, None, None]), 0.0)
    xi = jnp.exp((i + 1)[None].astype(F32) * log_g[:, None])
    zeta = jnp.exp((l - 1 - i)[None].astype(F32) * log_g[:, None])
    g_l = jnp.exp(l * log_g)
    return decay, xi, zeta, g_l


def kernel(x_prompt, x_sample, state_ret, state_pool, state_conv, w_in, w_pool, pool_scale, w_o,
           ln1_g, ln1_b, w_up, conv_w, conv_b, w_down, ln2_g, ln2_b):
    batch, seq, _ = x_prompt.shape
    n_s = x_sample.shape[0]
    assert x_sample.shape[1] == 1 and seq % RET_CHUNK == 0
    mp = batch * seq

    cos_p, sin_p = _rotary_tables(jnp.arange(seq))
    cos_s, sin_s = _rotary_tables(PAST_LEN + jnp.arange(1))
    cos_s = jnp.broadcast_to(cos_s, (n_s, HALF_HEAD))
    sin_s = jnp.broadcast_to(sin_s, (n_s, HALF_HEAD))
    decay, xi, zeta, g_l = _retention_tables(RET_CHUNK)
    xi = xi[:, :, None]
    zeta = zeta[:, :, None]
    gl_b = jnp.broadcast_to(g_l[:, None, None], (RET_HEADS, 1, HEAD_DIM))
    _, xi_s, _, _ = _retention_tables(1)
    gam_s = jnp.broadcast_to(xi_s[:, :, None], (RET_HEADS, 1, HEAD_DIM))

    xp = x_prompt.reshape(mp, D_MODEL)
    xs = x_sample.reshape(n_s, D_MODEL)
    xp_bf = xp.astype(BF16)
    xs_bf = xs.astype(BF16)
    conv_state2d = state_conv.reshape(DEPTH, n_s, (CONV_WIDTH - 1) * D_FF)
    conv_b3 = conv_b[:, None, :]

    ret_p, pool_p, pool_s, conv_p, conv_s = [], [], [], [], []
    new_ret_sample = None
    for li in range(DEPTH):
        w_pool_bf = w_pool[li].astype(BF16)
        scale = pool_scale[li][None, :]
        g1, b1 = ln1_g[li][None, :], ln1_b[li][None, :]
        g2, b2 = ln2_g[li][None, :], ln2_b[li][None, :]

        proj_p, proj_s, w_o_bf = _proj(xp_bf, xs_bf, w_in, w_o, li, cos_p, sin_p, cos_s, sin_s, seq)

        ret_out, s_new = _ret_prompt(proj_p, decay, xi, zeta, gl_b, batch, seq)
        pool_out = _pool_prompt(proj_p, w_pool_bf, scale, seq)
        xp, xp_bf = _out_proj(ret_out, pool_out, w_o_bf, xp, g1, b1, LN_TM)
        ret_p.append(s_new)
        pool_p.append(proj_p.reshape(batch, seq, IN_WIDTH)[:, seq - POOL_BUF:, 4 * RET_WIDTH:])

        ret_out, new_ret_sample = _ret_sample(proj_s, state_ret, li, gam_s, new_ret_sample)
        pool_out = _pool_sample(proj_s, state_pool, li, w_pool_bf, scale)
        xs, xs_bf = _out_proj(ret_out, pool_out, w_o_bf, xs, g1, b1, n_s)
        pool_s.append(jnp.concatenate([state_pool[li][:, 1:], proj_s[:, None, 4 * RET_WIDTH:]], 1))

        h_p, tail, h_s, c0, c1, w_down_bf = _ffn_up(xp_bf, xs_bf, w_up, conv_w, conv_b3,
                                                    conv_state2d, w_down, li, seq)
        xp, xp_bf = _ffn_down(h_p, w_down_bf, xp, g2, b2, LN_TM)
        xs, xs_bf = _ffn_down(h_s, w_down_bf, xs, g2, b2, n_s)
        conv_p.append(tail[:, SUBLANES - (CONV_WIDTH - 1):, :])
        conv_s.append(jnp.stack([c0, c1], 1))

    return (xp.reshape(batch, seq, D_MODEL), xs.reshape(n_s, 1, D_MODEL),
            jnp.stack(ret_p), new_ret_sample, jnp.stack(pool_p), jnp.stack(pool_s),
            jnp.stack(conv_p), jnp.stack(conv_s))
```

```python
import functools

import jax
import jax.numpy as jnp
from jax import lax
from jax.experimental import pallas as pl
from jax.experimental.pallas import tpu as pltpu

F32 = jnp.float32
BF16 = jnp.bfloat16

D_MODEL = 4096
DEPTH = 2
PAST_LEN = 16384
RET_WIDTH = D_MODEL // 2
RET_HEADS = 8
HEAD_DIM = RET_WIDTH // RET_HEADS
HALF_HEAD = HEAD_DIM // 2
RET_CHUNK = 128
ROPE_BASE = 10000.0
POOL_WIDTH = D_MODEL - RET_WIDTH
POOL_WINDOWS = (2, 4, 8, 16)
POOL_GROUP_DIM = POOL_WIDTH // len(POOL_WINDOWS)
POOL_BUF = max(POOL_WINDOWS) - 1
IN_WIDTH = 4 * RET_WIDTH + POOL_WIDTH
D_FF = ((8 * D_MODEL // 3) + 255) // 256 * 256
CONV_WIDTH = 3
DN_ALPHA = (2.0 * DEPTH) ** 0.25
LN_EPS = 1e-5
K_SCALE = HEAD_DIM ** -0.5

V7X_VMEM_BYTES = 64 * 1024 * 1024
VMEM_LIMIT = V7X_VMEM_BYTES - 8 * 1024 * 1024
SUBLANES = 8
HALO_ROWS = 2 * SUBLANES

PROJ_TN = 512
FFN_TN = 256
LN_TM = 512
OUT_TN = 1024
SUB_TN = 512
LN_ROWS = 64
DOWN_TK = D_FF // 2
DOWN_TN = 1024
POOL_TM = 512
SAMPLE_NB = 16
DOT_ROWS = 1024
CAST_ROWS = 64


def _params(*semantics):
    return pltpu.CompilerParams(dimension_semantics=semantics, vmem_limit_bytes=VMEM_LIMIT)


def _layer_norm(y, g, b):
    mu = jnp.mean(y, axis=-1, keepdims=True)
    d = y - mu
    var = jnp.mean(d * d, axis=-1, keepdims=True)
    return d * lax.rsqrt(var + LN_EPS) * g + b


def _layer_norm_store(acc_ref, g_ref, b_ref, y_hbm, ybf_hbm, stage_f, stage_b, sems, row0):
    g = g_ref[...]
    b = b_ref[...]
    n = acc_ref.shape[0] // LN_ROWS

    def copies(r, slot):
        dst = pl.ds(pl.multiple_of(row0 + r * LN_ROWS, LN_ROWS), LN_ROWS)
        return (pltpu.make_async_copy(stage_f.at[slot], y_hbm.at[dst, :], sems.at[0, slot]),
                pltpu.make_async_copy(stage_b.at[slot], ybf_hbm.at[dst, :], sems.at[1, slot]))

    def body(r, carry):
        slot = r % 2

        @pl.when(r >= 2)
        def _():
            for c in copies(r - 2, slot):
                c.wait()

        rows = pl.ds(pl.multiple_of(r * LN_ROWS, LN_ROWS), LN_ROWS)
        y = _layer_norm(acc_ref[rows, :], g, b)
        stage_f[slot] = y
        stage_b[slot] = y.astype(BF16)
        for c in copies(r, slot):
            c.start()
        return carry

    lax.fori_loop(0, n, body, 0)
    for r in range(max(n - 2, 0), n):
        for c in copies(r, r % 2):
            c.wait()


def _ln_scratch():
    return [pltpu.VMEM((2, LN_ROWS, D_MODEL), F32), pltpu.VMEM((2, LN_ROWS, D_MODEL), BF16),
            pltpu.SemaphoreType.DMA((2, 2))]


def _gelu_exact(x):
    return 0.5 * x * (1.0 + lax.erf(x * (0.5 ** 0.5)))


def _rotate_heads(o_ref, cos_ref, sin_ref, scale):
    cos = cos_ref[...]
    sin = sin_ref[...]
    for h in range(o_ref.shape[1] // HEAD_DIM):
        lo = pl.ds(h * HEAD_DIM, HALF_HEAD)
        hi = pl.ds(h * HEAD_DIM + HALF_HEAD, HALF_HEAD)
        x1 = o_ref[:, lo]
        x2 = o_ref[:, hi]
        o_ref[:, lo] = (x1 * cos - x2 * sin) * scale
        o_ref[:, hi] = (x2 * cos + x1 * sin) * scale


def _proj_kernel(xp_ref, xs_ref, w_ref, cosp_ref, sinp_ref, coss_ref, sins_ref, wo_ref,
                 op_ref, os_ref, wobf_ref, *, rot_tiles, q_tiles, cast_steps):
    i = pl.program_id(0)
    j = pl.program_id(1)
    w = w_ref[...].astype(BF16)
    scale = jnp.where(j >= q_tiles, K_SCALE, 1.0).astype(F32)
    for r in range(xp_ref.shape[0] // DOT_ROWS):
        rows = pl.ds(r * DOT_ROWS, DOT_ROWS)
        op_ref[rows, :] = jnp.dot(xp_ref[rows, :], w, preferred_element_type=F32)

    @pl.when(i * pl.num_programs(1) + j < cast_steps)
    def _():
        wobf_ref[...] = wo_ref[...].astype(BF16)

    @pl.when(j < rot_tiles)
    def _():
        _rotate_heads(op_ref, cosp_ref, sinp_ref, scale)

    @pl.when(i == 0)
    def _():
        os_ref[...] = jnp.dot(xs_ref[...], w, preferred_element_type=F32)

    @pl.when((i == 0) & (j < rot_tiles))
    def _():
        _rotate_heads(os_ref, coss_ref, sins_ref, scale)


def _proj(xp_bf, xs_bf, w_in, w_o, li, cos_p, sin_p, cos_s, sin_s, tm):
    mp = xp_bf.shape[0]
    ns = xs_bf.shape[0]
    nj = IN_WIDTH // PROJ_TN
    t_tiles = cos_p.shape[0] // tm
    cast_steps = D_MODEL // CAST_ROWS
    assert cast_steps <= (mp // tm) * nj
    kern = functools.partial(_proj_kernel, rot_tiles=2 * RET_WIDTH // PROJ_TN,
                             q_tiles=RET_WIDTH // PROJ_TN, cast_steps=cast_steps)
    whole = lambda a: pl.BlockSpec(a.shape, lambda i, j: (0, 0))
    slab = lambda i, j: jnp.minimum(i * nj + j, cast_steps - 1)
    return pl.pallas_call(
        kern,
        grid=(mp // tm, nj),
        in_specs=[
            pl.BlockSpec((tm, D_MODEL), lambda i, j: (i, 0), pipeline_mode=pl.Buffered(1)),
            whole(xs_bf),
            pl.BlockSpec((None, D_MODEL, PROJ_TN), lambda i, j: (li, 0, j)),
            pl.BlockSpec((tm, HALF_HEAD), lambda i, j: (i % t_tiles, 0)),
            pl.BlockSpec((tm, HALF_HEAD), lambda i, j: (i % t_tiles, 0)),
            whole(cos_s), whole(sin_s),
            pl.BlockSpec((None, CAST_ROWS, D_MODEL), lambda i, j: (li, slab(i, j), 0)),
        ],
        out_specs=[
            pl.BlockSpec((tm, PROJ_TN), lambda i, j: (i, j)),
            pl.BlockSpec((ns, PROJ_TN), lambda i, j: (0, jnp.where(i == 0, j, nj - 1))),
            pl.BlockSpec((CAST_ROWS, D_MODEL), lambda i, j: (slab(i, j), 0)),
        ],
        out_shape=[jax.ShapeDtypeStruct((mp, IN_WIDTH), F32),
                   jax.ShapeDtypeStruct((ns, IN_WIDTH), F32),
                   jax.ShapeDtypeStruct((D_MODEL, D_MODEL), BF16)],
        compiler_params=_params("arbitrary", "arbitrary"),
        name="proj",
    )(xp_bf, xs_bf, w_in, cos_p, sin_p, cos_s, sin_s, w_o)


def _group_norm_gate(o, g):
    mu = jnp.mean(o, axis=-1, keepdims=True)
    d = o - mu
    var = jnp.mean(d * d, axis=-1, keepdims=True)
    return d * lax.rsqrt(var + LN_EPS) * (g * jax.nn.sigmoid(g))


def _ret_prompt_kernel(q_ref, k_ref, v_ref, g_ref, decay_ref, xi_ref, zeta_ref, gl_ref,
                       o_ref, snew_ref, s_scr):
    c = pl.program_id(1)

    @pl.when(c == 0)
    def _():
        s_scr[...] = jnp.zeros_like(s_scr)

    for h in range(RET_HEADS):
        cols = pl.ds(h * HEAD_DIM, HEAD_DIM)
        q = q_ref[:, cols]
        k = k_ref[:, cols]
        vb = v_ref[:, cols].astype(BF16)
        s = s_scr[h]
        scores = lax.dot_general(q.astype(BF16), k.astype(BF16), (((1,), (1,)), ((), ())),
                                 preferred_element_type=F32) * decay_ref[h]
        o = jnp.dot(scores.astype(BF16), vb, preferred_element_type=F32)
        o = o + jnp.dot((q * xi_ref[h]).astype(BF16), s.astype(BF16), preferred_element_type=F32)
        kz_t = (k * zeta_ref[h]).T.astype(BF16)
        s_scr[h] = gl_ref[h] * s + jnp.dot(kz_t, vb, preferred_element_type=F32)
        o_ref[:, cols] = _group_norm_gate(o, g_ref[:, cols]).astype(BF16)

    @pl.when(c == pl.num_programs(1) - 1)
    def _():
        snew_ref[0] = s_scr[...]


def _ret_prompt(proj, decay, xi, zeta, gl, batch, seq):
    nc = seq // RET_CHUNK

    def col(cb):
        return pl.BlockSpec((RET_CHUNK, RET_WIDTH), lambda b, c: (b * nc + c, cb))

    def whole(a):
        return pl.BlockSpec(a.shape, lambda b, c: (0,) * a.ndim)

    return pl.pallas_call(
        _ret_prompt_kernel,
        grid=(batch, nc),
        in_specs=[col(0), col(1), col(2), col(3), whole(decay), whole(xi), whole(zeta), whole(gl)],
        out_specs=[
            pl.BlockSpec((RET_CHUNK, RET_WIDTH), lambda b, c: (b * nc + c, 0)),
            pl.BlockSpec((1, RET_HEADS, HEAD_DIM, HEAD_DIM), lambda b, c: (b, 0, 0, 0)),
        ],
        out_shape=[
            jax.ShapeDtypeStruct((batch * seq, D_MODEL), BF16),
            jax.ShapeDtypeStruct((batch, RET_HEADS, HEAD_DIM, HEAD_DIM), F32),
        ],
        scratch_shapes=[pltpu.VMEM((RET_HEADS, HEAD_DIM, HEAD_DIM), F32)],
        compiler_params=_params("parallel", "arbitrary"),
        name="ret_prompt",
    )(proj, proj, proj, proj, decay, xi, zeta, gl)


def _ret_sample_kernel(q_ref, k_ref, v_ref, g_ref, s0_ref, gam_ref, *rest):
    o_ref, snew_ref = rest[-2:]
    q = q_ref[...]
    k = k_ref[...]
    v = v_ref[...]
    vb = v.astype(BF16)
    gam = gam_ref[0]
    qg = (q * gam).astype(BF16)
    rows = lax.broadcasted_iota(jnp.int32, (SAMPLE_NB, 1), 0)
    o_state = jnp.zeros((SAMPLE_NB, HEAD_DIM), F32)
    for n in range(SAMPLE_NB):
        s = s0_ref[n, 0]
        r = jnp.dot(qg, s.astype(BF16), preferred_element_type=F32)
        o_state = jnp.where(rows == n, r, o_state)
        k_n_t = jnp.where(rows == n, k, 0.0).T.astype(BF16)
        snew_ref[n, 0] = gam * s + jnp.dot(k_n_t, vb, preferred_element_type=F32)
    qk = jnp.sum(q * k, axis=-1, keepdims=True)
    o = qk * v + o_state
    o_ref[...] = _group_norm_gate(o, g_ref[...]).astype(BF16)


def _ret_sample(proj, state_all, li, gam, new_state_all):
    n = proj.shape[0]

    def col(cb):
        return pl.BlockSpec((SAMPLE_NB, HEAD_DIM), lambda i, h: (i, cb * RET_HEADS + h))

    state_spec = pl.BlockSpec((None, SAMPLE_NB, 1, HEAD_DIM, HEAD_DIM),
                              lambda i, h: (li, i, h, 0, 0))
    in_specs = [col(0), col(1), col(2), col(3), state_spec,
                pl.BlockSpec((1, 1, HEAD_DIM), lambda i, h: (h, 0, 0))]
    args = [proj, proj, proj, proj, state_all, gam]
    aliases = {}
    if new_state_all is not None:
        in_specs.append(pl.BlockSpec(memory_space=pl.ANY))
        args.append(new_state_all)
        aliases = {len(args) - 1: 1}
    return pl.pallas_call(
        _ret_sample_kernel,
        grid=(n // SAMPLE_NB, RET_HEADS),
        in_specs=in_specs,
        out_specs=[pl.BlockSpec((SAMPLE_NB, HEAD_DIM), lambda i, h: (i, h)), state_spec],
        out_shape=[jax.ShapeDtypeStruct((n, D_MODEL), BF16),
                   jax.ShapeDtypeStruct(state_all.shape, F32)],
        input_output_aliases=aliases,
        compiler_params=_params("parallel", "arbitrary"),
        name="ret_sample",
    )(*args)


def _pool_prompt_kernel(u_ref, halo_ref, wp_ref, sc_ref, mix_hbm, o_ref, ext_scr, *,
                        tiles_per_seq):
    del mix_hbm
    i = pl.program_id(0)
    tm = u_ref.shape[0]
    t0 = (i % tiles_per_seq) * tm
    ext_scr[0:HALO_ROWS, :] = jnp.where(t0 == 0, 0.0, halo_ref[...])
    ext_scr[HALO_ROWS:HALO_ROWS + tm, :] = u_ref[...]
    pos = t0 + lax.broadcasted_iota(jnp.int32, (tm, 1), 0)
    for gi, w in enumerate(POOL_WINDOWS):
        cols = pl.ds(gi * POOL_GROUP_DIM, POOL_GROUP_DIM)
        u = u_ref[:, cols]
        s = u
        for back in range(1, w):
            s = s + ext_scr[pl.ds(HALO_ROWS - back, tm), cols]
        cnt = jnp.minimum(w, pos + 1).astype(F32)
        pooled = s / cnt - u
        mixed = jnp.dot(pooled.astype(BF16), wp_ref[gi], preferred_element_type=F32)
        o_ref[:, cols] = (mixed * sc_ref[:, cols]).astype(BF16)


def _pool_prompt(proj, w_pool_bf, scale, mix, seq):
    m = proj.shape[0]
    tiles_per_seq = seq // POOL_TM
    u_col = 4 * RET_WIDTH // POOL_WIDTH
    halo_per_tile = POOL_TM // HALO_ROWS
    kern = functools.partial(_pool_prompt_kernel, tiles_per_seq=tiles_per_seq)
    return pl.pallas_call(
        kern,
        grid=(m // POOL_TM,),
        in_specs=[
            pl.BlockSpec((POOL_TM, POOL_WIDTH), lambda i: (i, u_col)),
            pl.BlockSpec((HALO_ROWS, POOL_WIDTH),
                         lambda i: (jnp.maximum(i * halo_per_tile - 1, 0), u_col)),
            pl.BlockSpec(w_pool_bf.shape, lambda i: (0, 0, 0)),
            pl.BlockSpec((1, POOL_WIDTH), lambda i: (0, 0)),
            pl.BlockSpec(memory_space=pl.ANY),
        ],
        out_specs=pl.BlockSpec((POOL_TM, POOL_WIDTH), lambda i: (i, 1)),
        out_shape=jax.ShapeDtypeStruct(mix.shape, BF16),
        input_output_aliases={4: 0},
        scratch_shapes=[pltpu.VMEM((HALO_ROWS + POOL_TM, POOL_WIDTH), F32)],
        compiler_params=_params("parallel"),
        name="pool_prompt",
    )(proj, proj, w_pool_bf, scale, mix)


def _pool_sample_kernel(u_ref, buf_ref, wp_ref, sc_ref, mix_hbm, o_ref):
    del mix_hbm
    row = lax.broadcasted_iota(jnp.int32, (1, POOL_BUF, 1), 1)
    for gi, w in enumerate(POOL_WINDOWS):
        cols = pl.ds(gi * POOL_GROUP_DIM, POOL_GROUP_DIM)
        u = u_ref[:, cols]
        past = jnp.where(row >= POOL_BUF - (w - 1), buf_ref[:, :, cols], 0.0)
        s = u + jnp.sum(past, axis=1)
        pooled = s / float(w) - u
        mixed = jnp.dot(pooled.astype(BF16), wp_ref[gi], preferred_element_type=F32)
        o_ref[:, cols] = (mixed * sc_ref[:, cols]).astype(BF16)


def _pool_sample(proj, buf_all, li, w_pool_bf, scale, mix):
    n = proj.shape[0]
    u_col = 4 * RET_WIDTH // POOL_WIDTH
    return pl.pallas_call(
        _pool_sample_kernel,
        grid=(n // SAMPLE_NB,),
        in_specs=[
            pl.BlockSpec((SAMPLE_NB, POOL_WIDTH), lambda i: (i, u_col)),
            pl.BlockSpec((None, SAMPLE_NB, POOL_BUF, POOL_WIDTH), lambda i: (li, i, 0, 0)),
            pl.BlockSpec(w_pool_bf.shape, lambda i: (0, 0, 0)),
            pl.BlockSpec((1, POOL_WIDTH), lambda i: (0, 0)),
            pl.BlockSpec(memory_space=pl.ANY),
        ],
        out_specs=pl.BlockSpec((SAMPLE_NB, POOL_WIDTH), lambda i: (i, 1)),
        out_shape=jax.ShapeDtypeStruct(mix.shape, BF16),
        input_output_aliases={4: 0},
        compiler_params=_params("parallel"),
        name="pool_sample",
    )(proj, buf_all, w_pool_bf, scale, mix)


def _out_proj_kernel(mix_ref, w_ref, x_ref, g_ref, b_ref, y_hbm, ybf_hbm,
                     acc, stage_f, stage_b, sems):
    i = pl.program_id(0)
    j = pl.program_id(1)
    tm = mix_ref.shape[0]
    tn = w_ref.shape[1]
    for c in range(tn // SUB_TN):
        sub = pl.ds(c * SUB_TN, SUB_TN)
        cols = pl.ds(pl.multiple_of(j * tn + c * SUB_TN, SUB_TN), SUB_TN)
        mix = jnp.dot(mix_ref[...], w_ref[:, sub], preferred_element_type=F32)
        acc[:, cols] = DN_ALPHA * x_ref[:, sub] + mix

    @pl.when(j == pl.num_programs(1) - 1)
    def _():
        _layer_norm_store(acc, g_ref, b_ref, y_hbm, ybf_hbm, stage_f, stage_b, sems, i * tm)


def _out_proj(mix, w_o_bf, x, g, b, tm):
    m = x.shape[0]
    return pl.pallas_call(
        _out_proj_kernel,
        grid=(m // tm, D_MODEL // OUT_TN),
        in_specs=[
            pl.BlockSpec((tm, D_MODEL), lambda i, j: (i, 0)),
            pl.BlockSpec((D_MODEL, OUT_TN), lambda i, j: (0, j)),
            pl.BlockSpec((tm, OUT_TN), lambda i, j: (i, j)),
            pl.BlockSpec((1, D_MODEL), lambda i, j: (0, 0)),
            pl.BlockSpec((1, D_MODEL), lambda i, j: (0, 0)),
        ],
        out_specs=[pl.BlockSpec(memory_space=pl.ANY), pl.BlockSpec(memory_space=pl.ANY)],
        out_shape=[jax.ShapeDtypeStruct((m, D_MODEL), F32),
                   jax.ShapeDtypeStruct((m, D_MODEL), BF16)],
        scratch_shapes=[pltpu.VMEM((tm, D_MODEL), F32)] + _ln_scratch(),
        compiler_params=_params("arbitrary", "arbitrary"),
        name="out_proj",
    )(mix, w_o_bf, x, g, b)


def _ffn_up_kernel(xp_ref, xs_ref, wa_ref, wb_ref, cw_ref, cb_ref, s0_ref, s1_ref, wd_ref,
                   hp_ref, tail_ref, hs_ref, n0_ref, n1_ref, wdbf_ref, ext_scr, w_scr):
    i = pl.program_id(0)
    tm = xp_ref.shape[0]
    tn = wa_ref.shape[1]
    w_scr[:, 0:tn] = wa_ref[...].astype(BF16)
    w_scr[:, tn:2 * tn] = wb_ref[...].astype(BF16)
    w = w_scr[...]
    cw0 = cw_ref[0:1, :]
    cw1 = cw_ref[1:2, :]
    cw2 = cw_ref[2:3, :]
    cb = cb_ref[...]

    wdbf_ref[...] = wd_ref[...].astype(BF16)

    ext_scr[0:SUBLANES, :] = jnp.zeros((SUBLANES, ext_scr.shape[1]), F32)
    for r in range(tm // DOT_ROWS):
        r0 = r * DOT_ROWS
        ab = jnp.dot(xp_ref[pl.ds(r0, DOT_ROWS), :], w, preferred_element_type=F32)
        ext_scr[pl.ds(SUBLANES + r0, DOT_ROWS), :] = ab[:, 0:tn]
        conv = cb + ext_scr[pl.ds(SUBLANES + r0 - 2, DOT_ROWS), :] * cw0
        conv = conv + ext_scr[pl.ds(SUBLANES + r0 - 1, DOT_ROWS), :] * cw1
        conv = conv + ext_scr[pl.ds(SUBLANES + r0, DOT_ROWS), :] * cw2
        hp_ref[pl.ds(r0, DOT_ROWS), :] = (_gelu_exact(conv) * ab[:, tn:2 * tn]).astype(BF16)
    tail_ref[0] = ext_scr[tm:tm + SUBLANES, :]

    @pl.when(i == 0)
    def _():
        ab_s = jnp.dot(xs_ref[...], w_scr[...], preferred_element_type=F32)
        a_s = ab_s[:, 0:tn]
        b_s = ab_s[:, tn:2 * tn]
        s1 = s1_ref[...]
        conv_s = cb + s0_ref[...] * cw0
        conv_s = conv_s + s1 * cw1
        conv_s = conv_s + a_s * cw2
        hs_ref[...] = (_gelu_exact(conv_s) * b_s).astype(BF16)
        n0_ref[...] = s1
        n1_ref[...] = a_s


def _ffn_up(xp_bf, xs_bf, w_up, conv_w, conv_b, state2d, w_down, li, seq):
    mp = xp_bf.shape[0]
    ns = xs_bf.shape[0]
    nj = D_FF // FFN_TN
    tm = seq
    assert (mp // tm) * nj * CAST_ROWS == D_FF
    first = lambda i, j: jnp.where(i == 0, j, nj - 1)
    s_tile = pl.BlockSpec((ns, FFN_TN), lambda i, j: (0, first(i, j)))
    return pl.pallas_call(
        _ffn_up_kernel,
        grid=(mp // tm, nj),
        in_specs=[
            pl.BlockSpec((tm, D_MODEL), lambda i, j: (i, 0), pipeline_mode=pl.Buffered(1)),
            pl.BlockSpec((ns, D_MODEL), lambda i, j: (0, 0)),
            pl.BlockSpec((None, D_MODEL, FFN_TN), lambda i, j: (li, 0, j)),
            pl.BlockSpec((None, D_MODEL, FFN_TN), lambda i, j: (li, 0, nj + j)),
            pl.BlockSpec((None, CONV_WIDTH, FFN_TN), lambda i, j: (li, 0, j)),
            pl.BlockSpec((None, 1, FFN_TN), lambda i, j: (li, 0, j)),
            pl.BlockSpec((None, ns, FFN_TN), lambda i, j: (li, 0, first(i, j))),
            pl.BlockSpec((None, ns, FFN_TN), lambda i, j: (li, 0, nj + first(i, j))),
            pl.BlockSpec((None, CAST_ROWS, D_MODEL), lambda i, j: (li, i * nj + j, 0)),
        ],
        out_specs=[
            pl.BlockSpec((tm, FFN_TN), lambda i, j: (i, j)),
            pl.BlockSpec((1, SUBLANES, FFN_TN), lambda i, j: (i, 0, j)),
            s_tile, s_tile, s_tile,
            pl.BlockSpec((CAST_ROWS, D_MODEL), lambda i, j: (i * nj + j, 0)),
        ],
        out_shape=[jax.ShapeDtypeStruct((mp, D_FF), BF16),
                   jax.ShapeDtypeStruct((mp // tm, SUBLANES, D_FF), F32),
                   jax.ShapeDtypeStruct((ns, D_FF), BF16),
                   jax.ShapeDtypeStruct((ns, D_FF), F32),
                   jax.ShapeDtypeStruct((ns, D_FF), F32),
                   jax.ShapeDtypeStruct((D_FF, D_MODEL), BF16)],
        scratch_shapes=[pltpu.VMEM((SUBLANES + tm, FFN_TN), F32),
                        pltpu.VMEM((D_MODEL, 2 * FFN_TN), BF16)],
        compiler_params=_params("arbitrary", "arbitrary"),
        name="ffn_up",
    )(xp_bf, xs_bf, w_up, w_up, conv_w, conv_b, state2d, state2d, w_down)


def _ffn_down_kernel(h_ref, w_ref, x_ref, g_ref, b_ref, y_hbm, ybf_hbm,
                     acc, stage_f, stage_b, sems):
    i = pl.program_id(0)
    k = pl.program_id(1)
    j = pl.program_id(2)
    tm = h_ref.shape[0]
    tn = w_ref.shape[1]

    @pl.when((i == 0) & (k == 0) & (j == 0))
    def _():
        acc[...] = jnp.zeros_like(acc)

    for c in range(tn // SUB_TN):
        sub = pl.ds(c * SUB_TN, SUB_TN)
        cols = pl.ds(pl.multiple_of(j * tn + c * SUB_TN, SUB_TN), SUB_TN)
        part = jnp.dot(h_ref[...], w_ref[:, sub], preferred_element_type=F32)
        base = jnp.where(k == 0, DN_ALPHA * x_ref[:, sub], acc[:, cols])
        acc[:, cols] = base + part

    @pl.when((k == pl.num_programs(1) - 1) & (j == pl.num_programs(2) - 1))
    def _():
        _layer_norm_store(acc, g_ref, b_ref, y_hbm, ybf_hbm, stage_f, stage_b, sems, i * tm)


def _ffn_down(h, w_down_bf, x, g, b, tm):
    m = x.shape[0]
    nk = D_FF // DOWN_TK
    nj = D_MODEL // DOWN_TN
    return pl.pallas_call(
        _ffn_down_kernel,
        grid=(m // tm, nk, nj),
        in_specs=[
            pl.BlockSpec((tm, DOWN_TK), lambda i, k, j: (i, k)),
            pl.BlockSpec((DOWN_TK, DOWN_TN), lambda i, k, j: (k, j)),
            pl.BlockSpec((tm, DOWN_TN), lambda i, k, j: (i, jnp.where(k == 0, j, nj - 1))),
            pl.BlockSpec((1, D_MODEL), lambda i, k, j: (0, 0)),
            pl.BlockSpec((1, D_MODEL), lambda i, k, j: (0, 0)),
        ],
        out_specs=[pl.BlockSpec(memory_space=pl.ANY), pl.BlockSpec(memory_space=pl.ANY)],
        out_shape=[jax.ShapeDtypeStruct((m, D_MODEL), F32),
                   jax.ShapeDtypeStruct((m, D_MODEL), BF16)],
        scratch_shapes=[pltpu.VMEM((tm, D_MODEL), F32)] + _ln_scratch(),
        compiler_params=_params("arbitrary", "arbitrary", "arbitrary"),
        name="ffn_down",
    )(h, w_down_bf, x, g, b)


def _rotary_tables(pos):
    inv = ROPE_BASE ** (-jnp.arange(HALF_HEAD, dtype=F32) / HALF_HEAD)
    ang = pos.astype(F32)[:, None] * inv[None, :]
    return jnp.cos(ang), jnp.sin(ang)


def _retention_tables(l):
    log_g = jnp.log1p(-(2.0 ** (-5.0 - jnp.arange(RET_HEADS, dtype=F32))))
    i = jnp.arange(l)
    diff = i[:, None] - i[None, :]
    decay = jnp.where(diff[None] >= 0,
                      jnp.exp(jnp.maximum(diff, 0)[None].astype(F32) * log_g[:, None, None]), 0.0)
    xi = jnp.exp((i + 1)[None].astype(F32) * log_g[:, None])
    zeta = jnp.exp((l - 1 - i)[None].astype(F32) * log_g[:, None])
    g_l = jnp.exp(l * log_g)
    return decay, xi, zeta, g_l


def kernel(x_prompt, x_sample, state_ret, state_pool, state_conv, w_in, w_pool, pool_scale, w_o,
           ln1_g, ln1_b, w_up, conv_w, conv_b, w_down, ln2_g, ln2_b):
    batch, seq, _ = x_prompt.shape
    n_s = x_sample.shape[0]
    assert x_sample.shape[1] == 1 and seq % RET_CHUNK == 0
    mp = batch * seq

    cos_p, sin_p = _rotary_tables(jnp.arange(seq))
    cos_s, sin_s = _rotary_tables(PAST_LEN + jnp.arange(1))
    cos_s = jnp.broadcast_to(cos_s, (n_s, HALF_HEAD))
    sin_s = jnp.broadcast_to(sin_s, (n_s, HALF_HEAD))
    decay, xi, zeta, g_l = _retention_tables(RET_CHUNK)
    xi = xi[:, :, None]
    zeta = zeta[:, :, None]
    gl_b = jnp.broadcast_to(g_l[:, None, None], (RET_HEADS, 1, HEAD_DIM))
    _, xi_s, _, _ = _retention_tables(1)
    gam_s = jnp.broadcast_to(xi_s[:, :, None], (RET_HEADS, 1, HEAD_DIM))

    xp = x_prompt.reshape(mp, D_MODEL)
    xs = x_sample.reshape(n_s, D_MODEL)
    xp_bf = xp.astype(BF16)
    xs_bf = xs.astype(BF16)
    conv_state2d = state_conv.reshape(DEPTH, n_s, (CONV_WIDTH - 1) * D_FF)
    conv_b3 = conv_b[:, None, :]

    ret_p, pool_p, pool_s, conv_p, conv_s = [], [], [], [], []
    new_ret_sample = None
    for li in range(DEPTH):
        w_pool_bf = w_pool[li].astype(BF16)
        scale = pool_scale[li][None, :]
        g1, b1 = ln1_g[li][None, :], ln1_b[li][None, :]
        g2, b2 = ln2_g[li][None, :], ln2_b[li][None, :]

        proj_p, proj_s, w_o_bf = _proj(xp_bf, xs_bf, w_in, w_o, li, cos_p, sin_p, cos_s, sin_s, seq)

        mix, s_new = _ret_prompt(proj_p, decay, xi, zeta, gl_b, batch, seq)
        mix = _pool_prompt(proj_p, w_pool_bf, scale, mix, seq)
        xp, xp_bf = _out_proj(mix, w_o_bf, xp, g1, b1, LN_TM)
        ret_p.append(s_new)
        pool_p.append(proj_p.reshape(batch, seq, IN_WIDTH)[:, seq - POOL_BUF:, 4 * RET_WIDTH:])

        mix, new_ret_sample = _ret_sample(proj_s, state_ret, li, gam_s, new_ret_sample)
        mix = _pool_sample(proj_s, state_pool, li, w_pool_bf, scale, mix)
        xs, xs_bf = _out_proj(mix, w_o_bf, xs, g1, b1, n_s)
        pool_s.append(jnp.concatenate([state_pool[li][:, 1:], proj_s[:, None, 4 * RET_WIDTH:]], 1))

        h_p, tail, h_s, c0, c1, w_down_bf = _ffn_up(xp_bf, xs_bf, w_up, conv_w, conv_b3,
                                                    conv_state2d, w_down, li, seq)
        xp, xp_bf = _ffn_down(h_p, w_down_bf, xp, g2, b2, LN_TM)
        xs, xs_bf = _ffn_down(h_s, w_down_bf, xs, g2, b2, n_s)
        conv_p.append(tail[:, SUBLANES - (CONV_WIDTH - 1):, :])
        conv_s.append(jnp.stack([c0, c1], 1))

    return (xp.reshape(batch, seq, D_MODEL), xs.reshape(n_s, 1, D_MODEL),
            jnp.stack(ret_p), new_ret_sample, jnp.stack(pool_p), jnp.stack(pool_s),
            jnp.stack(conv_p), jnp.stack(conv_s))
```

```python
import functools

import jax
import jax.numpy as jnp
from jax import lax
from jax.experimental import pallas as pl
from jax.experimental.pallas import tpu as pltpu

F32 = jnp.float32
BF16 = jnp.bfloat16

D_MODEL = 4096
DEPTH = 2
PAST_LEN = 16384
RET_WIDTH = D_MODEL // 2
RET_HEADS = 8
HEAD_DIM = RET_WIDTH // RET_HEADS
HALF_HEAD = HEAD_DIM // 2
RET_CHUNK = 128
ROPE_BASE = 10000.0
POOL_WIDTH = D_MODEL - RET_WIDTH
POOL_WINDOWS = (2, 4, 8, 16)
POOL_GROUP_DIM = POOL_WIDTH // len(POOL_WINDOWS)
POOL_BUF = max(POOL_WINDOWS) - 1
IN_WIDTH = 4 * RET_WIDTH + POOL_WIDTH
D_FF = ((8 * D_MODEL // 3) + 255) // 256 * 256
CONV_WIDTH = 3
DN_ALPHA = (2.0 * DEPTH) ** 0.25
LN_EPS = 1e-5
K_SCALE = HEAD_DIM ** -0.5

V7X_VMEM_BYTES = 64 * 1024 * 1024
VMEM_LIMIT = V7X_VMEM_BYTES - 4 * 1024 * 1024
SUBLANES = 8
HALO_ROWS = 2 * SUBLANES

PROJ_TN = 512
FFN_TN = 256
LN_TM = 1024
OUT_TN = 1024
SUB_TN = 512
LN_ROWS = 64
DOWN_TK = D_FF // 2
DOWN_TN = 512
POOL_TM = 512
SAMPLE_NB = 16
DOT_ROWS = 1024
CAST_ROWS = 64


def _params(*semantics):
    return pltpu.CompilerParams(dimension_semantics=semantics, vmem_limit_bytes=VMEM_LIMIT)


def _layer_norm(y, g, b):
    mu = jnp.mean(y, axis=-1, keepdims=True)
    d = y - mu
    var = jnp.mean(d * d, axis=-1, keepdims=True)
    return d * lax.rsqrt(var + LN_EPS) * g + b


def _residual_copy(x_hbm, x_in, sems, row0, r, slot):
    src = pl.ds(pl.multiple_of(row0 + r * LN_ROWS, LN_ROWS), LN_ROWS)
    return pltpu.make_async_copy(x_hbm.at[src, :], x_in.at[slot], sems.at[2, slot])


def _residual_norm_store(acc_ref, x_hbm, g_ref, b_ref, y_hbm, ybf_hbm, x_in, stage_f, stage_b, sems,
                         row0):
    g = g_ref[...]
    b = b_ref[...]
    n = acc_ref.shape[0] // LN_ROWS

    def out_copies(r, slot):
        dst = pl.ds(pl.multiple_of(row0 + r * LN_ROWS, LN_ROWS), LN_ROWS)
        return (pltpu.make_async_copy(stage_f.at[slot], y_hbm.at[dst, :], sems.at[0, slot]),
                pltpu.make_async_copy(stage_b.at[slot], ybf_hbm.at[dst, :], sems.at[1, slot]))

    def body(r, carry):
        slot = r % 2

        @pl.when(r + 1 < n)
        def _():
            _residual_copy(x_hbm, x_in, sems, row0, r + 1, 1 - slot).start()

        @pl.when(r >= 2)
        def _():
            for c in out_copies(r - 2, slot):
                c.wait()

        _residual_copy(x_hbm, x_in, sems, row0, r, slot).wait()
        rows = pl.ds(pl.multiple_of(r * LN_ROWS, LN_ROWS), LN_ROWS)
        y = _layer_norm(DN_ALPHA * x_in[slot] + acc_ref[rows, :], g, b)
        stage_f[slot] = y
        stage_b[slot] = y.astype(BF16)
        for c in out_copies(r, slot):
            c.start()
        return carry

    lax.fori_loop(0, n, body, 0)
    for r in range(max(n - 2, 0), n):
        for c in out_copies(r, r % 2):
            c.wait()


def _ln_scratch():
    return [pltpu.VMEM((2, LN_ROWS, D_MODEL), F32),
            pltpu.VMEM((2, LN_ROWS, D_MODEL), F32),
            pltpu.VMEM((2, LN_ROWS, D_MODEL), BF16),
            pltpu.SemaphoreType.DMA((3, 2))]


def _gelu_exact(x):
    return 0.5 * x * (1.0 + lax.erf(x * (0.5 ** 0.5)))


def _rotate_heads(o_ref, cos_ref, sin_ref, scale):
    cos = cos_ref[...]
    sin = sin_ref[...]
    for h in range(o_ref.shape[1] // HEAD_DIM):
        lo = pl.ds(h * HEAD_DIM, HALF_HEAD)
        hi = pl.ds(h * HEAD_DIM + HALF_HEAD, HALF_HEAD)
        x1 = o_ref[:, lo]
        x2 = o_ref[:, hi]
        o_ref[:, lo] = (x1 * cos - x2 * sin) * scale
        o_ref[:, hi] = (x2 * cos + x1 * sin) * scale


def _proj_kernel(xp_ref, xs_ref, w_ref, cosp_ref, sinp_ref, coss_ref, sins_ref, wo_ref,
                 op_ref, os_ref, wobf_ref, *, rot_tiles, q_tiles, cast_steps):
    i = pl.program_id(0)
    j = pl.program_id(1)
    w = w_ref[...].astype(BF16)
    scale = jnp.where(j >= q_tiles, K_SCALE, 1.0).astype(F32)
    for r in range(xp_ref.shape[0] // DOT_ROWS):
        rows = pl.ds(r * DOT_ROWS, DOT_ROWS)
        op_ref[rows, :] = jnp.dot(xp_ref[rows, :], w, preferred_element_type=F32)

    @pl.when(i * pl.num_programs(1) + j < cast_steps)
    def _():
        wobf_ref[...] = wo_ref[...].astype(BF16)

    @pl.when(j < rot_tiles)
    def _():
        _rotate_heads(op_ref, cosp_ref, sinp_ref, scale)

    @pl.when(i == 0)
    def _():
        os_ref[...] = jnp.dot(xs_ref[...], w, preferred_element_type=F32)

    @pl.when((i == 0) & (j < rot_tiles))
    def _():
        _rotate_heads(os_ref, coss_ref, sins_ref, scale)


def _proj(xp_bf, xs_bf, w_in, w_o, li, cos_p, sin_p, cos_s, sin_s, tm):
    mp = xp_bf.shape[0]
    ns = xs_bf.shape[0]
    nj = IN_WIDTH // PROJ_TN
    t_tiles = cos_p.shape[0] // tm
    cast_steps = D_MODEL // CAST_ROWS
    assert cast_steps <= (mp // tm) * nj
    kern = functools.partial(_proj_kernel, rot_tiles=2 * RET_WIDTH // PROJ_TN,
                             q_tiles=RET_WIDTH // PROJ_TN, cast_steps=cast_steps)
    whole = lambda a: pl.BlockSpec(a.shape, lambda i, j: (0, 0))
    slab = lambda i, j: jnp.minimum(i * nj + j, cast_steps - 1)
    return pl.pallas_call(
        kern,
        grid=(mp // tm, nj),
        in_specs=[
            pl.BlockSpec((tm, D_MODEL), lambda i, j: (i, 0), pipeline_mode=pl.Buffered(1)),
            whole(xs_bf),
            pl.BlockSpec((None, D_MODEL, PROJ_TN), lambda i, j: (li, 0, j)),
            pl.BlockSpec((tm, HALF_HEAD), lambda i, j: (i % t_tiles, 0)),
            pl.BlockSpec((tm, HALF_HEAD), lambda i, j: (i % t_tiles, 0)),
            whole(cos_s), whole(sin_s),
            pl.BlockSpec((None, CAST_ROWS, D_MODEL), lambda i, j: (li, slab(i, j), 0)),
        ],
        out_specs=[
            pl.BlockSpec((tm, PROJ_TN), lambda i, j: (i, j)),
            pl.BlockSpec((ns, PROJ_TN), lambda i, j: (0, jnp.where(i == 0, j, nj - 1))),
            pl.BlockSpec((CAST_ROWS, D_MODEL), lambda i, j: (slab(i, j), 0)),
        ],
        out_shape=[jax.ShapeDtypeStruct((mp, IN_WIDTH), F32),
                   jax.ShapeDtypeStruct((ns, IN_WIDTH), F32),
                   jax.ShapeDtypeStruct((D_MODEL, D_MODEL), BF16)],
        compiler_params=_params("arbitrary", "arbitrary"),
        name="proj",
    )(xp_bf, xs_bf, w_in, cos_p, sin_p, cos_s, sin_s, w_o)


def _group_norm_gate(o, g):
    mu = jnp.mean(o, axis=-1, keepdims=True)
    d = o - mu
    var = jnp.mean(d * d, axis=-1, keepdims=True)
    return d * lax.rsqrt(var + LN_EPS) * (g * jax.nn.sigmoid(g))


def _ret_prompt_kernel(q_ref, k_ref, v_ref, g_ref, decay_ref, xi_ref, zeta_ref, gl_ref,
                       o_ref, snew_ref, s_scr):
    c = pl.program_id(1)

    @pl.when(c == 0)
    def _():
        s_scr[...] = jnp.zeros_like(s_scr)

    for h in range(RET_HEADS):
        cols = pl.ds(h * HEAD_DIM, HEAD_DIM)
        q = q_ref[:, cols]
        k = k_ref[:, cols]
        vb = v_ref[:, cols].astype(BF16)
        s = s_scr[h]
        scores = lax.dot_general(q.astype(BF16), k.astype(BF16), (((1,), (1,)), ((), ())),
                                 preferred_element_type=F32) * decay_ref[h]
        o = jnp.dot(scores.astype(BF16), vb, preferred_element_type=F32)
        o = o + jnp.dot((q * xi_ref[h]).astype(BF16), s.astype(BF16), preferred_element_type=F32)
        kz_t = (k * zeta_ref[h]).T.astype(BF16)
        s_scr[h] = gl_ref[h] * s + jnp.dot(kz_t, vb, preferred_element_type=F32)
        o_ref[:, cols] = _group_norm_gate(o, g_ref[:, cols]).astype(BF16)

    @pl.when(c == pl.num_programs(1) - 1)
    def _():
        snew_ref[0] = s_scr[...]


def _ret_prompt(proj, decay, xi, zeta, gl, batch, seq):
    nc = seq // RET_CHUNK

    def col(cb):
        return pl.BlockSpec((RET_CHUNK, RET_WIDTH), lambda b, c: (b * nc + c, cb))

    def whole(a):
        return pl.BlockSpec(a.shape, lambda b, c: (0,) * a.ndim)

    return pl.pallas_call(
        _ret_prompt_kernel,
        grid=(batch, nc),
        in_specs=[col(0), col(1), col(2), col(3), whole(decay), whole(xi), whole(zeta), whole(gl)],
        out_specs=[
            pl.BlockSpec((RET_CHUNK, RET_WIDTH), lambda b, c: (b * nc + c, 0)),
            pl.BlockSpec((1, RET_HEADS, HEAD_DIM, HEAD_DIM), lambda b, c: (b, 0, 0, 0)),
        ],
        out_shape=[
            jax.ShapeDtypeStruct((batch * seq, D_MODEL), BF16),
            jax.ShapeDtypeStruct((batch, RET_HEADS, HEAD_DIM, HEAD_DIM), F32),
        ],
        scratch_shapes=[pltpu.VMEM((RET_HEADS, HEAD_DIM, HEAD_DIM), F32)],
        compiler_params=_params("parallel", "arbitrary"),
        name="ret_prompt",
    )(proj, proj, proj, proj, decay, xi, zeta, gl)


def _ret_sample_kernel(q_ref, k_ref, v_ref, g_ref, s0_ref, gam_ref, *rest):
    o_ref, snew_ref = rest[-2:]
    q = q_ref[...]
    k = k_ref[...]
    v = v_ref[...]
    vb = v.astype(BF16)
    gam = gam_ref[0]
    qg = (q * gam).astype(BF16)
    rows = lax.broadcasted_iota(jnp.int32, (SAMPLE_NB, 1), 0)
    o_state = jnp.zeros((SAMPLE_NB, HEAD_DIM), F32)
    for n in range(SAMPLE_NB):
        s = s0_ref[n, 0]
        r = jnp.dot(qg, s.astype(BF16), preferred_element_type=F32)
        o_state = jnp.where(rows == n, r, o_state)
        k_n_t = jnp.where(rows == n, k, 0.0).T.astype(BF16)
        snew_ref[n, 0] = gam * s + jnp.dot(k_n_t, vb, preferred_element_type=F32)
    qk = jnp.sum(q * k, axis=-1, keepdims=True)
    o = qk * v + o_state
    o_ref[...] = _group_norm_gate(o, g_ref[...]).astype(BF16)


def _ret_sample(proj, state_all, li, gam, new_state_all):
    n = proj.shape[0]

    def col(cb):
        return pl.BlockSpec((SAMPLE_NB, HEAD_DIM), lambda i, h: (i, cb * RET_HEADS + h))

    state_spec = pl.BlockSpec((None, SAMPLE_NB, 1, HEAD_DIM, HEAD_DIM),
                              lambda i, h: (li, i, h, 0, 0))
    in_specs = [col(0), col(1), col(2), col(3), state_spec,
                pl.BlockSpec((1, 1, HEAD_DIM), lambda i, h: (h, 0, 0))]
    args = [proj, proj, proj, proj, state_all, gam]
    aliases = {}
    if new_state_all is not None:
        in_specs.append(pl.BlockSpec(memory_space=pl.ANY))
        args.append(new_state_all)
        aliases = {len(args) - 1: 1}
    return pl.pallas_call(
        _ret_sample_kernel,
        grid=(n // SAMPLE_NB, RET_HEADS),
        in_specs=in_specs,
        out_specs=[pl.BlockSpec((SAMPLE_NB, HEAD_DIM), lambda i, h: (i, h)), state_spec],
        out_shape=[jax.ShapeDtypeStruct((n, D_MODEL), BF16),
                   jax.ShapeDtypeStruct(state_all.shape, F32)],
        input_output_aliases=aliases,
        compiler_params=_params("parallel", "arbitrary"),
        name="ret_sample",
    )(*args)


def _pool_prompt_kernel(u_ref, halo_ref, wp_ref, sc_ref, mix_hbm, o_ref, ext_scr, *,
                        tiles_per_seq):
    del mix_hbm
    i = pl.program_id(0)
    tm = u_ref.shape[0]
    t0 = (i % tiles_per_seq) * tm
    ext_scr[0:HALO_ROWS, :] = jnp.where(t0 == 0, 0.0, halo_ref[...])
    ext_scr[HALO_ROWS:HALO_ROWS + tm, :] = u_ref[...]
    pos = t0 + lax.broadcasted_iota(jnp.int32, (tm, 1), 0)
    for gi, w in enumerate(POOL_WINDOWS):
        cols = pl.ds(gi * POOL_GROUP_DIM, POOL_GROUP_DIM)
        u = u_ref[:, cols]
        s = u
        for back in range(1, w):
            s = s + ext_scr[pl.ds(HALO_ROWS - back, tm), cols]
        cnt = jnp.minimum(w, pos + 1).astype(F32)
        pooled = s / cnt - u
        mixed = jnp.dot(pooled.astype(BF16), wp_ref[gi], preferred_element_type=F32)
        o_ref[:, cols] = (mixed * sc_ref[:, cols]).astype(BF16)


def _pool_prompt(proj, w_pool_bf, scale, mix, seq):
    m = proj.shape[0]
    tiles_per_seq = seq // POOL_TM
    u_col = 4 * RET_WIDTH // POOL_WIDTH
    halo_per_tile = POOL_TM // HALO_ROWS
    kern = functools.partial(_pool_prompt_kernel, tiles_per_seq=tiles_per_seq)
    return pl.pallas_call(
        kern,
        grid=(m // POOL_TM,),
        in_specs=[
            pl.BlockSpec((POOL_TM, POOL_WIDTH), lambda i: (i, u_col)),
            pl.BlockSpec((HALO_ROWS, POOL_WIDTH),
                         lambda i: (jnp.maximum(i * halo_per_tile - 1, 0), u_col)),
            pl.BlockSpec(w_pool_bf.shape, lambda i: (0, 0, 0)),
            pl.BlockSpec((1, POOL_WIDTH), lambda i: (0, 0)),
            pl.BlockSpec(memory_space=pl.ANY),
        ],
        out_specs=pl.BlockSpec((POOL_TM, POOL_WIDTH), lambda i: (i, 1)),
        out_shape=jax.ShapeDtypeStruct(mix.shape, BF16),
        input_output_aliases={4: 0},
        scratch_shapes=[pltpu.VMEM((HALO_ROWS + POOL_TM, POOL_WIDTH), F32)],
        compiler_params=_params("parallel"),
        name="pool_prompt",
    )(proj, proj, w_pool_bf, scale, mix)


def _pool_sample_kernel(u_ref, buf_ref, wp_ref, sc_ref, mix_hbm, o_ref):
    del mix_hbm
    row = lax.broadcasted_iota(jnp.int32, (1, POOL_BUF, 1), 1)
    for gi, w in enumerate(POOL_WINDOWS):
        cols = pl.ds(gi * POOL_GROUP_DIM, POOL_GROUP_DIM)
        u = u_ref[:, cols]
        past = jnp.where(row >= POOL_BUF - (w - 1), buf_ref[:, :, cols], 0.0)
        s = u + jnp.sum(past, axis=1)
        pooled = s / float(w) - u
        mixed = jnp.dot(pooled.astype(BF16), wp_ref[gi], preferred_element_type=F32)
        o_ref[:, cols] = (mixed * sc_ref[:, cols]).astype(BF16)


def _pool_sample(proj, buf_all, li, w_pool_bf, scale, mix):
    n = proj.shape[0]
    u_col = 4 * RET_WIDTH // POOL_WIDTH
    return pl.pallas_call(
        _pool_sample_kernel,
        grid=(n // SAMPLE_NB,),
        in_specs=[
            pl.BlockSpec((SAMPLE_NB, POOL_WIDTH), lambda i: (i, u_col)),
            pl.BlockSpec((None, SAMPLE_NB, POOL_BUF, POOL_WIDTH), lambda i: (li, i, 0, 0)),
            pl.BlockSpec(w_pool_bf.shape, lambda i: (0, 0, 0)),
            pl.BlockSpec((1, POOL_WIDTH), lambda i: (0, 0)),
            pl.BlockSpec(memory_space=pl.ANY),
        ],
        out_specs=pl.BlockSpec((SAMPLE_NB, POOL_WIDTH), lambda i: (i, 1)),
        out_shape=jax.ShapeDtypeStruct(mix.shape, BF16),
        input_output_aliases={4: 0},
        compiler_params=_params("parallel"),
        name="pool_sample",
    )(proj, buf_all, w_pool_bf, scale, mix)


def _out_proj_kernel(mix_ref, w_ref, x_hbm, g_ref, b_ref, y_hbm, ybf_hbm,
                     acc, x_in, stage_f, stage_b, sems):
    i = pl.program_id(0)
    j = pl.program_id(1)
    tm = mix_ref.shape[0]
    tn = w_ref.shape[1]
    last = j == pl.num_programs(1) - 1

    @pl.when(last)
    def _():
        _residual_copy(x_hbm, x_in, sems, i * tm, 0, 0).start()

    for c in range(tn // SUB_TN):
        cols = pl.ds(pl.multiple_of(j * tn + c * SUB_TN, SUB_TN), SUB_TN)
        acc[:, cols] = jnp.dot(mix_ref[...], w_ref[:, pl.ds(c * SUB_TN, SUB_TN)],
                               preferred_element_type=F32)

    @pl.when(last)
    def _():
        _residual_norm_store(acc, x_hbm, g_ref, b_ref, y_hbm, ybf_hbm, x_in, stage_f, stage_b,
                             sems, i * tm)


def _out_proj(mix, w_o_bf, x, g, b, tm):
    m = x.shape[0]
    return pl.pallas_call(
        _out_proj_kernel,
        grid=(m // tm, D_MODEL // OUT_TN),
        in_specs=[
            pl.BlockSpec((tm, D_MODEL), lambda i, j: (i, 0)),
            pl.BlockSpec((D_MODEL, OUT_TN), lambda i, j: (0, j)),
            pl.BlockSpec(memory_space=pl.ANY),
            pl.BlockSpec((1, D_MODEL), lambda i, j: (0, 0)),
            pl.BlockSpec((1, D_MODEL), lambda i, j: (0, 0)),
        ],
        out_specs=[pl.BlockSpec(memory_space=pl.ANY), pl.BlockSpec(memory_space=pl.ANY)],
        out_shape=[jax.ShapeDtypeStruct((m, D_MODEL), F32),
                   jax.ShapeDtypeStruct((m, D_MODEL), BF16)],
        scratch_shapes=[pltpu.VMEM((tm, D_MODEL), F32)] + _ln_scratch(),
        compiler_params=_params("arbitrary", "arbitrary"),
        name="out_proj",
    )(mix, w_o_bf, x, g, b)


def _ffn_up_kernel(xp_ref, xs_ref, wa_ref, wb_ref, cw_ref, cb_ref, s0_ref, s1_ref, wd_ref,
                   hp_ref, tail_ref, hs_ref, n0_ref, n1_ref, wdbf_ref, ext_scr, w_scr):
    i = pl.program_id(0)
    tm = xp_ref.shape[0]
    tn = wa_ref.shape[1]
    w_scr[:, 0:tn] = wa_ref[...].astype(BF16)
    w_scr[:, tn:2 * tn] = wb_ref[...].astype(BF16)
    w = w_scr[...]
    cw0 = cw_ref[0:1, :]
    cw1 = cw_ref[1:2, :]
    cw2 = cw_ref[2:3, :]
    cb = cb_ref[...]

    wdbf_ref[...] = wd_ref[...].astype(BF16)

    ext_scr[0:SUBLANES, :] = jnp.zeros((SUBLANES, ext_scr.shape[1]), F32)
    for r in range(tm // DOT_ROWS):
        r0 = r * DOT_ROWS
        ab = jnp.dot(xp_ref[pl.ds(r0, DOT_ROWS), :], w, preferred_element_type=F32)
        ext_scr[pl.ds(SUBLANES + r0, DOT_ROWS), :] = ab[:, 0:tn]
        conv = cb + ext_scr[pl.ds(SUBLANES + r0 - 2, DOT_ROWS), :] * cw0
        conv = conv + ext_scr[pl.ds(SUBLANES + r0 - 1, DOT_ROWS), :] * cw1
        conv = conv + ext_scr[pl.ds(SUBLANES + r0, DOT_ROWS), :] * cw2
        hp_ref[pl.ds(r0, DOT_ROWS), :] = (_gelu_exact(conv) * ab[:, tn:2 * tn]).astype(BF16)
    tail_ref[0] = ext_scr[tm:tm + SUBLANES, :]

    @pl.when(i == 0)
    def _():
        ab_s = jnp.dot(xs_ref[...], w_scr[...], preferred_element_type=F32)
        a_s = ab_s[:, 0:tn]
        b_s = ab_s[:, tn:2 * tn]
        s1 = s1_ref[...]
        conv_s = cb + s0_ref[...] * cw0
        conv_s = conv_s + s1 * cw1
        conv_s = conv_s + a_s * cw2
        hs_ref[...] = (_gelu_exact(conv_s) * b_s).astype(BF16)
        n0_ref[...] = s1
        n1_ref[...] = a_s


def _ffn_up(xp_bf, xs_bf, w_up, conv_w, conv_b, state2d, w_down, li, seq):
    mp = xp_bf.shape[0]
    ns = xs_bf.shape[0]
    nj = D_FF // FFN_TN
    tm = seq
    assert (mp // tm) * nj * CAST_ROWS == D_FF
    first = lambda i, j: jnp.where(i == 0, j, nj - 1)
    s_tile = pl.BlockSpec((ns, FFN_TN), lambda i, j: (0, first(i, j)))
    return pl.pallas_call(
        _ffn_up_kernel,
        grid=(mp // tm, nj),
        in_specs=[
            pl.BlockSpec((tm, D_MODEL), lambda i, j: (i, 0), pipeline_mode=pl.Buffered(1)),
            pl.BlockSpec((ns, D_MODEL), lambda i, j: (0, 0)),
            pl.BlockSpec((None, D_MODEL, FFN_TN), lambda i, j: (li, 0, j)),
            pl.BlockSpec((None, D_MODEL, FFN_TN), lambda i, j: (li, 0, nj + j)),
            pl.BlockSpec((None, CONV_WIDTH, FFN_TN), lambda i, j: (li, 0, j)),
            pl.BlockSpec((None, 1, FFN_TN), lambda i, j: (li, 0, j)),
            pl.BlockSpec((None, ns, FFN_TN), lambda i, j: (li, 0, first(i, j))),
            pl.BlockSpec((None, ns, FFN_TN), lambda i, j: (li, 0, nj + first(i, j))),
            pl.BlockSpec((None, CAST_ROWS, D_MODEL), lambda i, j: (li, i * nj + j, 0)),
        ],
        out_specs=[
            pl.BlockSpec((tm, FFN_TN), lambda i, j: (i, j)),
            pl.BlockSpec((1, SUBLANES, FFN_TN), lambda i, j: (i, 0, j)),
            s_tile, s_tile, s_tile,
            pl.BlockSpec((CAST_ROWS, D_MODEL), lambda i, j: (i * nj + j, 0)),
        ],
        out_shape=[jax.ShapeDtypeStruct((mp, D_FF), BF16),
                   jax.ShapeDtypeStruct((mp // tm, SUBLANES, D_FF), F32),
                   jax.ShapeDtypeStruct((ns, D_FF), BF16),
                   jax.ShapeDtypeStruct((ns, D_FF), F32),
                   jax.ShapeDtypeStruct((ns, D_FF), F32),
                   jax.ShapeDtypeStruct((D_FF, D_MODEL), BF16)],
        scratch_shapes=[pltpu.VMEM((SUBLANES + tm, FFN_TN), F32),
                        pltpu.VMEM((D_MODEL, 2 * FFN_TN), BF16)],
        compiler_params=_params("arbitrary", "arbitrary"),
        name="ffn_up",
    )(xp_bf, xs_bf, w_up, w_up, conv_w, conv_b, state2d, state2d, w_down)


def _ffn_down_kernel(h_ref, w_ref, x_hbm, g_ref, b_ref, y_hbm, ybf_hbm,
                     acc, x_in, stage_f, stage_b, sems):
    i = pl.program_id(0)
    k = pl.program_id(1)
    j = pl.program_id(2)
    tm = h_ref.shape[0]
    tn = w_ref.shape[1]
    last = (k == pl.num_programs(1) - 1) & (j == pl.num_programs(2) - 1)

    @pl.when(last)
    def _():
        _residual_copy(x_hbm, x_in, sems, i * tm, 0, 0).start()

    @pl.when((i == 0) & (k == 0) & (j == 0))
    def _():
        acc[...] = jnp.zeros_like(acc)

    for c in range(tn // SUB_TN):
        cols = pl.ds(pl.multiple_of(j * tn + c * SUB_TN, SUB_TN), SUB_TN)
        part = jnp.dot(h_ref[...], w_ref[:, pl.ds(c * SUB_TN, SUB_TN)],
                       preferred_element_type=F32)
        acc[:, cols] = part + jnp.where(k == 0, 0.0, acc[:, cols])

    @pl.when(last)
    def _():
        _residual_norm_store(acc, x_hbm, g_ref, b_ref, y_hbm, ybf_hbm, x_in, stage_f, stage_b,
                             sems, i * tm)


def _ffn_down(h, w_down_bf, x, g, b, tm):
    m = x.shape[0]
    nk = D_FF // DOWN_TK
    nj = D_MODEL // DOWN_TN
    return pl.pallas_call(
        _ffn_down_kernel,
        grid=(m // tm, nk, nj),
        in_specs=[
            pl.BlockSpec((tm, DOWN_TK), lambda i, k, j: (i, k)),
            pl.BlockSpec((DOWN_TK, DOWN_TN), lambda i, k, j: (k, j)),
            pl.BlockSpec(memory_space=pl.ANY),
            pl.BlockSpec((1, D_MODEL), lambda i, k, j: (0, 0)),
            pl.BlockSpec((1, D_MODEL), lambda i, k, j: (0, 0)),
        ],
        out_specs=[pl.BlockSpec(memory_space=pl.ANY), pl.BlockSpec(memory_space=pl.ANY)],
        out_shape=[jax.ShapeDtypeStruct((m, D_MODEL), F32),
                   jax.ShapeDtypeStruct((m, D_MODEL), BF16)],
        scratch_shapes=[pltpu.VMEM((tm, D_MODEL), F32)] + _ln_scratch(),
        compiler_params=_params("arbitrary", "arbitrary", "arbitrary"),
        name="ffn_down",
    )(h, w_down_bf, x, g, b)


def _rotary_tables(pos):
    inv = ROPE_BASE ** (-jnp.arange(HALF_HEAD, dtype=F32) / HALF_HEAD)
    ang = pos.astype(F32)[:, None] * inv[None, :]
    return jnp.cos(ang), jnp.sin(ang)


def _retention_tables(l):
    log_g = jnp.log1p(-(2.0 ** (-5.0 - jnp.arange(RET_HEADS, dtype=F32))))
    i = jnp.arange(l)
    diff = i[:, None] - i[None, :]
    decay = jnp.where(diff[None] >= 0,
                      jnp.exp(jnp.maximum(diff, 0)[None].astype(F32) * log_g[:, None, None]), 0.0)
    xi = jnp.exp((i + 1)[None].astype(F32) * log_g[:, None])
    zeta = jnp.exp((l - 1 - i)[None].astype(F32) * log_g[:, None])
    g_l = jnp.exp(l * log_g)
    return decay, xi, zeta, g_l


def kernel(x_prompt, x_sample, state_ret, state_pool, state_conv, w_in, w_pool, pool_scale, w_o,
           ln1_g, ln1_b, w_up, conv_w, conv_b, w_down, ln2_g, ln2_b):
    batch, seq, _ = x_prompt.shape
    n_s = x_sample.shape[0]
    assert x_sample.shape[1] == 1 and seq % RET_CHUNK == 0
    mp = batch * seq

    cos_p, sin_p = _rotary_tables(jnp.arange(seq))
    cos_s, sin_s = _rotary_tables(PAST_LEN + jnp.arange(1))
    cos_s = jnp.broadcast_to(cos_s, (n_s, HALF_HEAD))
    sin_s = jnp.broadcast_to(sin_s, (n_s, HALF_HEAD))
    decay, xi, zeta, g_l = _retention_tables(RET_CHUNK)
    xi = xi[:, :, None]
    zeta = zeta[:, :, None]
    gl_b = jnp.broadcast_to(g_l[:, None, None], (RET_HEADS, 1, HEAD_DIM))
    _, xi_s, _, _ = _retention_tables(1)
    gam_s = jnp.broadcast_to(xi_s[:, :, None], (RET_HEADS, 1, HEAD_DIM))

    xp = x_prompt.reshape(mp, D_MODEL)
    xs = x_sample.reshape(n_s, D_MODEL)
    xp_bf = xp.astype(BF16)
    xs_bf = xs.astype(BF16)
    conv_state2d = state_conv.reshape(DEPTH, n_s, (CONV_WIDTH - 1) * D_FF)
    conv_b3 = conv_b[:, None, :]

    ret_p, pool_p, pool_s, conv_p, conv_s = [], [], [], [], []
    new_ret_sample = None
    for li in range(DEPTH):
        w_pool_bf = w_pool[li].astype(BF16)
        scale = pool_scale[li][None, :]
        g1, b1 = ln1_g[li][None, :], ln1_b[li][None, :]
        g2, b2 = ln2_g[li][None, :], ln2_b[li][None, :]

        proj_p, proj_s, w_o_bf = _proj(xp_bf, xs_bf, w_in, w_o, li, cos_p, sin_p, cos_s, sin_s, seq)

        mix, s_new = _ret_prompt(proj_p, decay, xi, zeta, gl_b, batch, seq)
        mix = _pool_prompt(proj_p, w_pool_bf, scale, mix, seq)
        xp, xp_bf = _out_proj(mix, w_o_bf, xp, g1, b1, LN_TM)
        ret_p.append(s_new)
        pool_p.append(proj_p.reshape(batch, seq, IN_WIDTH)[:, seq - POOL_BUF:, 4 * RET_WIDTH:])

        mix, new_ret_sample = _ret_sample(proj_s, state_ret, li, gam_s, new_ret_sample)
        mix = _pool_sample(proj_s, state_pool, li, w_pool_bf, scale, mix)
        xs, xs_bf = _out_proj(mix, w_o_bf, xs, g1, b1, n_s)
        pool_s.append(jnp.concatenate([state_pool[li][:, 1:], proj_s[:, None, 4 * RET_WIDTH:]], 1))

        h_p, tail, h_s, c0, c1, w_down_bf = _ffn_up(xp_bf, xs_bf, w_up, conv_w, conv_b3,
                                                    conv_state2d, w_down, li, seq)
        xp, xp_bf = _ffn_down(h_p, w_down_bf, xp, g2, b2, LN_TM)
        xs, xs_bf = _ffn_down(h_s, w_down_bf, xs, g2, b2, n_s)
        conv_p.append(tail[:, SUBLANES - (CONV_WIDTH - 1):, :])
        conv_s.append(jnp.stack([c0, c1], 1))

    return (xp.reshape(batch, seq, D_MODEL), xs.reshape(n_s, 1, D_MODEL),
            jnp.stack(ret_p), new_ret_sample, jnp.stack(pool_p), jnp.stack(pool_s),
            jnp.stack(conv_p), jnp.stack(conv_s))
```

```python
import functools

import jax
import jax.numpy as jnp
from jax import lax
from jax.experimental import pallas as pl
from jax.experimental.pallas import tpu as pltpu

F32 = jnp.float32
BF16 = jnp.bfloat16

D_MODEL = 4096
DEPTH = 2
PAST_LEN = 16384
RET_WIDTH = D_MODEL // 2
RET_HEADS = 8
HEAD_DIM = RET_WIDTH // RET_HEADS
HALF_HEAD = HEAD_DIM // 2
RET_CHUNK = 128
ROPE_BASE = 10000.0
POOL_WIDTH = D_MODEL - RET_WIDTH
POOL_WINDOWS = (2, 4, 8, 16)
POOL_GROUP_DIM = POOL_WIDTH // len(POOL_WINDOWS)
POOL_BUF = max(POOL_WINDOWS) - 1
IN_WIDTH = 4 * RET_WIDTH + POOL_WIDTH
D_FF = ((8 * D_MODEL // 3) + 255) // 256 * 256
CONV_WIDTH = 3
DN_ALPHA = (2.0 * DEPTH) ** 0.25
LN_EPS = 1e-5
K_SCALE = HEAD_DIM ** -0.5

V7X_VMEM_BYTES = 64 * 1024 * 1024
VMEM_LIMIT = V7X_VMEM_BYTES - 8 * 1024 * 1024
SUBLANES = 8
HALO_ROWS = 2 * SUBLANES

PROJ_TN = 512
FFN_TN = 256
LN_TM = 512
OUT_TN = 1024
SUB_TN = 512
LN_ROWS = 64
DOWN_TK = D_FF // 2
DOWN_TN = 1024
POOL_TM = 512
SAMPLE_NB = 16
DOT_ROWS = 1024
CAST_ROWS = 64


def _params(*semantics):
    return pltpu.CompilerParams(dimension_semantics=semantics, vmem_limit_bytes=VMEM_LIMIT)


def _layer_norm(y, g, b):
    mu = jnp.mean(y, axis=-1, keepdims=True)
    d = y - mu
    var = jnp.mean(d * d, axis=-1, keepdims=True)
    return d * lax.rsqrt(var + LN_EPS) * g + b


def _layer_norm_store(acc_ref, g_ref, b_ref, y_hbm, ybf_hbm, stage_f, stage_b, sems, row0):
    g = g_ref[...]
    b = b_ref[...]
    n = acc_ref.shape[0] // LN_ROWS

    def copies(r, slot):
        dst = pl.ds(pl.multiple_of(row0 + r * LN_ROWS, LN_ROWS), LN_ROWS)
        return (pltpu.make_async_copy(stage_f.at[slot], y_hbm.at[dst, :], sems.at[0, slot]),
                pltpu.make_async_copy(stage_b.at[slot], ybf_hbm.at[dst, :], sems.at[1, slot]))

    def body(r, carry):
        slot = r % 2

        @pl.when(r >= 2)
        def _():
            for c in copies(r - 2, slot):
                c.wait()

        rows = pl.ds(pl.multiple_of(r * LN_ROWS, LN_ROWS), LN_ROWS)
        y = _layer_norm(acc_ref[rows, :], g, b)
        stage_f[slot] = y
        stage_b[slot] = y.astype(BF16)
        for c in copies(r, slot):
            c.start()
        return carry

    lax.fori_loop(0, n, body, 0)
    for r in range(max(n - 2, 0), n):
        for c in copies(r, r % 2):
            c.wait()


def _ln_scratch():
    return [pltpu.VMEM((2, LN_ROWS, D_MODEL), F32), pltpu.VMEM((2, LN_ROWS, D_MODEL), BF16),
            pltpu.SemaphoreType.DMA((2, 2))]


def _gelu_exact(x):
    return 0.5 * x * (1.0 + lax.erf(x * (0.5 ** 0.5)))


def _rotate_heads(o_ref, cos_ref, sin_ref, scale):
    cos = cos_ref[...]
    sin = sin_ref[...]
    for h in range(o_ref.shape[1] // HEAD_DIM):
        lo = pl.ds(h * HEAD_DIM, HALF_HEAD)
        hi = pl.ds(h * HEAD_DIM + HALF_HEAD, HALF_HEAD)
        x1 = o_ref[:, lo]
        x2 = o_ref[:, hi]
        o_ref[:, lo] = (x1 * cos - x2 * sin) * scale
        o_ref[:, hi] = (x2 * cos + x1 * sin) * scale


def _proj_kernel(xp_ref, xs_ref, w_ref, cosp_ref, sinp_ref, coss_ref, sins_ref, wo_ref,
                 op_ref, os_ref, wobf_ref, *, rot_tiles, q_tiles, cast_steps):
    i = pl.program_id(0)
    j = pl.program_id(1)
    w = w_ref[...].astype(BF16)
    scale = jnp.where(j >= q_tiles, K_SCALE, 1.0).astype(F32)
    for r in range(xp_ref.shape[0] // DOT_ROWS):
        rows = pl.ds(r * DOT_ROWS, DOT_ROWS)
        op_ref[rows, :] = jnp.dot(xp_ref[rows, :], w, preferred_element_type=F32)

    @pl.when(i * pl.num_programs(1) + j < cast_steps)
    def _():
        wobf_ref[...] = wo_ref[...].astype(BF16)

    @pl.when(j < rot_tiles)
    def _():
        _rotate_heads(op_ref, cosp_ref, sinp_ref, scale)

    @pl.when(i == 0)
    def _():
        os_ref[...] = jnp.dot(xs_ref[...], w, preferred_element_type=F32)

    @pl.when((i == 0) & (j < rot_tiles))
    def _():
        _rotate_heads(os_ref, coss_ref, sins_ref, scale)


def _proj(xp_bf, xs_bf, w_in, w_o, li, cos_p, sin_p, cos_s, sin_s, tm):
    mp = xp_bf.shape[0]
    ns = xs_bf.shape[0]
    nj = IN_WIDTH // PROJ_TN
    t_tiles = cos_p.shape[0] // tm
    cast_steps = D_MODEL // CAST_ROWS
    assert cast_steps <= (mp // tm) * nj
    kern = functools.partial(_proj_kernel, rot_tiles=2 * RET_WIDTH // PROJ_TN,
                             q_tiles=RET_WIDTH // PROJ_TN, cast_steps=cast_steps)
    whole = lambda a: pl.BlockSpec(a.shape, lambda i, j: (0, 0))
    slab = lambda i, j: jnp.minimum(i * nj + j, cast_steps - 1)
    return pl.pallas_call(
        kern,
        grid=(mp // tm, nj),
        in_specs=[
            pl.BlockSpec((tm, D_MODEL), lambda i, j: (i, 0), pipeline_mode=pl.Buffered(1)),
            whole(xs_bf),
            pl.BlockSpec((None, D_MODEL, PROJ_TN), lambda i, j: (li, 0, j)),
            pl.BlockSpec((tm, HALF_HEAD), lambda i, j: (i % t_tiles, 0)),
            pl.BlockSpec((tm, HALF_HEAD), lambda i, j: (i % t_tiles, 0)),
            whole(cos_s), whole(sin_s),
            pl.BlockSpec((None, CAST_ROWS, D_MODEL), lambda i, j: (li, slab(i, j), 0)),
        ],
        out_specs=[
            pl.BlockSpec((tm, PROJ_TN), lambda i, j: (i, j)),
            pl.BlockSpec((ns, PROJ_TN), lambda i, j: (0, jnp.where(i == 0, j, nj - 1))),
            pl.BlockSpec((CAST_ROWS, D_MODEL), lambda i, j: (slab(i, j), 0)),
        ],
        out_shape=[jax.ShapeDtypeStruct((mp, IN_WIDTH), F32),
                   jax.ShapeDtypeStruct((ns, IN_WIDTH), F32),
                   jax.ShapeDtypeStruct((D_MODEL, D_MODEL), BF16)],
        compiler_params=_params("arbitrary", "arbitrary"),
        name="proj",
    )(xp_bf, xs_bf, w_in, cos_p, sin_p, cos_s, sin_s, w_o)


def _group_norm_gate(o, g):
    mu = jnp.mean(o, axis=-1, keepdims=True)
    d = o - mu
    var = jnp.mean(d * d, axis=-1, keepdims=True)
    return d * lax.rsqrt(var + LN_EPS) * (g * jax.nn.sigmoid(g))


def _ret_prompt_kernel(q_ref, k_ref, v_ref, g_ref, decay_ref, xi_ref, zeta_ref, gl_ref, *rest):
    o_ref, snew_ref, s_scr = rest[-3:]
    c = pl.program_id(1)

    @pl.when(c == 0)
    def _():
        s_scr[...] = jnp.zeros_like(s_scr)

    for h in range(RET_HEADS):
        cols = pl.ds(h * HEAD_DIM, HEAD_DIM)
        q = q_ref[:, cols]
        k = k_ref[:, cols]
        vb = v_ref[:, cols].astype(BF16)
        s = s_scr[h]
        scores = lax.dot_general(q.astype(BF16), k.astype(BF16), (((1,), (1,)), ((), ())),
                                 preferred_element_type=F32) * decay_ref[h]
        o = jnp.dot(scores.astype(BF16), vb, preferred_element_type=F32)
        o = o + jnp.dot((q * xi_ref[h]).astype(BF16), s.astype(BF16), preferred_element_type=F32)
        kz_t = (k * zeta_ref[h]).T.astype(BF16)
        s_scr[h] = gl_ref[h] * s + jnp.dot(kz_t, vb, preferred_element_type=F32)
        o_ref[:, cols] = _group_norm_gate(o, g_ref[:, cols]).astype(BF16)

    @pl.when(c == pl.num_programs(1) - 1)
    def _():
        snew_ref[...] = s_scr[...]


def _ret_prompt(proj, decay, xi, zeta, gl, li, new_state_all, batch, seq):
    nc = seq // RET_CHUNK

    def col(cb):
        return pl.BlockSpec((RET_CHUNK, RET_WIDTH), lambda b, c: (b * nc + c, cb))

    def whole(a):
        return pl.BlockSpec(a.shape, lambda b, c: (0,) * a.ndim)

    in_specs = [col(0), col(1), col(2), col(3), whole(decay), whole(xi), whole(zeta), whole(gl)]
    args = [proj, proj, proj, proj, decay, xi, zeta, gl]
    aliases = {}
    if new_state_all is not None:
        in_specs.append(pl.BlockSpec(memory_space=pl.ANY))
        args.append(new_state_all)
        aliases = {len(args) - 1: 1}
    return pl.pallas_call(
        _ret_prompt_kernel,
        grid=(batch, nc),
        in_specs=in_specs,
        out_specs=[
            pl.BlockSpec((RET_CHUNK, RET_WIDTH), lambda b, c: (b * nc + c, 0)),
            pl.BlockSpec((None, None, RET_HEADS, HEAD_DIM, HEAD_DIM), lambda b, c: (li, b, 0, 0, 0)),
        ],
        out_shape=[
            jax.ShapeDtypeStruct((batch * seq, D_MODEL), BF16),
            jax.ShapeDtypeStruct((DEPTH, batch, RET_HEADS, HEAD_DIM, HEAD_DIM), F32),
        ],
        input_output_aliases=aliases,
        scratch_shapes=[pltpu.VMEM((RET_HEADS, HEAD_DIM, HEAD_DIM), F32)],
        compiler_params=_params("parallel", "arbitrary"),
        name="ret_prompt",
    )(*args)


def _ret_sample_kernel(q_ref, k_ref, v_ref, g_ref, s0_ref, gam_ref, *rest):
    o_ref, snew_ref = rest[-2:]
    q = q_ref[...]
    k = k_ref[...]
    v = v_ref[...]
    vb = v.astype(BF16)
    gam = gam_ref[0]
    qg = (q * gam).astype(BF16)
    rows = lax.broadcasted_iota(jnp.int32, (SAMPLE_NB, 1), 0)
    o_state = jnp.zeros((SAMPLE_NB, HEAD_DIM), F32)
    for n in range(SAMPLE_NB):
        s = s0_ref[n, 0]
        r = jnp.dot(qg, s.astype(BF16), preferred_element_type=F32)
        o_state = jnp.where(rows == n, r, o_state)
        k_n_t = jnp.where(rows == n, k, 0.0).T.astype(BF16)
        snew_ref[n, 0] = gam * s + jnp.dot(k_n_t, vb, preferred_element_type=F32)
    qk = jnp.sum(q * k, axis=-1, keepdims=True)
    o = qk * v + o_state
    o_ref[...] = _group_norm_gate(o, g_ref[...]).astype(BF16)


def _ret_sample(proj, state_all, li, gam, new_state_all):
    n = proj.shape[0]

    def col(cb):
        return pl.BlockSpec((SAMPLE_NB, HEAD_DIM), lambda i, h: (i, cb * RET_HEADS + h))

    state_spec = pl.BlockSpec((None, SAMPLE_NB, 1, HEAD_DIM, HEAD_DIM),
                              lambda i, h: (li, i, h, 0, 0))
    in_specs = [col(0), col(1), col(2), col(3), state_spec,
                pl.BlockSpec((1, 1, HEAD_DIM), lambda i, h: (h, 0, 0))]
    args = [proj, proj, proj, proj, state_all, gam]
    aliases = {}
    if new_state_all is not None:
        in_specs.append(pl.BlockSpec(memory_space=pl.ANY))
        args.append(new_state_all)
        aliases = {len(args) - 1: 1}
    return pl.pallas_call(
        _ret_sample_kernel,
        grid=(n // SAMPLE_NB, RET_HEADS),
        in_specs=in_specs,
        out_specs=[pl.BlockSpec((SAMPLE_NB, HEAD_DIM), lambda i, h: (i, h)), state_spec],
        out_shape=[jax.ShapeDtypeStruct((n, D_MODEL), BF16),
                   jax.ShapeDtypeStruct(state_all.shape, F32)],
        input_output_aliases=aliases,
        compiler_params=_params("parallel", "arbitrary"),
        name="ret_sample",
    )(*args)


def _pool_prompt_kernel(u_ref, halo_ref, wp_ref, sc_ref, mix_hbm, o_ref, ext_scr, *,
                        tiles_per_seq):
    del mix_hbm
    i = pl.program_id(0)
    tm = u_ref.shape[0]
    t0 = (i % tiles_per_seq) * tm
    ext_scr[0:HALO_ROWS, :] = jnp.where(t0 == 0, 0.0, halo_ref[...])
    ext_scr[HALO_ROWS:HALO_ROWS + tm, :] = u_ref[...]
    pos = t0 + lax.broadcasted_iota(jnp.int32, (tm, 1), 0)
    for gi, w in enumerate(POOL_WINDOWS):
        cols = pl.ds(gi * POOL_GROUP_DIM, POOL_GROUP_DIM)
        u = u_ref[:, cols]
        s = u
        for back in range(1, w):
            s = s + ext_scr[pl.ds(HALO_ROWS - back, tm), cols]
        cnt = jnp.minimum(w, pos + 1).astype(F32)
        pooled = s / cnt - u
        mixed = jnp.dot(pooled.astype(BF16), wp_ref[gi], preferred_element_type=F32)
        o_ref[:, cols] = (mixed * sc_ref[:, cols]).astype(BF16)


def _pool_prompt(proj, w_pool_bf, scale, mix, seq):
    m = proj.shape[0]
    tiles_per_seq = seq // POOL_TM
    u_col = 4 * RET_WIDTH // POOL_WIDTH
    halo_per_tile = POOL_TM // HALO_ROWS
    kern = functools.partial(_pool_prompt_kernel, tiles_per_seq=tiles_per_seq)
    return pl.pallas_call(
        kern,
        grid=(m // POOL_TM,),
        in_specs=[
            pl.BlockSpec((POOL_TM, POOL_WIDTH), lambda i: (i, u_col)),
            pl.BlockSpec((HALO_ROWS, POOL_WIDTH),
                         lambda i: (jnp.maximum(i * halo_per_tile - 1, 0), u_col)),
            pl.BlockSpec(w_pool_bf.shape, lambda i: (0, 0, 0)),
            pl.BlockSpec((1, POOL_WIDTH), lambda i: (0, 0)),
            pl.BlockSpec(memory_space=pl.ANY),
        ],
        out_specs=pl.BlockSpec((POOL_TM, POOL_WIDTH), lambda i: (i, 1)),
        out_shape=jax.ShapeDtypeStruct(mix.shape, BF16),
        input_output_aliases={4: 0},
        scratch_shapes=[pltpu.VMEM((HALO_ROWS + POOL_TM, POOL_WIDTH), F32)],
        compiler_params=_params("parallel"),
        name="pool_prompt",
    )(proj, proj, w_pool_bf, scale, mix)


def _pool_sample_kernel(u_ref, buf_ref, wp_ref, sc_ref, *rest):
    o_ref, nbuf_ref = rest[-2:]
    nbuf_ref[:, 0:POOL_BUF - 1, :] = buf_ref[:, 1:POOL_BUF, :]
    nbuf_ref[:, POOL_BUF - 1, :] = u_ref[...]
    row = lax.broadcasted_iota(jnp.int32, (1, POOL_BUF, 1), 1)
    for gi, w in enumerate(POOL_WINDOWS):
        cols = pl.ds(gi * POOL_GROUP_DIM, POOL_GROUP_DIM)
        u = u_ref[:, cols]
        past = jnp.where(row >= POOL_BUF - (w - 1), buf_ref[:, :, cols], 0.0)
        s = u + jnp.sum(past, axis=1)
        pooled = s / float(w) - u
        mixed = jnp.dot(pooled.astype(BF16), wp_ref[gi], preferred_element_type=F32)
        o_ref[:, cols] = (mixed * sc_ref[:, cols]).astype(BF16)


def _pool_sample(proj, buf_all, li, w_pool_bf, scale, mix, new_buf_all):
    n = proj.shape[0]
    u_col = 4 * RET_WIDTH // POOL_WIDTH
    buf_spec = pl.BlockSpec((None, SAMPLE_NB, POOL_BUF, POOL_WIDTH), lambda i: (li, i, 0, 0))
    in_specs = [
        pl.BlockSpec((SAMPLE_NB, POOL_WIDTH), lambda i: (i, u_col)),
        buf_spec,
        pl.BlockSpec(w_pool_bf.shape, lambda i: (0, 0, 0)),
        pl.BlockSpec((1, POOL_WIDTH), lambda i: (0, 0)),
        pl.BlockSpec(memory_space=pl.ANY),
    ]
    args = [proj, buf_all, w_pool_bf, scale, mix]
    aliases = {4: 0}
    if new_buf_all is not None:
        in_specs.append(pl.BlockSpec(memory_space=pl.ANY))
        args.append(new_buf_all)
        aliases[5] = 1
    return pl.pallas_call(
        _pool_sample_kernel,
        grid=(n // SAMPLE_NB,),
        in_specs=in_specs,
        out_specs=[pl.BlockSpec((SAMPLE_NB, POOL_WIDTH), lambda i: (i, 1)), buf_spec],
        out_shape=[jax.ShapeDtypeStruct(mix.shape, BF16),
                   jax.ShapeDtypeStruct(buf_all.shape, F32)],
        input_output_aliases=aliases,
        compiler_params=_params("parallel"),
        name="pool_sample",
    )(*args)


def _out_proj_kernel(mix_ref, w_ref, x_ref, g_ref, b_ref, y_hbm, ybf_hbm,
                     acc, stage_f, stage_b, sems):
    i = pl.program_id(0)
    j = pl.program_id(1)
    tm = mix_ref.shape[0]
    tn = w_ref.shape[1]
    for c in range(tn // SUB_TN):
        sub = pl.ds(c * SUB_TN, SUB_TN)
        cols = pl.ds(pl.multiple_of(j * tn + c * SUB_TN, SUB_TN), SUB_TN)
        mix = jnp.dot(mix_ref[...], w_ref[:, sub], preferred_element_type=F32)
        acc[:, cols] = DN_ALPHA * x_ref[:, sub] + mix

    @pl.when(j == pl.num_programs(1) - 1)
    def _():
        _layer_norm_store(acc, g_ref, b_ref, y_hbm, ybf_hbm, stage_f, stage_b, sems, i * tm)


def _out_proj(mix, w_o_bf, x, g, b, tm):
    m = x.shape[0]
    return pl.pallas_call(
        _out_proj_kernel,
        grid=(m // tm, D_MODEL // OUT_TN),
        in_specs=[
            pl.BlockSpec((tm, D_MODEL), lambda i, j: (i, 0)),
            pl.BlockSpec((D_MODEL, OUT_TN), lambda i, j: (0, j)),
            pl.BlockSpec((tm, OUT_TN), lambda i, j: (i, j)),
            pl.BlockSpec((1, D_MODEL), lambda i, j: (0, 0)),
            pl.BlockSpec((1, D_MODEL), lambda i, j: (0, 0)),
        ],
        out_specs=[pl.BlockSpec(memory_space=pl.ANY), pl.BlockSpec(memory_space=pl.ANY)],
        out_shape=[jax.ShapeDtypeStruct((m, D_MODEL), F32),
                   jax.ShapeDtypeStruct((m, D_MODEL), BF16)],
        scratch_shapes=[pltpu.VMEM((tm, D_MODEL), F32)] + _ln_scratch(),
        compiler_params=_params("arbitrary", "arbitrary"),
        name="out_proj",
    )(mix, w_o_bf, x, g, b)


def _shift_rows(x, carry_row, first8):
    rolled = pltpu.roll(x, 1, axis=0)
    fill = 0.0 if carry_row is None else carry_row
    head = jnp.where(first8 == 0, fill, rolled[0:SUBLANES, :])
    return jnp.concatenate([head, rolled[SUBLANES:, :]], axis=0)


def _ffn_up_kernel(xp_ref, xs_ref, wa_ref, wb_ref, cw_ref, cb_ref, sc_ref, wd_ref, *rest):
    hp_ref, tail_ref, hs_ref, nsc_ref, wdbf_ref, w_scr = rest[-6:]
    i = pl.program_id(0)
    tm = xp_ref.shape[0]
    tn = wa_ref.shape[1]
    w_scr[:, 0:tn] = wa_ref[...].astype(BF16)
    w_scr[:, tn:2 * tn] = wb_ref[...].astype(BF16)
    w = w_scr[...]
    cw0 = cw_ref[0:1, :]
    cw1 = cw_ref[1:2, :]
    cw2 = cw_ref[2:3, :]
    cb = cb_ref[...]

    wdbf_ref[...] = wd_ref[...].astype(BF16)

    first8 = lax.broadcasted_iota(jnp.int32, (SUBLANES, tn), 0)
    p_last = None
    q_last = None
    for r in range(tm // DOT_ROWS):
        r0 = r * DOT_ROWS
        ab = jnp.dot(xp_ref[pl.ds(r0, DOT_ROWS), :], w, preferred_element_type=F32)
        a = ab[:, 0:tn]
        p = a * cw0
        q = a * cw1 + _shift_rows(p, p_last, first8)
        conv = (cb + a * cw2) + _shift_rows(q, q_last, first8)
        p_last = p[DOT_ROWS - 1:DOT_ROWS, :]
        q_last = q[DOT_ROWS - 1:DOT_ROWS, :]
        hp_ref[pl.ds(r0, DOT_ROWS), :] = (_gelu_exact(conv) * ab[:, tn:2 * tn]).astype(BF16)
    tail_ref[0] = a[DOT_ROWS - SUBLANES:DOT_ROWS, :]

    @pl.when(i == 0)
    def _():
        ab_s = jnp.dot(xs_ref[...], w_scr[...], preferred_element_type=F32)
        a_s = ab_s[:, 0:tn]
        b_s = ab_s[:, tn:2 * tn]
        s1 = sc_ref[:, 1, :]
        conv_s = cb + sc_ref[:, 0, :] * cw0
        conv_s = conv_s + s1 * cw1
        conv_s = conv_s + a_s * cw2
        hs_ref[...] = (_gelu_exact(conv_s) * b_s).astype(BF16)
        nsc_ref[:, 0, :] = s1
        nsc_ref[:, 1, :] = a_s


def _ffn_up(xp_bf, xs_bf, w_up, conv_w, conv_b, state_all, new_state_all, w_down, li, seq):
    mp = xp_bf.shape[0]
    ns = xs_bf.shape[0]
    nj = D_FF // FFN_TN
    tm = seq
    assert (mp // tm) * nj * CAST_ROWS == D_FF
    first = lambda i, j: jnp.where(i == 0, j, nj - 1)
    state_spec = pl.BlockSpec((None, ns, CONV_WIDTH - 1, FFN_TN),
                              lambda i, j: (li, 0, 0, first(i, j)))
    in_specs = [
        pl.BlockSpec((tm, D_MODEL), lambda i, j: (i, 0), pipeline_mode=pl.Buffered(1)),
        pl.BlockSpec((ns, D_MODEL), lambda i, j: (0, 0)),
        pl.BlockSpec((None, D_MODEL, FFN_TN), lambda i, j: (li, 0, j)),
        pl.BlockSpec((None, D_MODEL, FFN_TN), lambda i, j: (li, 0, nj + j)),
        pl.BlockSpec((None, CONV_WIDTH, FFN_TN), lambda i, j: (li, 0, j)),
        pl.BlockSpec((None, 1, FFN_TN), lambda i, j: (li, 0, j)),
        state_spec,
        pl.BlockSpec((None, CAST_ROWS, D_MODEL), lambda i, j: (li, i * nj + j, 0)),
    ]
    args = [xp_bf, xs_bf, w_up, w_up, conv_w, conv_b, state_all, w_down]
    aliases = {}
    if new_state_all is not None:
        in_specs.append(pl.BlockSpec(memory_space=pl.ANY))
        args.append(new_state_all)
        aliases = {len(args) - 1: 3}
    return pl.pallas_call(
        _ffn_up_kernel,
        grid=(mp // tm, nj),
        in_specs=in_specs,
        out_specs=[
            pl.BlockSpec((tm, FFN_TN), lambda i, j: (i, j)),
            pl.BlockSpec((1, SUBLANES, FFN_TN), lambda i, j: (i, 0, j)),
            pl.BlockSpec((ns, FFN_TN), lambda i, j: (0, first(i, j))),
            state_spec,
            pl.BlockSpec((CAST_ROWS, D_MODEL), lambda i, j: (i * nj + j, 0)),
        ],
        out_shape=[jax.ShapeDtypeStruct((mp, D_FF), BF16),
                   jax.ShapeDtypeStruct((mp // tm, SUBLANES, D_FF), F32),
                   jax.ShapeDtypeStruct((ns, D_FF), BF16),
                   jax.ShapeDtypeStruct(state_all.shape, F32),
                   jax.ShapeDtypeStruct((D_FF, D_MODEL), BF16)],
        input_output_aliases=aliases,
        scratch_shapes=[pltpu.VMEM((D_MODEL, 2 * FFN_TN), BF16)],
        compiler_params=_params("arbitrary", "arbitrary"),
        name="ffn_up",
    )(*args)


def _ffn_down_kernel(h_ref, w_ref, x_ref, g_ref, b_ref, y_hbm, ybf_hbm,
                     acc, stage_f, stage_b, sems):
    i = pl.program_id(0)
    k = pl.program_id(1)
    j = pl.program_id(2)
    tm = h_ref.shape[0]
    tn = w_ref.shape[1]

    @pl.when((i == 0) & (k == 0) & (j == 0))
    def _():
        acc[...] = jnp.zeros_like(acc)

    for c in range(tn // SUB_TN):
        sub = pl.ds(c * SUB_TN, SUB_TN)
        cols = pl.ds(pl.multiple_of(j * tn + c * SUB_TN, SUB_TN), SUB_TN)
        part = jnp.dot(h_ref[...], w_ref[:, sub], preferred_element_type=F32)
        base = jnp.where(k == 0, DN_ALPHA * x_ref[:, sub], acc[:, cols])
        acc[:, cols] = base + part

    @pl.when((k == pl.num_programs(1) - 1) & (j == pl.num_programs(2) - 1))
    def _():
        _layer_norm_store(acc, g_ref, b_ref, y_hbm, ybf_hbm, stage_f, stage_b, sems, i * tm)


def _ffn_down(h, w_down_bf, x, g, b, tm):
    m = x.shape[0]
    nk = D_FF // DOWN_TK
    nj = D_MODEL // DOWN_TN
    return pl.pallas_call(
        _ffn_down_kernel,
        grid=(m // tm, nk, nj),
        in_specs=[
            pl.BlockSpec((tm, DOWN_TK), lambda i, k, j: (i, k)),
            pl.BlockSpec((DOWN_TK, DOWN_TN), lambda i, k, j: (k, j)),
            pl.BlockSpec((tm, DOWN_TN), lambda i, k, j: (i, jnp.where(k == 0, j, nj - 1))),
            pl.BlockSpec((1, D_MODEL), lambda i, k, j: (0, 0)),
            pl.BlockSpec((1, D_MODEL), lambda i, k, j: (0, 0)),
        ],
        out_specs=[pl.BlockSpec(memory_space=pl.ANY), pl.BlockSpec(memory_space=pl.ANY)],
        out_shape=[jax.ShapeDtypeStruct((m, D_MODEL), F32),
                   jax.ShapeDtypeStruct((m, D_MODEL), BF16)],
        scratch_shapes=[pltpu.VMEM((tm, D_MODEL), F32)] + _ln_scratch(),
        compiler_params=_params("arbitrary", "arbitrary", "arbitrary"),
        name="ffn_down",
    )(h, w_down_bf, x, g, b)


def _rotary_tables(pos):
    inv = ROPE_BASE ** (-jnp.arange(HALF_HEAD, dtype=F32) / HALF_HEAD)
    ang = pos.astype(F32)[:, None] * inv[None, :]
    return jnp.cos(ang), jnp.sin(ang)


def _retention_tables(l):
    log_g = jnp.log1p(-(2.0 ** (-5.0 - jnp.arange(RET_HEADS, dtype=F32))))
    i = jnp.arange(l)
    diff = i[:, None] - i[None, :]
    decay = jnp.where(diff[None] >= 0,
                      jnp.exp(jnp.maximum(diff, 0)[None].astype(F32) * log_g[:, None, None]), 0.0)
    xi = jnp.exp((i + 1)[None].astype(F32) * log_g[:, None])
    zeta = jnp.exp((l - 1 - i)[None].astype(F32) * log_g[:, None])
    g_l = jnp.exp(l * log_g)
    return decay, xi, zeta, g_l


def kernel(x_prompt, x_sample, state_ret, state_pool, state_conv, w_in, w_pool, pool_scale, w_o,
           ln1_g, ln1_b, w_up, conv_w, conv_b, w_down, ln2_g, ln2_b):
    batch, seq, _ = x_prompt.shape
    n_s = x_sample.shape[0]
    assert x_sample.shape[1] == 1 and seq % RET_CHUNK == 0
    mp = batch * seq

    cos_p, sin_p = _rotary_tables(jnp.arange(seq))
    cos_s, sin_s = _rotary_tables(PAST_LEN + jnp.arange(1))
    cos_s = jnp.broadcast_to(cos_s, (n_s, HALF_HEAD))
    sin_s = jnp.broadcast_to(sin_s, (n_s, HALF_HEAD))
    decay, xi, zeta, g_l = _retention_tables(RET_CHUNK)
    xi = xi[:, :, None]
    zeta = zeta[:, :, None]
    gl_b = jnp.broadcast_to(g_l[:, None, None], (RET_HEADS, 1, HEAD_DIM))
    _, xi_s, _, _ = _retention_tables(1)
    gam_s = jnp.broadcast_to(xi_s[:, :, None], (RET_HEADS, 1, HEAD_DIM))

    xp = x_prompt.reshape(mp, D_MODEL)
    xs = x_sample.reshape(n_s, D_MODEL)
    xp_bf = xp.astype(BF16)
    xs_bf = xs.astype(BF16)
    conv_b3 = conv_b[:, None, :]

    pool_p, conv_p = [], []
    new_ret_prompt = new_ret_sample = new_pool_sample = new_conv_sample = None
    for li in range(DEPTH):
        w_pool_bf = w_pool[li].astype(BF16)
        scale = pool_scale[li][None, :]
        g1, b1 = ln1_g[li][None, :], ln1_b[li][None, :]
        g2, b2 = ln2_g[li][None, :], ln2_b[li][None, :]

        proj_p, proj_s, w_o_bf = _proj(xp_bf, xs_bf, w_in, w_o, li, cos_p, sin_p, cos_s, sin_s, seq)

        mix, new_ret_prompt = _ret_prompt(proj_p, decay, xi, zeta, gl_b, li, new_ret_prompt,
                                          batch, seq)
        mix = _pool_prompt(proj_p, w_pool_bf, scale, mix, seq)
        xp, xp_bf = _out_proj(mix, w_o_bf, xp, g1, b1, LN_TM)
        pool_p.append(proj_p.reshape(batch, seq, IN_WIDTH)[:, seq - POOL_BUF:, 4 * RET_WIDTH:])

        mix, new_ret_sample = _ret_sample(proj_s, state_ret, li, gam_s, new_ret_sample)
        mix, new_pool_sample = _pool_sample(proj_s, state_pool, li, w_pool_bf, scale, mix,
                                            new_pool_sample)
        xs, xs_bf = _out_proj(mix, w_o_bf, xs, g1, b1, n_s)

        h_p, tail, h_s, new_conv_sample, w_down_bf = _ffn_up(
            xp_bf, xs_bf, w_up, conv_w, conv_b3, state_conv, new_conv_sample, w_down, li, seq)
        xp, xp_bf = _ffn_down(h_p, w_down_bf, xp, g2, b2, LN_TM)
        xs, xs_bf = _ffn_down(h_s, w_down_bf, xs, g2, b2, n_s)
        conv_p.append(tail[:, SUBLANES - (CONV_WIDTH - 1):, :])

    return (xp.reshape(batch, seq, D_MODEL), xs.reshape(n_s, 1, D_MODEL),
            new_ret_prompt, new_ret_sample, jnp.stack(pool_p), new_pool_sample,
            jnp.stack(conv_p), new_conv_sample)
```

```python
import functools

import jax
import jax.numpy as jnp
from jax import lax
from jax.experimental import pallas as pl
from jax.experimental.pallas import tpu as pltpu

F32 = jnp.float32
BF16 = jnp.bfloat16

D_MODEL = 4096
DEPTH = 2
PAST_LEN = 16384
RET_WIDTH = D_MODEL // 2
RET_HEADS = 8
HEAD_DIM = RET_WIDTH // RET_HEADS
HALF_HEAD = HEAD_DIM // 2
RET_CHUNK = 128
ROPE_BASE = 10000.0
POOL_WIDTH = D_MODEL - RET_WIDTH
POOL_WINDOWS = (2, 4, 8, 16)
POOL_GROUP_DIM = POOL_WIDTH // len(POOL_WINDOWS)
POOL_BUF = max(POOL_WINDOWS) - 1
IN_WIDTH = 4 * RET_WIDTH + POOL_WIDTH
D_FF = ((8 * D_MODEL // 3) + 255) // 256 * 256
CONV_WIDTH = 3
DN_ALPHA = (2.0 * DEPTH) ** 0.25
LN_EPS = 1e-5
K_SCALE = HEAD_DIM ** -0.5

V7X_VMEM_BYTES = 64 * 1024 * 1024
VMEM_LIMIT = V7X_VMEM_BYTES - 8 * 1024 * 1024
SUBLANES = 8
HALO_ROWS = 2 * SUBLANES

PROJ_TN = 512
FFN_TN = 256
LN_TM = 512
OUT_TN = 1024
SUB_TN = 512
LN_ROWS = 64
DOWN_TK = D_FF // 2
DOWN_TN = 1024
POOL_TM = 512
SAMPLE_NB = 16
DOT_ROWS = 1024
CAST_ROWS = 64


def _params(*semantics):
    return pltpu.CompilerParams(dimension_semantics=semantics, vmem_limit_bytes=VMEM_LIMIT)


def _layer_norm(y, g, b):
    mu = jnp.mean(y, axis=-1, keepdims=True)
    d = y - mu
    var = jnp.mean(d * d, axis=-1, keepdims=True)
    return d * lax.rsqrt(var + LN_EPS) * g + b


def _layer_norm_store(acc_ref, g_ref, b_ref, y_hbm, ybf_hbm, stage_f, stage_b, sems, row0):
    g = g_ref[...]
    b = b_ref[...]
    n = acc_ref.shape[0] // LN_ROWS

    def copies(r, slot):
        dst = pl.ds(pl.multiple_of(row0 + r * LN_ROWS, LN_ROWS), LN_ROWS)
        return (pltpu.make_async_copy(stage_f.at[slot], y_hbm.at[dst, :], sems.at[0, slot]),
                pltpu.make_async_copy(stage_b.at[slot], ybf_hbm.at[dst, :], sems.at[1, slot]))

    def body(r, carry):
        slot = r % 2

        @pl.when(r >= 2)
        def _():
            for c in copies(r - 2, slot):
                c.wait()

        rows = pl.ds(pl.multiple_of(r * LN_ROWS, LN_ROWS), LN_ROWS)
        y = _layer_norm(acc_ref[rows, :], g, b)
        stage_f[slot] = y
        stage_b[slot] = y.astype(BF16)
        for c in copies(r, slot):
            c.start()
        return carry

    lax.fori_loop(0, n, body, 0)
    for r in range(max(n - 2, 0), n):
        for c in copies(r, r % 2):
            c.wait()


def _ln_scratch():
    return [pltpu.VMEM((2, LN_ROWS, D_MODEL), F32), pltpu.VMEM((2, LN_ROWS, D_MODEL), BF16),
            pltpu.SemaphoreType.DMA((2, 2))]


def _gelu_exact(x):
    return 0.5 * x * (1.0 + lax.erf(x * (0.5 ** 0.5)))


def _rotate_heads(o_ref, cos_ref, sin_ref, scale):
    cos = cos_ref[...]
    sin = sin_ref[...]
    for h in range(o_ref.shape[1] // HEAD_DIM):
        lo = pl.ds(h * HEAD_DIM, HALF_HEAD)
        hi = pl.ds(h * HEAD_DIM + HALF_HEAD, HALF_HEAD)
        x1 = o_ref[:, lo]
        x2 = o_ref[:, hi]
        o_ref[:, lo] = (x1 * cos - x2 * sin) * scale
        o_ref[:, hi] = (x2 * cos + x1 * sin) * scale


def _proj_kernel(xp_ref, xs_ref, w_ref, cosp_ref, sinp_ref, coss_ref, sins_ref, wo_ref,
                 op_ref, os_ref, wobf_ref, *, rot_tiles, q_tiles, cast_steps):
    i = pl.program_id(0)
    j = pl.program_id(1)
    w = w_ref[...].astype(BF16)
    scale = jnp.where(j >= q_tiles, K_SCALE, 1.0).astype(F32)
    for r in range(xp_ref.shape[0] // DOT_ROWS):
        rows = pl.ds(r * DOT_ROWS, DOT_ROWS)
        op_ref[rows, :] = jnp.dot(xp_ref[rows, :], w, preferred_element_type=F32)

    @pl.when(i * pl.num_programs(1) + j < cast_steps)
    def _():
        for c in range(wobf_ref.shape[0]):
            wobf_ref[c] = wo_ref[:, c * OUT_TN:(c + 1) * OUT_TN].astype(BF16)

    @pl.when(j < rot_tiles)
    def _():
        _rotate_heads(op_ref, cosp_ref, sinp_ref, scale)

    @pl.when(i == 0)
    def _():
        os_ref[...] = jnp.dot(xs_ref[...], w, preferred_element_type=F32)

    @pl.when((i == 0) & (j < rot_tiles))
    def _():
        _rotate_heads(os_ref, coss_ref, sins_ref, scale)


def _proj(xp_bf, xs_bf, w_in, w_o, li, cos_p, sin_p, cos_s, sin_s, tm):
    mp = xp_bf.shape[0]
    ns = xs_bf.shape[0]
    nj = IN_WIDTH // PROJ_TN
    t_tiles = cos_p.shape[0] // tm
    cast_steps = D_MODEL // CAST_ROWS
    assert cast_steps <= (mp // tm) * nj
    kern = functools.partial(_proj_kernel, rot_tiles=2 * RET_WIDTH // PROJ_TN,
                             q_tiles=RET_WIDTH // PROJ_TN, cast_steps=cast_steps)
    whole = lambda a: pl.BlockSpec(a.shape, lambda i, j: (0, 0))
    slab = lambda i, j: jnp.minimum(i * nj + j, cast_steps - 1)
    return pl.pallas_call(
        kern,
        grid=(mp // tm, nj),
        in_specs=[
            pl.BlockSpec((tm, D_MODEL), lambda i, j: (i, 0), pipeline_mode=pl.Buffered(1)),
            whole(xs_bf),
            pl.BlockSpec((None, D_MODEL, PROJ_TN), lambda i, j: (li, 0, j)),
            pl.BlockSpec((tm, HALF_HEAD), lambda i, j: (i % t_tiles, 0)),
            pl.BlockSpec((tm, HALF_HEAD), lambda i, j: (i % t_tiles, 0)),
            whole(cos_s), whole(sin_s),
            pl.BlockSpec((None, CAST_ROWS, D_MODEL), lambda i, j: (li, slab(i, j), 0)),
        ],
        out_specs=[
            pl.BlockSpec((tm, PROJ_TN), lambda i, j: (i, j)),
            pl.BlockSpec((ns, PROJ_TN), lambda i, j: (0, jnp.where(i == 0, j, nj - 1))),
            pl.BlockSpec((D_MODEL // OUT_TN, CAST_ROWS, OUT_TN), lambda i, j: (0, slab(i, j), 0)),
        ],
        out_shape=[jax.ShapeDtypeStruct((mp, IN_WIDTH), F32),
                   jax.ShapeDtypeStruct((ns, IN_WIDTH), F32),
                   jax.ShapeDtypeStruct((D_MODEL // OUT_TN, D_MODEL, OUT_TN), BF16)],
        compiler_params=_params("arbitrary", "arbitrary"),
        name="proj",
    )(xp_bf, xs_bf, w_in, cos_p, sin_p, cos_s, sin_s, w_o)


def _group_norm_gate(o, g):
    mu = jnp.mean(o, axis=-1, keepdims=True)
    d = o - mu
    var = jnp.mean(d * d, axis=-1, keepdims=True)
    return d * lax.rsqrt(var + LN_EPS) * (g * jax.nn.sigmoid(g))


def _ret_prompt_kernel(q_ref, k_ref, v_ref, g_ref, decay_ref, xi_ref, zeta_ref, gl_ref, *rest):
    o_ref, snew_ref, s_scr = rest[-3:]
    c = pl.program_id(1)

    @pl.when(c == 0)
    def _():
        s_scr[...] = jnp.zeros_like(s_scr)

    for h in range(RET_HEADS):
        cols = pl.ds(h * HEAD_DIM, HEAD_DIM)
        q = q_ref[:, cols]
        k = k_ref[:, cols]
        vb = v_ref[:, cols].astype(BF16)
        s = s_scr[h]
        scores = lax.dot_general(q.astype(BF16), k.astype(BF16), (((1,), (1,)), ((), ())),
                                 preferred_element_type=F32) * decay_ref[h]
        o = jnp.dot(scores.astype(BF16), vb, preferred_element_type=F32)
        o = o + jnp.dot((q * xi_ref[h]).astype(BF16), s.astype(BF16), preferred_element_type=F32)
        kz_t = (k * zeta_ref[h]).T.astype(BF16)
        s_scr[h] = gl_ref[h] * s + jnp.dot(kz_t, vb, preferred_element_type=F32)
        o_ref[:, cols] = _group_norm_gate(o, g_ref[:, cols]).astype(BF16)

    @pl.when(c == pl.num_programs(1) - 1)
    def _():
        snew_ref[...] = s_scr[...]


def _ret_prompt(proj, decay, xi, zeta, gl, li, new_state_all, batch, seq):
    nc = seq // RET_CHUNK

    def col(cb):
        return pl.BlockSpec((RET_CHUNK, RET_WIDTH), lambda b, c: (b * nc + c, cb))

    def whole(a):
        return pl.BlockSpec(a.shape, lambda b, c: (0,) * a.ndim)

    in_specs = [col(0), col(1), col(2), col(3), whole(decay), whole(xi), whole(zeta), whole(gl)]
    args = [proj, proj, proj, proj, decay, xi, zeta, gl]
    aliases = {}
    if new_state_all is not None:
        in_specs.append(pl.BlockSpec(memory_space=pl.ANY))
        args.append(new_state_all)
        aliases = {len(args) - 1: 1}
    return pl.pallas_call(
        _ret_prompt_kernel,
        grid=(batch, nc),
        in_specs=in_specs,
        out_specs=[
            pl.BlockSpec((RET_CHUNK, RET_WIDTH), lambda b, c: (b * nc + c, 0)),
            pl.BlockSpec((None, None, RET_HEADS, HEAD_DIM, HEAD_DIM), lambda b, c: (li, b, 0, 0, 0)),
        ],
        out_shape=[
            jax.ShapeDtypeStruct((batch * seq, D_MODEL), BF16),
            jax.ShapeDtypeStruct((DEPTH, batch, RET_HEADS, HEAD_DIM, HEAD_DIM), F32),
        ],
        input_output_aliases=aliases,
        scratch_shapes=[pltpu.VMEM((RET_HEADS, HEAD_DIM, HEAD_DIM), F32)],
        compiler_params=_params("parallel", "arbitrary"),
        name="ret_prompt",
    )(*args)


def _ret_sample_kernel(q_ref, k_ref, v_ref, g_ref, s0_ref, gam_ref, *rest):
    o_ref, snew_ref = rest[-2:]
    q = q_ref[...]
    k = k_ref[...]
    v = v_ref[...]
    vb = v.astype(BF16)
    gam = gam_ref[0]
    qg = (q * gam).astype(BF16)
    rows = lax.broadcasted_iota(jnp.int32, (SAMPLE_NB, 1), 0)
    o_state = jnp.zeros((SAMPLE_NB, HEAD_DIM), F32)
    for n in range(SAMPLE_NB):
        s = s0_ref[n, 0]
        r = jnp.dot(qg, s.astype(BF16), preferred_element_type=F32)
        o_state = jnp.where(rows == n, r, o_state)
        k_n_t = jnp.where(rows == n, k, 0.0).T.astype(BF16)
        snew_ref[n, 0] = gam * s + jnp.dot(k_n_t, vb, preferred_element_type=F32)
    qk = jnp.sum(q * k, axis=-1, keepdims=True)
    o = qk * v + o_state
    o_ref[...] = _group_norm_gate(o, g_ref[...]).astype(BF16)


def _ret_sample(proj, state_all, li, gam, new_state_all):
    n = proj.shape[0]

    def col(cb):
        return pl.BlockSpec((SAMPLE_NB, HEAD_DIM), lambda i, h: (i, cb * RET_HEADS + h))

    state_spec = pl.BlockSpec((None, SAMPLE_NB, 1, HEAD_DIM, HEAD_DIM),
                              lambda i, h: (li, i, h, 0, 0))
    in_specs = [col(0), col(1), col(2), col(3), state_spec,
                pl.BlockSpec((1, 1, HEAD_DIM), lambda i, h: (h, 0, 0))]
    args = [proj, proj, proj, proj, state_all, gam]
    aliases = {}
    if new_state_all is not None:
        in_specs.append(pl.BlockSpec(memory_space=pl.ANY))
        args.append(new_state_all)
        aliases = {len(args) - 1: 1}
    return pl.pallas_call(
        _ret_sample_kernel,
        grid=(n // SAMPLE_NB, RET_HEADS),
        in_specs=in_specs,
        out_specs=[pl.BlockSpec((SAMPLE_NB, HEAD_DIM), lambda i, h: (i, h)), state_spec],
        out_shape=[jax.ShapeDtypeStruct((n, D_MODEL), BF16),
                   jax.ShapeDtypeStruct(state_all.shape, F32)],
        input_output_aliases=aliases,
        compiler_params=_params("parallel", "arbitrary"),
        name="ret_sample",
    )(*args)


def _pool_prompt_kernel(u_ref, halo_ref, wp_ref, sc_ref, mix_hbm, o_ref, ext_scr, *,
                        tiles_per_seq):
    del mix_hbm
    i = pl.program_id(0)
    tm = u_ref.shape[0]
    t0 = (i % tiles_per_seq) * tm
    ext_scr[0:HALO_ROWS, :] = jnp.where(t0 == 0, 0.0, halo_ref[...])
    ext_scr[HALO_ROWS:HALO_ROWS + tm, :] = u_ref[...]
    pos = t0 + lax.broadcasted_iota(jnp.int32, (tm, 1), 0)
    for gi, w in enumerate(POOL_WINDOWS):
        cols = pl.ds(gi * POOL_GROUP_DIM, POOL_GROUP_DIM)
        u = u_ref[:, cols]
        s = u
        for back in range(1, w):
            s = s + ext_scr[pl.ds(HALO_ROWS - back, tm), cols]
        cnt = jnp.minimum(w, pos + 1).astype(F32)
        pooled = s / cnt - u
        mixed = jnp.dot(pooled.astype(BF16), wp_ref[gi], preferred_element_type=F32)
        o_ref[:, cols] = (mixed * sc_ref[:, cols]).astype(BF16)


def _pool_prompt(proj, w_pool_bf, scale, mix, seq):
    m = proj.shape[0]
    tiles_per_seq = seq // POOL_TM
    u_col = 4 * RET_WIDTH // POOL_WIDTH
    halo_per_tile = POOL_TM // HALO_ROWS
    kern = functools.partial(_pool_prompt_kernel, tiles_per_seq=tiles_per_seq)
    return pl.pallas_call(
        kern,
        grid=(m // POOL_TM,),
        in_specs=[
            pl.BlockSpec((POOL_TM, POOL_WIDTH), lambda i: (i, u_col)),
            pl.BlockSpec((HALO_ROWS, POOL_WIDTH),
                         lambda i: (jnp.maximum(i * halo_per_tile - 1, 0), u_col)),
            pl.BlockSpec(w_pool_bf.shape, lambda i: (0, 0, 0)),
            pl.BlockSpec((1, POOL_WIDTH), lambda i: (0, 0)),
            pl.BlockSpec(memory_space=pl.ANY),
        ],
        out_specs=pl.BlockSpec((POOL_TM, POOL_WIDTH), lambda i: (i, 1)),
        out_shape=jax.ShapeDtypeStruct(mix.shape, BF16),
        input_output_aliases={4: 0},
        scratch_shapes=[pltpu.VMEM((HALO_ROWS + POOL_TM, POOL_WIDTH), F32)],
        compiler_params=_params("parallel"),
        name="pool_prompt",
    )(proj, proj, w_pool_bf, scale, mix)


def _pool_sample_kernel(u_ref, buf_ref, wp_ref, sc_ref, *rest):
    o_ref, nbuf_ref = rest[-2:]
    nbuf_ref[:, 0:POOL_BUF - 1, :] = buf_ref[:, 1:POOL_BUF, :]
    nbuf_ref[:, POOL_BUF - 1, :] = u_ref[...]
    row = lax.broadcasted_iota(jnp.int32, (1, POOL_BUF, 1), 1)
    for gi, w in enumerate(POOL_WINDOWS):
        cols = pl.ds(gi * POOL_GROUP_DIM, POOL_GROUP_DIM)
        u = u_ref[:, cols]
        past = jnp.where(row >= POOL_BUF - (w - 1), buf_ref[:, :, cols], 0.0)
        s = u + jnp.sum(past, axis=1)
        pooled = s / float(w) - u
        mixed = jnp.dot(pooled.astype(BF16), wp_ref[gi], preferred_element_type=F32)
        o_ref[:, cols] = (mixed * sc_ref[:, cols]).astype(BF16)


def _pool_sample(proj, buf_all, li, w_pool_bf, scale, mix, new_buf_all):
    n = proj.shape[0]
    u_col = 4 * RET_WIDTH // POOL_WIDTH
    buf_spec = pl.BlockSpec((None, SAMPLE_NB, POOL_BUF, POOL_WIDTH), lambda i: (li, i, 0, 0))
    in_specs = [
        pl.BlockSpec((SAMPLE_NB, POOL_WIDTH), lambda i: (i, u_col)),
        buf_spec,
        pl.BlockSpec(w_pool_bf.shape, lambda i: (0, 0, 0)),
        pl.BlockSpec((1, POOL_WIDTH), lambda i: (0, 0)),
        pl.BlockSpec(memory_space=pl.ANY),
    ]
    args = [proj, buf_all, w_pool_bf, scale, mix]
    aliases = {4: 0}
    if new_buf_all is not None:
        in_specs.append(pl.BlockSpec(memory_space=pl.ANY))
        args.append(new_buf_all)
        aliases[5] = 1
    return pl.pallas_call(
        _pool_sample_kernel,
        grid=(n // SAMPLE_NB,),
        in_specs=in_specs,
        out_specs=[pl.BlockSpec((SAMPLE_NB, POOL_WIDTH), lambda i: (i, 1)), buf_spec],
        out_shape=[jax.ShapeDtypeStruct(mix.shape, BF16),
                   jax.ShapeDtypeStruct(buf_all.shape, F32)],
        input_output_aliases=aliases,
        compiler_params=_params("parallel"),
        name="pool_sample",
    )(*args)


def _out_proj_kernel(mix_ref, w_ref, x_ref, g_ref, b_ref, y_hbm, ybf_hbm,
                     acc, stage_f, stage_b, sems):
    i = pl.program_id(0)
    j = pl.program_id(1)
    tm = mix_ref.shape[0]
    tn = w_ref.shape[1]
    for c in range(tn // SUB_TN):
        sub = pl.ds(c * SUB_TN, SUB_TN)
        cols = pl.ds(pl.multiple_of(j * tn + c * SUB_TN, SUB_TN), SUB_TN)
        mix = jnp.dot(mix_ref[...], w_ref[:, sub], preferred_element_type=F32)
        acc[:, cols] = DN_ALPHA * x_ref[:, sub] + mix

    @pl.when(j == pl.num_programs(1) - 1)
    def _():
        _layer_norm_store(acc, g_ref, b_ref, y_hbm, ybf_hbm, stage_f, stage_b, sems, i * tm)


def _out_proj(mix, w_o_bf, x, g, b, tm):
    m = x.shape[0]
    return pl.pallas_call(
        _out_proj_kernel,
        grid=(m // tm, D_MODEL // OUT_TN),
        in_specs=[
            pl.BlockSpec((tm, D_MODEL), lambda i, j: (i, 0)),
            pl.BlockSpec((None, D_MODEL, OUT_TN), lambda i, j: (j, 0, 0)),
            pl.BlockSpec((tm, OUT_TN), lambda i, j: (i, j)),
            pl.BlockSpec((1, D_MODEL), lambda i, j: (0, 0)),
            pl.BlockSpec((1, D_MODEL), lambda i, j: (0, 0)),
        ],
        out_specs=[pl.BlockSpec(memory_space=pl.ANY), pl.BlockSpec(memory_space=pl.ANY)],
        out_shape=[jax.ShapeDtypeStruct((m, D_MODEL), F32),
                   jax.ShapeDtypeStruct((m, D_MODEL), BF16)],
        scratch_shapes=[pltpu.VMEM((tm, D_MODEL), F32)] + _ln_scratch(),
        compiler_params=_params("arbitrary", "arbitrary"),
        name="out_proj",
    )(mix, w_o_bf, x, g, b)


def _shift_rows(x, carry_row, first8):
    rolled = pltpu.roll(x, 1, axis=0)
    fill = 0.0 if carry_row is None else carry_row
    head = jnp.where(first8 == 0, fill, rolled[0:SUBLANES, :])
    return jnp.concatenate([head, rolled[SUBLANES:, :]], axis=0)


def _ffn_up_kernel(xp_ref, xs_ref, wa_ref, wb_ref, cw_ref, cb_ref, sc_ref, wd_ref, *rest):
    hp_ref, tail_ref, hs_ref, nsc_ref, wdbf_ref, w_scr = rest[-6:]
    i = pl.program_id(0)
    tm = xp_ref.shape[0]
    tn = wa_ref.shape[1]
    w_scr[:, 0:tn] = wa_ref[...].astype(BF16)
    w_scr[:, tn:2 * tn] = wb_ref[...].astype(BF16)
    w = w_scr[...]
    cw0 = cw_ref[0:1, :]
    cw1 = cw_ref[1:2, :]
    cw2 = cw_ref[2:3, :]
    cb = cb_ref[...]

    for c in range(wdbf_ref.shape[0]):
        wdbf_ref[c] = wd_ref[:, c * DOWN_TN:(c + 1) * DOWN_TN].astype(BF16)

    first8 = lax.broadcasted_iota(jnp.int32, (SUBLANES, tn), 0)
    p_last = None
    q_last = None
    for r in range(tm // DOT_ROWS):
        r0 = r * DOT_ROWS
        ab = jnp.dot(xp_ref[pl.ds(r0, DOT_ROWS), :], w, preferred_element_type=F32)
        a = ab[:, 0:tn]
        p = a * cw0
        q = a * cw1 + _shift_rows(p, p_last, first8)
        conv = (cb + a * cw2) + _shift_rows(q, q_last, first8)
        p_last = p[DOT_ROWS - 1:DOT_ROWS, :]
        q_last = q[DOT_ROWS - 1:DOT_ROWS, :]
        hp_ref[pl.ds(r0, DOT_ROWS), :] = (_gelu_exact(conv) * ab[:, tn:2 * tn]).astype(BF16)
    tail_ref[0] = a[DOT_ROWS - SUBLANES:DOT_ROWS, :]

    @pl.when(i == 0)
    def _():
        ab_s = jnp.dot(xs_ref[...], w_scr[...], preferred_element_type=F32)
        a_s = ab_s[:, 0:tn]
        b_s = ab_s[:, tn:2 * tn]
        s1 = sc_ref[:, 1, :]
        conv_s = cb + sc_ref[:, 0, :] * cw0
        conv_s = conv_s + s1 * cw1
        conv_s = conv_s + a_s * cw2
        hs_ref[...] = (_gelu_exact(conv_s) * b_s).astype(BF16)
        nsc_ref[:, 0, :] = s1
        nsc_ref[:, 1, :] = a_s


def _ffn_up(xp_bf, xs_bf, w_up, conv_w, conv_b, state_all, new_state_all, w_down, li, seq):
    mp = xp_bf.shape[0]
    ns = xs_bf.shape[0]
    nj = D_FF // FFN_TN
    tm = seq
    assert (mp // tm) * nj * CAST_ROWS == D_FF
    first = lambda i, j: jnp.where(i == 0, j, nj - 1)
    state_spec = pl.BlockSpec((None, ns, CONV_WIDTH - 1, FFN_TN),
                              lambda i, j: (li, 0, 0, first(i, j)))
    in_specs = [
        pl.BlockSpec((tm, D_MODEL), lambda i, j: (i, 0), pipeline_mode=pl.Buffered(1)),
        pl.BlockSpec((ns, D_MODEL), lambda i, j: (0, 0)),
        pl.BlockSpec((None, D_MODEL, FFN_TN), lambda i, j: (li, 0, j)),
        pl.BlockSpec((None, D_MODEL, FFN_TN), lambda i, j: (li, 0, nj + j)),
        pl.BlockSpec((None, CONV_WIDTH, FFN_TN), lambda i, j: (li, 0, j)),
        pl.BlockSpec((None, 1, FFN_TN), lambda i, j: (li, 0, j)),
        state_spec,
        pl.BlockSpec((None, CAST_ROWS, D_MODEL), lambda i, j: (li, i * nj + j, 0)),
    ]
    args = [xp_bf, xs_bf, w_up, w_up, conv_w, conv_b, state_all, w_down]
    aliases = {}
    if new_state_all is not None:
        in_specs.append(pl.BlockSpec(memory_space=pl.ANY))
        args.append(new_state_all)
        aliases = {len(args) - 1: 3}
    return pl.pallas_call(
        _ffn_up_kernel,
        grid=(mp // tm, nj),
        in_specs=in_specs,
        out_specs=[
            pl.BlockSpec((tm, FFN_TN), lambda i, j: (i, j)),
            pl.BlockSpec((1, SUBLANES, FFN_TN), lambda i, j: (i, 0, j)),
            pl.BlockSpec((ns, FFN_TN), lambda i, j: (0, first(i, j))),
            state_spec,
            pl.BlockSpec((D_MODEL // DOWN_TN, CAST_ROWS, DOWN_TN), lambda i, j: (0, i * nj + j, 0)),
        ],
        out_shape=[jax.ShapeDtypeStruct((mp, D_FF), BF16),
                   jax.ShapeDtypeStruct((mp // tm, SUBLANES, D_FF), F32),
                   jax.ShapeDtypeStruct((ns, D_FF), BF16),
                   jax.ShapeDtypeStruct(state_all.shape, F32),
                   jax.ShapeDtypeStruct((D_MODEL // DOWN_TN, D_FF, DOWN_TN), BF16)],
        input_output_aliases=aliases,
        scratch_shapes=[pltpu.VMEM((D_MODEL, 2 * FFN_TN), BF16)],
        compiler_params=_params("arbitrary", "arbitrary"),
        name="ffn_up",
    )(*args)


def _ffn_down_kernel(h_ref, w_ref, x_ref, g_ref, b_ref, y_hbm, ybf_hbm,
                     acc, stage_f, stage_b, sems):
    i = pl.program_id(0)
    k = pl.program_id(1)
    j = pl.program_id(2)
    tm = h_ref.shape[0]
    tn = w_ref.shape[1]

    @pl.when((i == 0) & (k == 0) & (j == 0))
    def _():
        acc[...] = jnp.zeros_like(acc)

    for c in range(tn // SUB_TN):
        sub = pl.ds(c * SUB_TN, SUB_TN)
        cols = pl.ds(pl.multiple_of(j * tn + c * SUB_TN, SUB_TN), SUB_TN)
        part = jnp.dot(h_ref[...], w_ref[:, sub], preferred_element_type=F32)
        base = jnp.where(k == 0, DN_ALPHA * x_ref[:, sub], acc[:, cols])
        acc[:, cols] = base + part

    @pl.when((k == pl.num_programs(1) - 1) & (j == pl.num_programs(2) - 1))
    def _():
        _layer_norm_store(acc, g_ref, b_ref, y_hbm, ybf_hbm, stage_f, stage_b, sems, i * tm)


def _ffn_down(h, w_down_bf, x, g, b, tm):
    m = x.shape[0]
    nk = D_FF // DOWN_TK
    nj = D_MODEL // DOWN_TN
    return pl.pallas_call(
        _ffn_down_kernel,
        grid=(m // tm, nk, nj),
        in_specs=[
            pl.BlockSpec((tm, DOWN_TK), lambda i, k, j: (i, k)),
            pl.BlockSpec((None, DOWN_TK, DOWN_TN), lambda i, k, j: (j, k, 0)),
            pl.BlockSpec((tm, DOWN_TN), lambda i, k, j: (i, jnp.where(k == 0, j, nj - 1))),
            pl.BlockSpec((1, D_MODEL), lambda i, k, j: (0, 0)),
            pl.BlockSpec((1, D_MODEL), lambda i, k, j: (0, 0)),
        ],
        out_specs=[pl.BlockSpec(memory_space=pl.ANY), pl.BlockSpec(memory_space=pl.ANY)],
        out_shape=[jax.ShapeDtypeStruct((m, D_MODEL), F32),
                   jax.ShapeDtypeStruct((m, D_MODEL), BF16)],
        scratch_shapes=[pltpu.VMEM((tm, D_MODEL), F32)] + _ln_scratch(),
        compiler_params=_params("arbitrary", "arbitrary", "arbitrary"),
        name="ffn_down",
    )(h, w_down_bf, x, g, b)


def _rotary_tables(pos):
    inv = ROPE_BASE ** (-jnp.arange(HALF_HEAD, dtype=F32) / HALF_HEAD)
    ang = pos.astype(F32)[:, None] * inv[None, :]
    return jnp.cos(ang), jnp.sin(ang)


def _retention_tables(l):
    log_g = jnp.log1p(-(2.0 ** (-5.0 - jnp.arange(RET_HEADS, dtype=F32))))
    i = jnp.arange(l)
    diff = i[:, None] - i[None, :]
    decay = jnp.where(diff[None] >= 0,
                      jnp.exp(jnp.maximum(diff, 0)[None].astype(F32) * log_g[:, None, None]), 0.0)
    xi = jnp.exp((i + 1)[None].astype(F32) * log_g[:, None])
    zeta = jnp.exp((l - 1 - i)[None].astype(F32) * log_g[:, None])
    g_l = jnp.exp(l * log_g)
    return decay, xi, zeta, g_l


def kernel(x_prompt, x_sample, state_ret, state_pool, state_conv, w_in, w_pool, pool_scale, w_o,
           ln1_g, ln1_b, w_up, conv_w, conv_b, w_down, ln2_g, ln2_b):
    batch, seq, _ = x_prompt.shape
    n_s = x_sample.shape[0]
    assert x_sample.shape[1] == 1 and seq % RET_CHUNK == 0
    mp = batch * seq

    cos_p, sin_p = _rotary_tables(jnp.arange(seq))
    cos_s, sin_s = _rotary_tables(PAST_LEN + jnp.arange(1))
    cos_s = jnp.broadcast_to(cos_s, (n_s, HALF_HEAD))
    sin_s = jnp.broadcast_to(sin_s, (n_s, HALF_HEAD))
    decay, xi, zeta, g_l = _retention_tables(RET_CHUNK)
    xi = xi[:, :, None]
    zeta = zeta[:, :, None]
    gl_b = jnp.broadcast_to(g_l[:, None, None], (RET_HEADS, 1, HEAD_DIM))
    _, xi_s, _, _ = _retention_tables(1)
    gam_s = jnp.broadcast_to(xi_s[:, :, None], (RET_HEADS, 1, HEAD_DIM))

    xp = x_prompt.reshape(mp, D_MODEL)
    xs = x_sample.reshape(n_s, D_MODEL)
    xp_bf = xp.astype(BF16)
    xs_bf = xs.astype(BF16)
    conv_b3 = conv_b[:, None, :]

    pool_p, conv_p = [], []
    new_ret_prompt = new_ret_sample = new_pool_sample = new_conv_sample = None
    for li in range(DEPTH):
        w_pool_bf = w_pool[li].astype(BF16)
        scale = pool_scale[li][None, :]
        g1, b1 = ln1_g[li][None, :], ln1_b[li][None, :]
        g2, b2 = ln2_g[li][None, :], ln2_b[li][None, :]

        proj_p, proj_s, w_o_bf = _proj(xp_bf, xs_bf, w_in, w_o, li, cos_p, sin_p, cos_s, sin_s, seq)

        mix, new_ret_prompt = _ret_prompt(proj_p, decay, xi, zeta, gl_b, li, new_ret_prompt,
                                          batch, seq)
        mix = _pool_prompt(proj_p, w_pool_bf, scale, mix, seq)
        xp, xp_bf = _out_proj(mix, w_o_bf, xp, g1, b1, LN_TM)
        pool_p.append(proj_p.reshape(batch, seq, IN_WIDTH)[:, seq - POOL_BUF:, 4 * RET_WIDTH:])

        mix, new_ret_sample = _ret_sample(proj_s, state_ret, li, gam_s, new_ret_sample)
        mix, new_pool_sample = _pool_sample(proj_s, state_pool, li, w_pool_bf, scale, mix,
                                            new_pool_sample)
        xs, xs_bf = _out_proj(mix, w_o_bf, xs, g1, b1, n_s)

        h_p, tail, h_s, new_conv_sample, w_down_bf = _ffn_up(
            xp_bf, xs_bf, w_up, conv_w, conv_b3, state_conv, new_conv_sample, w_down, li, seq)
        xp, xp_bf = _ffn_down(h_p, w_down_bf, xp, g2, b2, LN_TM)
        xs, xs_bf = _ffn_down(h_s, w_down_bf, xs, g2, b2, n_s)
        conv_p.append(tail[:, SUBLANES - (CONV_WIDTH - 1):, :])

    return (xp.reshape(batch, seq, D_MODEL), xs.reshape(n_s, 1, D_MODEL),
            new_ret_prompt, new_ret_sample, jnp.stack(pool_p), new_pool_sample,
            jnp.stack(conv_p), new_conv_sample)
```

```python
import functools

import jax
import jax.numpy as jnp
from jax import lax
from jax.experimental import pallas as pl
from jax.experimental.pallas import tpu as pltpu

F32 = jnp.float32
BF16 = jnp.bfloat16

D_MODEL = 4096
DEPTH = 2
PAST_LEN = 16384
RET_WIDTH = D_MODEL // 2
RET_HEADS = 8
HEAD_DIM = RET_WIDTH // RET_HEADS
HALF_HEAD = HEAD_DIM // 2
RET_CHUNK = 128
ROPE_BASE = 10000.0
POOL_WIDTH = D_MODEL - RET_WIDTH
POOL_WINDOWS = (2, 4, 8, 16)
POOL_GROUP_DIM = POOL_WIDTH // len(POOL_WINDOWS)
POOL_BUF = max(POOL_WINDOWS) - 1
IN_WIDTH = 4 * RET_WIDTH + POOL_WIDTH
D_FF = ((8 * D_MODEL // 3) + 255) // 256 * 256
CONV_WIDTH = 3
DN_ALPHA = (2.0 * DEPTH) ** 0.25
LN_EPS = 1e-5
K_SCALE = HEAD_DIM ** -0.5

V7X_VMEM_BYTES = 64 * 1024 * 1024
VMEM_LIMIT = V7X_VMEM_BYTES - 8 * 1024 * 1024
SUBLANES = 8
HALO_ROWS = 2 * SUBLANES

PROJ_TN = 512
FFN_TN = 256
LN_TM = 512
OUT_TN = 1024
SUB_TN = 512
LN_ROWS = 64
DOWN_TK = D_FF // 2
DOWN_TN = 1024
POOL_TM = 512
SAMPLE_NB = 16
DOT_ROWS = 1024
CAST_ROWS = 64


def _params(*semantics):
    return pltpu.CompilerParams(dimension_semantics=semantics, vmem_limit_bytes=VMEM_LIMIT)


def _layer_norm(y, g, b):
    mu = jnp.mean(y, axis=-1, keepdims=True)
    d = y - mu
    var = jnp.mean(d * d, axis=-1, keepdims=True)
    return d * lax.rsqrt(var + LN_EPS) * g + b


def _ln_copies(acc_ref, ybf_scr, y_hbm, ybf_hbm, sems, row0, r):
    src = pl.ds(pl.multiple_of(r * LN_ROWS, LN_ROWS), LN_ROWS)
    dst = pl.ds(pl.multiple_of(row0 + r * LN_ROWS, LN_ROWS), LN_ROWS)
    return (pltpu.make_async_copy(acc_ref.at[src, :], y_hbm.at[dst, :], sems.at[0, r]),
            pltpu.make_async_copy(ybf_scr.at[src, :], ybf_hbm.at[dst, :], sems.at[1, r]))


def _layer_norm_start(acc_ref, ybf_scr, g_ref, b_ref, y_hbm, ybf_hbm, sems, row0):
    g = g_ref[...]
    b = b_ref[...]

    def body(r, carry):
        rows = pl.ds(pl.multiple_of(r * LN_ROWS, LN_ROWS), LN_ROWS)
        y = _layer_norm(acc_ref[rows, :], g, b)
        acc_ref[rows, :] = y
        ybf_scr[rows, :] = y.astype(BF16)
        for c in _ln_copies(acc_ref, ybf_scr, y_hbm, ybf_hbm, sems, row0, r):
            c.start()
        return carry

    lax.fori_loop(0, acc_ref.shape[0] // LN_ROWS, body, 0)


def _layer_norm_wait(acc_ref, ybf_scr, y_hbm, ybf_hbm, sems, row0):
    for r in range(acc_ref.shape[0] // LN_ROWS):
        for c in _ln_copies(acc_ref, ybf_scr, y_hbm, ybf_hbm, sems, row0, r):
            c.wait()


def _ln_scratch(tm):
    return [pltpu.VMEM((tm, D_MODEL), F32),
            pltpu.VMEM((tm, D_MODEL), BF16),
            pltpu.SemaphoreType.DMA((2, tm // LN_ROWS))]


def _gelu_exact(x):
    return 0.5 * x * (1.0 + lax.erf(x * (0.5 ** 0.5)))


def _rotate_heads(o_ref, cos_ref, sin_ref, scale):
    cos = cos_ref[...]
    sin = sin_ref[...]
    for h in range(o_ref.shape[1] // HEAD_DIM):
        lo = pl.ds(h * HEAD_DIM, HALF_HEAD)
        hi = pl.ds(h * HEAD_DIM + HALF_HEAD, HALF_HEAD)
        x1 = o_ref[:, lo]
        x2 = o_ref[:, hi]
        o_ref[:, lo] = (x1 * cos - x2 * sin) * scale
        o_ref[:, hi] = (x2 * cos + x1 * sin) * scale


def _proj_kernel(xp_ref, xs_ref, w_ref, cosp_ref, sinp_ref, coss_ref, sins_ref, wo_ref,
                 op_ref, os_ref, wobf_ref, *, rot_tiles, q_tiles, cast_steps):
    i = pl.program_id(0)
    j = pl.program_id(1)
    w = w_ref[...].astype(BF16)
    scale = jnp.where(j >= q_tiles, K_SCALE, 1.0).astype(F32)
    for r in range(xp_ref.shape[0] // DOT_ROWS):
        rows = pl.ds(r * DOT_ROWS, DOT_ROWS)
        op_ref[rows, :] = jnp.dot(xp_ref[rows, :], w, preferred_element_type=F32)

    @pl.when(i * pl.num_programs(1) + j < cast_steps)
    def _():
        wobf_ref[...] = wo_ref[...].astype(BF16)

    @pl.when(j < rot_tiles)
    def _():
        _rotate_heads(op_ref, cosp_ref, sinp_ref, scale)

    @pl.when(i == 0)
    def _():
        os_ref[...] = jnp.dot(xs_ref[...], w, preferred_element_type=F32)

    @pl.when((i == 0) & (j < rot_tiles))
    def _():
        _rotate_heads(os_ref, coss_ref, sins_ref, scale)


def _proj(xp_bf, xs_bf, w_in, w_o, li, cos_p, sin_p, cos_s, sin_s, tm):
    mp = xp_bf.shape[0]
    ns = xs_bf.shape[0]
    nj = IN_WIDTH // PROJ_TN
    t_tiles = cos_p.shape[0] // tm
    cast_steps = D_MODEL // CAST_ROWS
    assert cast_steps <= (mp // tm) * nj
    kern = functools.partial(_proj_kernel, rot_tiles=2 * RET_WIDTH // PROJ_TN,
                             q_tiles=RET_WIDTH // PROJ_TN, cast_steps=cast_steps)
    whole = lambda a: pl.BlockSpec(a.shape, lambda i, j: (0, 0))
    slab = lambda i, j: jnp.minimum(i * nj + j, cast_steps - 1)
    return pl.pallas_call(
        kern,
        grid=(mp // tm, nj),
        in_specs=[
            pl.BlockSpec((tm, D_MODEL), lambda i, j: (i, 0), pipeline_mode=pl.Buffered(1)),
            whole(xs_bf),
            pl.BlockSpec((None, D_MODEL, PROJ_TN), lambda i, j: (li, 0, j)),
            pl.BlockSpec((tm, HALF_HEAD), lambda i, j: (i % t_tiles, 0)),
            pl.BlockSpec((tm, HALF_HEAD), lambda i, j: (i % t_tiles, 0)),
            whole(cos_s), whole(sin_s),
            pl.BlockSpec((None, CAST_ROWS, D_MODEL), lambda i, j: (li, slab(i, j), 0)),
        ],
        out_specs=[
            pl.BlockSpec((tm, PROJ_TN), lambda i, j: (i, j)),
            pl.BlockSpec((ns, PROJ_TN), lambda i, j: (0, jnp.where(i == 0, j, nj - 1))),
            pl.BlockSpec((CAST_ROWS, D_MODEL), lambda i, j: (slab(i, j), 0)),
        ],
        out_shape=[jax.ShapeDtypeStruct((mp, IN_WIDTH), F32),
                   jax.ShapeDtypeStruct((ns, IN_WIDTH), F32),
                   jax.ShapeDtypeStruct((D_MODEL, D_MODEL), BF16)],
        compiler_params=_params("arbitrary", "arbitrary"),
        name="proj",
    )(xp_bf, xs_bf, w_in, cos_p, sin_p, cos_s, sin_s, w_o)


def _group_norm_gate(o, g):
    mu = jnp.mean(o, axis=-1, keepdims=True)
    d = o - mu
    var = jnp.mean(d * d, axis=-1, keepdims=True)
    return d * lax.rsqrt(var + LN_EPS) * (g * jax.nn.sigmoid(g))


def _ret_prompt_kernel(q_ref, k_ref, v_ref, g_ref, decay_ref, xi_ref, zeta_ref, gl_ref, *rest):
    o_ref, snew_ref, s_scr = rest[-3:]
    c = pl.program_id(1)

    @pl.when(c == 0)
    def _():
        s_scr[...] = jnp.zeros_like(s_scr)

    for h in range(RET_HEADS):
        cols = pl.ds(h * HEAD_DIM, HEAD_DIM)
        q = q_ref[:, cols]
        k = k_ref[:, cols]
        vb = v_ref[:, cols].astype(BF16)
        s = s_scr[h]
        scores = lax.dot_general(q.astype(BF16), k.astype(BF16), (((1,), (1,)), ((), ())),
                                 preferred_element_type=F32) * decay_ref[h]
        o = jnp.dot(scores.astype(BF16), vb, preferred_element_type=F32)
        o = o + jnp.dot((q * xi_ref[h]).astype(BF16), s.astype(BF16), preferred_element_type=F32)
        kz_t = (k * zeta_ref[h]).T.astype(BF16)
        s_scr[h] = gl_ref[h] * s + jnp.dot(kz_t, vb, preferred_element_type=F32)
        o_ref[:, cols] = _group_norm_gate(o, g_ref[:, cols]).astype(BF16)

    @pl.when(c == pl.num_programs(1) - 1)
    def _():
        snew_ref[...] = s_scr[...]


def _ret_prompt(proj, decay, xi, zeta, gl, li, new_state_all, batch, seq):
    nc = seq // RET_CHUNK

    def col(cb):
        return pl.BlockSpec((RET_CHUNK, RET_WIDTH), lambda b, c: (b * nc + c, cb))

    def whole(a):
        return pl.BlockSpec(a.shape, lambda b, c: (0,) * a.ndim)

    in_specs = [col(0), col(1), col(2), col(3), whole(decay), whole(xi), whole(zeta), whole(gl)]
    args = [proj, proj, proj, proj, decay, xi, zeta, gl]
    aliases = {}
    if new_state_all is not None:
        in_specs.append(pl.BlockSpec(memory_space=pl.ANY))
        args.append(new_state_all)
        aliases = {len(args) - 1: 1}
    return pl.pallas_call(
        _ret_prompt_kernel,
        grid=(batch, nc),
        in_specs=in_specs,
        out_specs=[
            pl.BlockSpec((RET_CHUNK, RET_WIDTH), lambda b, c: (b * nc + c, 0)),
            pl.BlockSpec((None, None, RET_HEADS, HEAD_DIM, HEAD_DIM), lambda b, c: (li, b, 0, 0, 0)),
        ],
        out_shape=[
            jax.ShapeDtypeStruct((batch * seq, D_MODEL), BF16),
            jax.ShapeDtypeStruct((DEPTH, batch, RET_HEADS, HEAD_DIM, HEAD_DIM), F32),
        ],
        input_output_aliases=aliases,
        scratch_shapes=[pltpu.VMEM((RET_HEADS, HEAD_DIM, HEAD_DIM), F32)],
        compiler_params=_params("parallel", "arbitrary"),
        name="ret_prompt",
    )(*args)


def _ret_sample_kernel(q_ref, k_ref, v_ref, g_ref, s0_ref, gam_ref, *rest):
    o_ref, snew_ref = rest[-2:]
    q = q_ref[...]
    k = k_ref[...]
    v = v_ref[...]
    vb = v.astype(BF16)
    gam = gam_ref[0]
    qg = (q * gam).astype(BF16)
    rows = lax.broadcasted_iota(jnp.int32, (SAMPLE_NB, 1), 0)
    o_state = jnp.zeros((SAMPLE_NB, HEAD_DIM), F32)
    for n in range(SAMPLE_NB):
        s = s0_ref[n, 0]
        r = jnp.dot(qg, s.astype(BF16), preferred_element_type=F32)
        o_state = jnp.where(rows == n, r, o_state)
        k_n_t = jnp.where(rows == n, k, 0.0).T.astype(BF16)
        snew_ref[n, 0] = gam * s + jnp.dot(k_n_t, vb, preferred_element_type=F32)
    qk = jnp.sum(q * k, axis=-1, keepdims=True)
    o = qk * v + o_state
    o_ref[...] = _group_norm_gate(o, g_ref[...]).astype(BF16)


def _ret_sample(proj, state_all, li, gam, new_state_all):
    n = proj.shape[0]

    def col(cb):
        return pl.BlockSpec((SAMPLE_NB, HEAD_DIM), lambda i, h: (i, cb * RET_HEADS + h))

    state_spec = pl.BlockSpec((None, SAMPLE_NB, 1, HEAD_DIM, HEAD_DIM),
                              lambda i, h: (li, i, h, 0, 0))
    in_specs = [col(0), col(1), col(2), col(3), state_spec,
                pl.BlockSpec((1, 1, HEAD_DIM), lambda i, h: (h, 0, 0))]
    args = [proj, proj, proj, proj, state_all, gam]
    aliases = {}
    if new_state_all is not None:
        in_specs.append(pl.BlockSpec(memory_space=pl.ANY))
        args.append(new_state_all)
        aliases = {len(args) - 1: 1}
    return pl.pallas_call(
        _ret_sample_kernel,
        grid=(n // SAMPLE_NB, RET_HEADS),
        in_specs=in_specs,
        out_specs=[pl.BlockSpec((SAMPLE_NB, HEAD_DIM), lambda i, h: (i, h)), state_spec],
        out_shape=[jax.ShapeDtypeStruct((n, D_MODEL), BF16),
                   jax.ShapeDtypeStruct(state_all.shape, F32)],
        input_output_aliases=aliases,
        compiler_params=_params("parallel", "arbitrary"),
        name="ret_sample",
    )(*args)


def _pool_prompt_kernel(u_ref, halo_ref, wp_ref, sc_ref, mix_hbm, o_ref, ext_scr, *,
                        tiles_per_seq):
    del mix_hbm
    i = pl.program_id(0)
    tm = u_ref.shape[0]
    t0 = (i % tiles_per_seq) * tm
    ext_scr[0:HALO_ROWS, :] = jnp.where(t0 == 0, 0.0, halo_ref[...])
    ext_scr[HALO_ROWS:HALO_ROWS + tm, :] = u_ref[...]
    pos = t0 + lax.broadcasted_iota(jnp.int32, (tm, 1), 0)
    for gi, w in enumerate(POOL_WINDOWS):
        cols = pl.ds(gi * POOL_GROUP_DIM, POOL_GROUP_DIM)
        u = u_ref[:, cols]
        s = u
        for back in range(1, w):
            s = s + ext_scr[pl.ds(HALO_ROWS - back, tm), cols]
        cnt = jnp.minimum(w, pos + 1).astype(F32)
        pooled = s / cnt - u
        mixed = jnp.dot(pooled.astype(BF16), wp_ref[gi], preferred_element_type=F32)
        o_ref[:, cols] = (mixed * sc_ref[:, cols]).astype(BF16)


def _pool_prompt(proj, w_pool_bf, scale, mix, seq):
    m = proj.shape[0]
    tiles_per_seq = seq // POOL_TM
    u_col = 4 * RET_WIDTH // POOL_WIDTH
    halo_per_tile = POOL_TM // HALO_ROWS
    kern = functools.partial(_pool_prompt_kernel, tiles_per_seq=tiles_per_seq)
    return pl.pallas_call(
        kern,
        grid=(m // POOL_TM,),
        in_specs=[
            pl.BlockSpec((POOL_TM, POOL_WIDTH), lambda i: (i, u_col)),
            pl.BlockSpec((HALO_ROWS, POOL_WIDTH),
                         lambda i: (jnp.maximum(i * halo_per_tile - 1, 0), u_col)),
            pl.BlockSpec(w_pool_bf.shape, lambda i: (0, 0, 0)),
            pl.BlockSpec((1, POOL_WIDTH), lambda i: (0, 0)),
            pl.BlockSpec(memory_space=pl.ANY),
        ],
        out_specs=pl.BlockSpec((POOL_TM, POOL_WIDTH), lambda i: (i, 1)),
        out_shape=jax.ShapeDtypeStruct(mix.shape, BF16),
        input_output_aliases={4: 0},
        scratch_shapes=[pltpu.VMEM((HALO_ROWS + POOL_TM, POOL_WIDTH), F32)],
        compiler_params=_params("parallel"),
        name="pool_prompt",
    )(proj, proj, w_pool_bf, scale, mix)


def _pool_sample_kernel(u_ref, buf_ref, wp_ref, sc_ref, *rest):
    o_ref, nbuf_ref = rest[-2:]
    nbuf_ref[:, 0:POOL_BUF - 1, :] = buf_ref[:, 1:POOL_BUF, :]
    nbuf_ref[:, POOL_BUF - 1, :] = u_ref[...]
    row = lax.broadcasted_iota(jnp.int32, (1, POOL_BUF, 1), 1)
    for gi, w in enumerate(POOL_WINDOWS):
        cols = pl.ds(gi * POOL_GROUP_DIM, POOL_GROUP_DIM)
        u = u_ref[:, cols]
        past = jnp.where(row >= POOL_BUF - (w - 1), buf_ref[:, :, cols], 0.0)
        s = u + jnp.sum(past, axis=1)
        pooled = s / float(w) - u
        mixed = jnp.dot(pooled.astype(BF16), wp_ref[gi], preferred_element_type=F32)
        o_ref[:, cols] = (mixed * sc_ref[:, cols]).astype(BF16)


def _pool_sample(proj, buf_all, li, w_pool_bf, scale, mix, new_buf_all):
    n = proj.shape[0]
    u_col = 4 * RET_WIDTH // POOL_WIDTH
    buf_spec = pl.BlockSpec((None, SAMPLE_NB, POOL_BUF, POOL_WIDTH), lambda i: (li, i, 0, 0))
    in_specs = [
        pl.BlockSpec((SAMPLE_NB, POOL_WIDTH), lambda i: (i, u_col)),
        buf_spec,
        pl.BlockSpec(w_pool_bf.shape, lambda i: (0, 0, 0)),
        pl.BlockSpec((1, POOL_WIDTH), lambda i: (0, 0)),
        pl.BlockSpec(memory_space=pl.ANY),
    ]
    args = [proj, buf_all, w_pool_bf, scale, mix]
    aliases = {4: 0}
    if new_buf_all is not None:
        in_specs.append(pl.BlockSpec(memory_space=pl.ANY))
        args.append(new_buf_all)
        aliases[5] = 1
    return pl.pallas_call(
        _pool_sample_kernel,
        grid=(n // SAMPLE_NB,),
        in_specs=in_specs,
        out_specs=[pl.BlockSpec((SAMPLE_NB, POOL_WIDTH), lambda i: (i, 1)), buf_spec],
        out_shape=[jax.ShapeDtypeStruct(mix.shape, BF16),
                   jax.ShapeDtypeStruct(buf_all.shape, F32)],
        input_output_aliases=aliases,
        compiler_params=_params("parallel"),
        name="pool_sample",
    )(*args)


def _out_proj_kernel(mix_ref, w_ref, x_ref, g_ref, b_ref, y_hbm, ybf_hbm, acc, ybf_scr, sems):
    i = pl.program_id(0)
    j = pl.program_id(1)
    tm = mix_ref.shape[0]
    tn = w_ref.shape[1]
    last = j == pl.num_programs(1) - 1

    @pl.when((j == 0) & (i > 0))
    def _():
        _layer_norm_wait(acc, ybf_scr, y_hbm, ybf_hbm, sems, (i - 1) * tm)

    for c in range(tn // SUB_TN):
        sub = pl.ds(c * SUB_TN, SUB_TN)
        cols = pl.ds(pl.multiple_of(j * tn + c * SUB_TN, SUB_TN), SUB_TN)
        mix = jnp.dot(mix_ref[...], w_ref[:, sub], preferred_element_type=F32)
        acc[:, cols] = DN_ALPHA * x_ref[:, sub] + mix

    @pl.when(last)
    def _():
        _layer_norm_start(acc, ybf_scr, g_ref, b_ref, y_hbm, ybf_hbm, sems, i * tm)

    @pl.when(last & (i == pl.num_programs(0) - 1))
    def _():
        _layer_norm_wait(acc, ybf_scr, y_hbm, ybf_hbm, sems, i * tm)


def _out_proj(mix, w_o_bf, x, g, b, tm):
    m = x.shape[0]
    return pl.pallas_call(
        _out_proj_kernel,
        grid=(m // tm, D_MODEL // OUT_TN),
        in_specs=[
            pl.BlockSpec((tm, D_MODEL), lambda i, j: (i, 0)),
            pl.BlockSpec((D_MODEL, OUT_TN), lambda i, j: (0, j)),
            pl.BlockSpec((tm, OUT_TN), lambda i, j: (i, j)),
            pl.BlockSpec((1, D_MODEL), lambda i, j: (0, 0)),
            pl.BlockSpec((1, D_MODEL), lambda i, j: (0, 0)),
        ],
        out_specs=[pl.BlockSpec(memory_space=pl.ANY), pl.BlockSpec(memory_space=pl.ANY)],
        out_shape=[jax.ShapeDtypeStruct((m, D_MODEL), F32),
                   jax.ShapeDtypeStruct((m, D_MODEL), BF16)],
        scratch_shapes=_ln_scratch(tm),
        compiler_params=_params("arbitrary", "arbitrary"),
        name="out_proj",
    )(mix, w_o_bf, x, g, b)


def _shift_rows(x, carry_row, first8):
    rolled = pltpu.roll(x, 1, axis=0)
    fill = 0.0 if carry_row is None else carry_row
    head = jnp.where(first8 == 0, fill, rolled[0:SUBLANES, :])
    return jnp.concatenate([head, rolled[SUBLANES:, :]], axis=0)


def _ffn_up_kernel(xp_ref, xs_ref, wa_ref, wb_ref, cw_ref, cb_ref, sc_ref, wd_ref, *rest):
    hp_ref, tail_ref, hs_ref, nsc_ref, wdbf_ref, w_scr = rest[-6:]
    i = pl.program_id(0)
    tm = xp_ref.shape[0]
    tn = wa_ref.shape[1]
    w_scr[:, 0:tn] = wa_ref[...].astype(BF16)
    w_scr[:, tn:2 * tn] = wb_ref[...].astype(BF16)
    w = w_scr[...]
    cw0 = cw_ref[0:1, :]
    cw1 = cw_ref[1:2, :]
    cw2 = cw_ref[2:3, :]
    cb = cb_ref[...]

    wdbf_ref[...] = wd_ref[...].astype(BF16)

    first8 = lax.broadcasted_iota(jnp.int32, (SUBLANES, tn), 0)
    p_last = None
    q_last = None
    for r in range(tm // DOT_ROWS):
        r0 = r * DOT_ROWS
        ab = jnp.dot(xp_ref[pl.ds(r0, DOT_ROWS), :], w, preferred_element_type=F32)
        a = ab[:, 0:tn]
        p = a * cw0
        q = a * cw1 + _shift_rows(p, p_last, first8)
        conv = (cb + a * cw2) + _shift_rows(q, q_last, first8)
        p_last = p[DOT_ROWS - 1:DOT_ROWS, :]
        q_last = q[DOT_ROWS - 1:DOT_ROWS, :]
        hp_ref[pl.ds(r0, DOT_ROWS), :] = (_gelu_exact(conv) * ab[:, tn:2 * tn]).astype(BF16)
    tail_ref[0] = a[DOT_ROWS - SUBLANES:DOT_ROWS, :]

    @pl.when(i == 0)
    def _():
        ab_s = jnp.dot(xs_ref[...], w_scr[...], preferred_element_type=F32)
        a_s = ab_s[:, 0:tn]
        b_s = ab_s[:, tn:2 * tn]
        s1 = sc_ref[:, 1, :]
        conv_s = cb + sc_ref[:, 0, :] * cw0
        conv_s = conv_s + s1 * cw1
        conv_s = conv_s + a_s * cw2
        hs_ref[...] = (_gelu_exact(conv_s) * b_s).astype(BF16)
        nsc_ref[:, 0, :] = s1
        nsc_ref[:, 1, :] = a_s


def _ffn_up(xp_bf, xs_bf, w_up, conv_w, conv_b, state_all, new_state_all, w_down, li, seq):
    mp = xp_bf.shape[0]
    ns = xs_bf.shape[0]
    nj = D_FF // FFN_TN
    tm = seq
    assert (mp // tm) * nj * CAST_ROWS == D_FF
    first = lambda i, j: jnp.where(i == 0, j, nj - 1)
    state_spec = pl.BlockSpec((None, ns, CONV_WIDTH - 1, FFN_TN),
                              lambda i, j: (li, 0, 0, first(i, j)))
    in_specs = [
        pl.BlockSpec((tm, D_MODEL), lambda i, j: (i, 0), pipeline_mode=pl.Buffered(1)),
        pl.BlockSpec((ns, D_MODEL), lambda i, j: (0, 0)),
        pl.BlockSpec((None, D_MODEL, FFN_TN), lambda i, j: (li, 0, j)),
        pl.BlockSpec((None, D_MODEL, FFN_TN), lambda i, j: (li, 0, nj + j)),
        pl.BlockSpec((None, CONV_WIDTH, FFN_TN), lambda i, j: (li, 0, j)),
        pl.BlockSpec((None, 1, FFN_TN), lambda i, j: (li, 0, j)),
        state_spec,
        pl.BlockSpec((None, CAST_ROWS, D_MODEL), lambda i, j: (li, i * nj + j, 0)),
    ]
    args = [xp_bf, xs_bf, w_up, w_up, conv_w, conv_b, state_all, w_down]
    aliases = {}
    if new_state_all is not None:
        in_specs.append(pl.BlockSpec(memory_space=pl.ANY))
        args.append(new_state_all)
        aliases = {len(args) - 1: 3}
    return pl.pallas_call(
        _ffn_up_kernel,
        grid=(mp // tm, nj),
        in_specs=in_specs,
        out_specs=[
            pl.BlockSpec((tm, FFN_TN), lambda i, j: (i, j)),
            pl.BlockSpec((1, SUBLANES, FFN_TN), lambda i, j: (i, 0, j)),
            pl.BlockSpec((ns, FFN_TN), lambda i, j: (0, first(i, j))),
            state_spec,
            pl.BlockSpec((CAST_ROWS, D_MODEL), lambda i, j: (i * nj + j, 0)),
        ],
        out_shape=[jax.ShapeDtypeStruct((mp, D_FF), BF16),
                   jax.ShapeDtypeStruct((mp // tm, SUBLANES, D_FF), F32),
                   jax.ShapeDtypeStruct((ns, D_FF), BF16),
                   jax.ShapeDtypeStruct(state_all.shape, F32),
                   jax.ShapeDtypeStruct((D_FF, D_MODEL), BF16)],
        input_output_aliases=aliases,
        scratch_shapes=[pltpu.VMEM((D_MODEL, 2 * FFN_TN), BF16)],
        compiler_params=_params("arbitrary", "arbitrary"),
        name="ffn_up",
    )(*args)


def _ffn_down_kernel(h_ref, w_ref, x_ref, g_ref, b_ref, y_hbm, ybf_hbm, acc, ybf_scr, sems):
    i = pl.program_id(0)
    k = pl.program_id(1)
    j = pl.program_id(2)
    tm = h_ref.shape[0]
    tn = w_ref.shape[1]
    first = (k == 0) & (j == 0)
    last = (k == pl.num_programs(1) - 1) & (j == pl.num_programs(2) - 1)

    @pl.when(first & (i == 0))
    def _():
        acc[...] = jnp.zeros_like(acc)

    @pl.when(first & (i > 0))
    def _():
        _layer_norm_wait(acc, ybf_scr, y_hbm, ybf_hbm, sems, (i - 1) * tm)

    for c in range(tn // SUB_TN):
        sub = pl.ds(c * SUB_TN, SUB_TN)
        cols = pl.ds(pl.multiple_of(j * tn + c * SUB_TN, SUB_TN), SUB_TN)
        part = jnp.dot(h_ref[...], w_ref[:, sub], preferred_element_type=F32)
        base = jnp.where(k == 0, DN_ALPHA * x_ref[:, sub], acc[:, cols])
        acc[:, cols] = base + part

    @pl.when(last)
    def _():
        _layer_norm_start(acc, ybf_scr, g_ref, b_ref, y_hbm, ybf_hbm, sems, i * tm)

    @pl.when(last & (i == pl.num_programs(0) - 1))
    def _():
        _layer_norm_wait(acc, ybf_scr, y_hbm, ybf_hbm, sems, i * tm)


def _ffn_down(h, w_down_bf, x, g, b, tm):
    m = x.shape[0]
    nk = D_FF // DOWN_TK
    nj = D_MODEL // DOWN_TN
    return pl.pallas_call(
        _ffn_down_kernel,
        grid=(m // tm, nk, nj),
        in_specs=[
            pl.BlockSpec((tm, DOWN_TK), lambda i, k, j: (i, k)),
            pl.BlockSpec((DOWN_TK, DOWN_TN), lambda i, k, j: (k, j)),
            pl.BlockSpec((tm, DOWN_TN), lambda i, k, j: (i, jnp.where(k == 0, j, nj - 1))),
            pl.BlockSpec((1, D_MODEL), lambda i, k, j: (0, 0)),
            pl.BlockSpec((1, D_MODEL), lambda i, k, j: (0, 0)),
        ],
        out_specs=[pl.BlockSpec(memory_space=pl.ANY), pl.BlockSpec(memory_space=pl.ANY)],
        out_shape=[jax.ShapeDtypeStruct((m, D_MODEL), F32),
                   jax.ShapeDtypeStruct((m, D_MODEL), BF16)],
        scratch_shapes=_ln_scratch(tm),
        compiler_params=_params("arbitrary", "arbitrary", "arbitrary"),
        name="ffn_down",
    )(h, w_down_bf, x, g, b)


def _rotary_tables(pos):
    inv = ROPE_BASE ** (-jnp.arange(HALF_HEAD, dtype=F32) / HALF_HEAD)
    ang = pos.astype(F32)[:, None] * inv[None, :]
    return jnp.cos(ang), jnp.sin(ang)


def _retention_tables(l):
    log_g = jnp.log1p(-(2.0 ** (-5.0 - jnp.arange(RET_HEADS, dtype=F32))))
    i = jnp.arange(l)
    diff = i[:, None] - i[None, :]
    decay = jnp.where(diff[None] >= 0,
                      jnp.exp(jnp.maximum(diff, 0)[None].astype(F32) * log_g[:, None, None]), 0.0)
    xi = jnp.exp((i + 1)[None].astype(F32) * log_g[:, None])
    zeta = jnp.exp((l - 1 - i)[None].astype(F32) * log_g[:, None])
    g_l = jnp.exp(l * log_g)
    return decay, xi, zeta, g_l


def kernel(x_prompt, x_sample, state_ret, state_pool, state_conv, w_in, w_pool, pool_scale, w_o,
           ln1_g, ln1_b, w_up, conv_w, conv_b, w_down, ln2_g, ln2_b):
    batch, seq, _ = x_prompt.shape
    n_s = x_sample.shape[0]
    assert x_sample.shape[1] == 1 and seq % RET_CHUNK == 0
    mp = batch * seq

    cos_p, sin_p = _rotary_tables(jnp.arange(seq))
    cos_s, sin_s = _rotary_tables(PAST_LEN + jnp.arange(1))
    cos_s = jnp.broadcast_to(cos_s, (n_s, HALF_HEAD))
    sin_s = jnp.broadcast_to(sin_s, (n_s, HALF_HEAD))
    decay, xi, zeta, g_l = _retention_tables(RET_CHUNK)
    xi = xi[:, :, None]
    zeta = zeta[:, :, None]
    gl_b = jnp.broadcast_to(g_l[:, None, None], (RET_HEADS, 1, HEAD_DIM))
    _, xi_s, _, _ = _retention_tables(1)
    gam_s = jnp.broadcast_to(xi_s[:, :, None], (RET_HEADS, 1, HEAD_DIM))

    xp = x_prompt.reshape(mp, D_MODEL)
    xs = x_sample.reshape(n_s, D_MODEL)
    xp_bf = xp.astype(BF16)
    xs_bf = xs.astype(BF16)
    conv_b3 = conv_b[:, None, :]

    pool_p, conv_p = [], []
    new_ret_prompt = new_ret_sample = new_pool_sample = new_conv_sample = None
    for li in range(DEPTH):
        w_pool_bf = w_pool[li].astype(BF16)
        scale = pool_scale[li][None, :]
        g1, b1 = ln1_g[li][None, :], ln1_b[li][None, :]
        g2, b2 = ln2_g[li][None, :], ln2_b[li][None, :]

        proj_p, proj_s, w_o_bf = _proj(xp_bf, xs_bf, w_in, w_o, li, cos_p, sin_p, cos_s, sin_s, seq)

        mix, new_ret_prompt = _ret_prompt(proj_p, decay, xi, zeta, gl_b, li, new_ret_prompt,
                                          batch, seq)
        mix = _pool_prompt(proj_p, w_pool_bf, scale, mix, seq)
        xp, xp_bf = _out_proj(mix, w_o_bf, xp, g1, b1, LN_TM)
        pool_p.append(proj_p.reshape(batch, seq, IN_WIDTH)[:, seq - POOL_BUF:, 4 * RET_WIDTH:])

        mix, new_ret_sample = _ret_sample(proj_s, state_ret, li, gam_s, new_ret_sample)
        mix, new_pool_sample = _pool_sample(proj_s, state_pool, li, w_pool_bf, scale, mix,
                                            new_pool_sample)
        xs, xs_bf = _out_proj(mix, w_o_bf, xs, g1, b1, n_s)

        h_p, tail, h_s, new_conv_sample, w_down_bf = _ffn_up(
            xp_bf, xs_bf, w_up, conv_w, conv_b3, state_conv, new_conv_sample, w_down, li, seq)
        xp, xp_bf = _ffn_down(h_p, w_down_bf, xp, g2, b2, LN_TM)
        xs, xs_bf = _ffn_down(h_s, w_down_bf, xs, g2, b2, n_s)
        conv_p.append(tail[:, SUBLANES - (CONV_WIDTH - 1):, :])

    return (xp.reshape(batch, seq, D_MODEL), xs.reshape(n_s, 1, D_MODEL),
            new_ret_prompt, new_ret_sample, jnp.stack(pool_p), new_pool_sample,
            jnp.stack(conv_p), new_conv_sample)
```

```python
import functools

import jax
import jax.numpy as jnp
from jax import lax
from jax.experimental import pallas as pl
from jax.experimental.pallas import tpu as pltpu

F32 = jnp.float32
BF16 = jnp.bfloat16

D_MODEL = 4096
DEPTH = 2
PAST_LEN = 16384
RET_WIDTH = D_MODEL // 2
RET_HEADS = 8
HEAD_DIM = RET_WIDTH // RET_HEADS
HALF_HEAD = HEAD_DIM // 2
RET_CHUNK = 128
ROPE_BASE = 10000.0
POOL_WIDTH = D_MODEL - RET_WIDTH
POOL_WINDOWS = (2, 4, 8, 16)
POOL_GROUP_DIM = POOL_WIDTH // len(POOL_WINDOWS)
POOL_BUF = max(POOL_WINDOWS) - 1
IN_WIDTH = 4 * RET_WIDTH + POOL_WIDTH
D_FF = ((8 * D_MODEL // 3) + 255) // 256 * 256
CONV_WIDTH = 3
DN_ALPHA = (2.0 * DEPTH) ** 0.25
LN_EPS = 1e-5
K_SCALE = HEAD_DIM ** -0.5

V7X_VMEM_BYTES = 64 * 1024 * 1024
VMEM_LIMIT = V7X_VMEM_BYTES - 3 * 1024 * 1024
SUBLANES = 8
HALO_ROWS = 2 * SUBLANES

PROJ_TN = 512
FFN_TN = 256
LN_TM = 512
OUT_TN = 1024
SUB_TN = 512
LN_ROWS = 64
DOWN_TK = D_FF // 2
DOWN_TN = 1024
POOL_TM = 512
SAMPLE_NB = 16
DOT_ROWS = 1024
CAST_ROWS = 64


def _params(*semantics):
    return pltpu.CompilerParams(dimension_semantics=semantics, vmem_limit_bytes=VMEM_LIMIT)


def _layer_norm(y, g, b):
    mu = jnp.mean(y, axis=-1, keepdims=True)
    d = y - mu
    var = jnp.mean(d * d, axis=-1, keepdims=True)
    return d * lax.rsqrt(var + LN_EPS) * g + b


def _ln_copies(acc_ref, ybf_scr, y_hbm, ybf_hbm, sems, row0, r):
    src = pl.ds(pl.multiple_of(r * LN_ROWS, LN_ROWS), LN_ROWS)
    dst = pl.ds(pl.multiple_of(row0 + r * LN_ROWS, LN_ROWS), LN_ROWS)
    return (pltpu.make_async_copy(acc_ref.at[src, :], y_hbm.at[dst, :], sems.at[0, r]),
            pltpu.make_async_copy(ybf_scr.at[src, :], ybf_hbm.at[dst, :], sems.at[1, r]))


def _layer_norm_slab(acc_ref, ybf_scr, g, b, r):
    rows = pl.ds(pl.multiple_of(r * LN_ROWS, LN_ROWS), LN_ROWS)
    y = _layer_norm(acc_ref[rows, :], g, b)
    acc_ref[rows, :] = y
    ybf_scr[rows, :] = y.astype(BF16)


def _layer_norm_start(acc_ref, ybf_scr, g_ref, b_ref, y_hbm, ybf_hbm, sems, row0):
    g = g_ref[...]
    b = b_ref[...]

    def body(r, carry):
        _layer_norm_slab(acc_ref, ybf_scr, g, b, r)
        for c in _ln_copies(acc_ref, ybf_scr, y_hbm, ybf_hbm, sems, row0, r):
            c.start()
        return carry

    lax.fori_loop(0, acc_ref.shape[0] // LN_ROWS, body, 0)


def _layer_norm_wait(acc_ref, ybf_scr, y_hbm, ybf_hbm, sems, row0):
    for r in range(acc_ref.shape[0] // LN_ROWS):
        for c in _ln_copies(acc_ref, ybf_scr, y_hbm, ybf_hbm, sems, row0, r):
            c.wait()


def _ln_scratch(tm):
    return [pltpu.VMEM((tm, D_MODEL), F32),
            pltpu.VMEM((tm, D_MODEL), F32),
            pltpu.VMEM((tm, D_MODEL), BF16),
            pltpu.SemaphoreType.DMA((2, tm // LN_ROWS))]


def _gelu_exact(x):
    return 0.5 * x * (1.0 + lax.erf(x * (0.5 ** 0.5)))


def _rotate_heads(o_ref, cos_ref, sin_ref, scale):
    cos = cos_ref[...]
    sin = sin_ref[...]
    for h in range(o_ref.shape[1] // HEAD_DIM):
        lo = pl.ds(h * HEAD_DIM, HALF_HEAD)
        hi = pl.ds(h * HEAD_DIM + HALF_HEAD, HALF_HEAD)
        x1 = o_ref[:, lo]
        x2 = o_ref[:, hi]
        o_ref[:, lo] = (x1 * cos - x2 * sin) * scale
        o_ref[:, hi] = (x2 * cos + x1 * sin) * scale


def _proj_kernel(xp_ref, xs_ref, w_ref, cosp_ref, sinp_ref, coss_ref, sins_ref, wo_ref,
                 op_ref, os_ref, wobf_ref, *, rot_tiles, q_tiles, cast_steps):
    i = pl.program_id(0)
    j = pl.program_id(1)
    w = w_ref[...].astype(BF16)
    scale = jnp.where(j >= q_tiles, K_SCALE, 1.0).astype(F32)
    for r in range(xp_ref.shape[0] // DOT_ROWS):
        rows = pl.ds(r * DOT_ROWS, DOT_ROWS)
        op_ref[rows, :] = jnp.dot(xp_ref[rows, :], w, preferred_element_type=F32)

    @pl.when(i * pl.num_programs(1) + j < cast_steps)
    def _():
        wobf_ref[...] = wo_ref[...].astype(BF16)

    @pl.when(j < rot_tiles)
    def _():
        _rotate_heads(op_ref, cosp_ref, sinp_ref, scale)

    @pl.when(i == 0)
    def _():
        os_ref[...] = jnp.dot(xs_ref[...], w, preferred_element_type=F32)

    @pl.when((i == 0) & (j < rot_tiles))
    def _():
        _rotate_heads(os_ref, coss_ref, sins_ref, scale)


def _proj(xp_bf, xs_bf, w_in, w_o, li, cos_p, sin_p, cos_s, sin_s, tm):
    mp = xp_bf.shape[0]
    ns = xs_bf.shape[0]
    nj = IN_WIDTH // PROJ_TN
    t_tiles = cos_p.shape[0] // tm
    cast_steps = D_MODEL // CAST_ROWS
    assert cast_steps <= (mp // tm) * nj
    kern = functools.partial(_proj_kernel, rot_tiles=2 * RET_WIDTH // PROJ_TN,
                             q_tiles=RET_WIDTH // PROJ_TN, cast_steps=cast_steps)
    whole = lambda a: pl.BlockSpec(a.shape, lambda i, j: (0, 0))
    slab = lambda i, j: jnp.minimum(i * nj + j, cast_steps - 1)
    return pl.pallas_call(
        kern,
        grid=(mp // tm, nj),
        in_specs=[
            pl.BlockSpec((tm, D_MODEL), lambda i, j: (i, 0), pipeline_mode=pl.Buffered(1)),
            whole(xs_bf),
            pl.BlockSpec((None, D_MODEL, PROJ_TN), lambda i, j: (li, 0, j)),
            pl.BlockSpec((tm, HALF_HEAD), lambda i, j: (i % t_tiles, 0)),
            pl.BlockSpec((tm, HALF_HEAD), lambda i, j: (i % t_tiles, 0)),
            whole(cos_s), whole(sin_s),
            pl.BlockSpec((None, CAST_ROWS, D_MODEL), lambda i, j: (li, slab(i, j), 0)),
        ],
        out_specs=[
            pl.BlockSpec((tm, PROJ_TN), lambda i, j: (i, j)),
            pl.BlockSpec((ns, PROJ_TN), lambda i, j: (0, jnp.where(i == 0, j, nj - 1))),
            pl.BlockSpec((CAST_ROWS, D_MODEL), lambda i, j: (slab(i, j), 0)),
        ],
        out_shape=[jax.ShapeDtypeStruct((mp, IN_WIDTH), F32),
                   jax.ShapeDtypeStruct((ns, IN_WIDTH), F32),
                   jax.ShapeDtypeStruct((D_MODEL, D_MODEL), BF16)],
        compiler_params=_params("arbitrary", "arbitrary"),
        name="proj",
    )(xp_bf, xs_bf, w_in, cos_p, sin_p, cos_s, sin_s, w_o)


def _group_norm_gate(o, g):
    mu = jnp.mean(o, axis=-1, keepdims=True)
    d = o - mu
    var = jnp.mean(d * d, axis=-1, keepdims=True)
    return d * lax.rsqrt(var + LN_EPS) * (g * jax.nn.sigmoid(g))


def _ret_prompt_kernel(q_ref, k_ref, v_ref, g_ref, decay_ref, xi_ref, zeta_ref, gl_ref, *rest):
    o_ref, snew_ref, s_scr = rest[-3:]
    c = pl.program_id(1)

    @pl.when(c == 0)
    def _():
        s_scr[...] = jnp.zeros_like(s_scr)

    for h in range(RET_HEADS):
        cols = pl.ds(h * HEAD_DIM, HEAD_DIM)
        q = q_ref[:, cols]
        k = k_ref[:, cols]
        vb = v_ref[:, cols].astype(BF16)
        s = s_scr[h]
        scores = lax.dot_general(q.astype(BF16), k.astype(BF16), (((1,), (1,)), ((), ())),
                                 preferred_element_type=F32) * decay_ref[h]
        o = jnp.dot(scores.astype(BF16), vb, preferred_element_type=F32)
        o = o + jnp.dot((q * xi_ref[h]).astype(BF16), s.astype(BF16), preferred_element_type=F32)
        kz_t = (k * zeta_ref[h]).T.astype(BF16)
        s_scr[h] = gl_ref[h] * s + jnp.dot(kz_t, vb, preferred_element_type=F32)
        o_ref[:, cols] = _group_norm_gate(o, g_ref[:, cols]).astype(BF16)

    @pl.when(c == pl.num_programs(1) - 1)
    def _():
        snew_ref[...] = s_scr[...]


def _ret_prompt(proj, decay, xi, zeta, gl, li, new_state_all, batch, seq):
    nc = seq // RET_CHUNK

    def col(cb):
        return pl.BlockSpec((RET_CHUNK, RET_WIDTH), lambda b, c: (b * nc + c, cb))

    def whole(a):
        return pl.BlockSpec(a.shape, lambda b, c: (0,) * a.ndim)

    in_specs = [col(0), col(1), col(2), col(3), whole(decay), whole(xi), whole(zeta), whole(gl)]
    args = [proj, proj, proj, proj, decay, xi, zeta, gl]
    aliases = {}
    if new_state_all is not None:
        in_specs.append(pl.BlockSpec(memory_space=pl.ANY))
        args.append(new_state_all)
        aliases = {len(args) - 1: 1}
    return pl.pallas_call(
        _ret_prompt_kernel,
        grid=(batch, nc),
        in_specs=in_specs,
        out_specs=[
            pl.BlockSpec((RET_CHUNK, RET_WIDTH), lambda b, c: (b * nc + c, 0)),
            pl.BlockSpec((None, None, RET_HEADS, HEAD_DIM, HEAD_DIM), lambda b, c: (li, b, 0, 0, 0)),
        ],
        out_shape=[
            jax.ShapeDtypeStruct((batch * seq, D_MODEL), BF16),
            jax.ShapeDtypeStruct((DEPTH, batch, RET_HEADS, HEAD_DIM, HEAD_DIM), F32),
        ],
        input_output_aliases=aliases,
        scratch_shapes=[pltpu.VMEM((RET_HEADS, HEAD_DIM, HEAD_DIM), F32)],
        compiler_params=_params("parallel", "arbitrary"),
        name="ret_prompt",
    )(*args)


def _ret_sample_kernel(q_ref, k_ref, v_ref, g_ref, s0_ref, gam_ref, *rest):
    o_ref, snew_ref = rest[-2:]
    q = q_ref[...]
    k = k_ref[...]
    v = v_ref[...]
    vb = v.astype(BF16)
    gam = gam_ref[0]
    qg = (q * gam).astype(BF16)
    rows = lax.broadcasted_iota(jnp.int32, (SAMPLE_NB, 1), 0)
    o_state = jnp.zeros((SAMPLE_NB, HEAD_DIM), F32)
    for n in range(SAMPLE_NB):
        s = s0_ref[n, 0]
        r = jnp.dot(qg, s.astype(BF16), preferred_element_type=F32)
        o_state = jnp.where(rows == n, r, o_state)
        k_n_t = jnp.where(rows == n, k, 0.0).T.astype(BF16)
        snew_ref[n, 0] = gam * s + jnp.dot(k_n_t, vb, preferred_element_type=F32)
    qk = jnp.sum(q * k, axis=-1, keepdims=True)
    o = qk * v + o_state
    o_ref[...] = _group_norm_gate(o, g_ref[...]).astype(BF16)


def _ret_sample(proj, state_all, li, gam, new_state_all):
    n = proj.shape[0]

    def col(cb):
        return pl.BlockSpec((SAMPLE_NB, HEAD_DIM), lambda i, h: (i, cb * RET_HEADS + h))

    state_spec = pl.BlockSpec((None, SAMPLE_NB, 1, HEAD_DIM, HEAD_DIM),
                              lambda i, h: (li, i, h, 0, 0))
    in_specs = [col(0), col(1), col(2), col(3), state_spec,
                pl.BlockSpec((1, 1, HEAD_DIM), lambda i, h: (h, 0, 0))]
    args = [proj, proj, proj, proj, state_all, gam]
    aliases = {}
    if new_state_all is not None:
        in_specs.append(pl.BlockSpec(memory_space=pl.ANY))
        args.append(new_state_all)
        aliases = {len(args) - 1: 1}
    return pl.pallas_call(
        _ret_sample_kernel,
        grid=(n // SAMPLE_NB, RET_HEADS),
        in_specs=in_specs,
        out_specs=[pl.BlockSpec((SAMPLE_NB, HEAD_DIM), lambda i, h: (i, h)), state_spec],
        out_shape=[jax.ShapeDtypeStruct((n, D_MODEL), BF16),
                   jax.ShapeDtypeStruct(state_all.shape, F32)],
        input_output_aliases=aliases,
        compiler_params=_params("parallel", "arbitrary"),
        name="ret_sample",
    )(*args)


def _pool_prompt_kernel(u_ref, halo_ref, wp_ref, sc_ref, mix_hbm, o_ref, ext_scr, *,
                        tiles_per_seq):
    del mix_hbm
    i = pl.program_id(0)
    tm = u_ref.shape[0]
    t0 = (i % tiles_per_seq) * tm
    ext_scr[0:HALO_ROWS, :] = jnp.where(t0 == 0, 0.0, halo_ref[...])
    ext_scr[HALO_ROWS:HALO_ROWS + tm, :] = u_ref[...]
    pos = t0 + lax.broadcasted_iota(jnp.int32, (tm, 1), 0)
    for gi, w in enumerate(POOL_WINDOWS):
        cols = pl.ds(gi * POOL_GROUP_DIM, POOL_GROUP_DIM)
        u = u_ref[:, cols]
        s = u
        for back in range(1, w):
            s = s + ext_scr[pl.ds(HALO_ROWS - back, tm), cols]
        cnt = jnp.minimum(w, pos + 1).astype(F32)
        pooled = s / cnt - u
        mixed = jnp.dot(pooled.astype(BF16), wp_ref[gi], preferred_element_type=F32)
        o_ref[:, cols] = (mixed * sc_ref[:, cols]).astype(BF16)


def _pool_prompt(proj, w_pool_bf, scale, mix, seq):
    m = proj.shape[0]
    tiles_per_seq = seq // POOL_TM
    u_col = 4 * RET_WIDTH // POOL_WIDTH
    halo_per_tile = POOL_TM // HALO_ROWS
    kern = functools.partial(_pool_prompt_kernel, tiles_per_seq=tiles_per_seq)
    return pl.pallas_call(
        kern,
        grid=(m // POOL_TM,),
        in_specs=[
            pl.BlockSpec((POOL_TM, POOL_WIDTH), lambda i: (i, u_col)),
            pl.BlockSpec((HALO_ROWS, POOL_WIDTH),
                         lambda i: (jnp.maximum(i * halo_per_tile - 1, 0), u_col)),
            pl.BlockSpec(w_pool_bf.shape, lambda i: (0, 0, 0)),
            pl.BlockSpec((1, POOL_WIDTH), lambda i: (0, 0)),
            pl.BlockSpec(memory_space=pl.ANY),
        ],
        out_specs=pl.BlockSpec((POOL_TM, POOL_WIDTH), lambda i: (i, 1)),
        out_shape=jax.ShapeDtypeStruct(mix.shape, BF16),
        input_output_aliases={4: 0},
        scratch_shapes=[pltpu.VMEM((HALO_ROWS + POOL_TM, POOL_WIDTH), F32)],
        compiler_params=_params("parallel"),
        name="pool_prompt",
    )(proj, proj, w_pool_bf, scale, mix)


def _pool_sample_kernel(u_ref, buf_ref, wp_ref, sc_ref, *rest):
    o_ref, nbuf_ref = rest[-2:]
    nbuf_ref[:, 0:POOL_BUF - 1, :] = buf_ref[:, 1:POOL_BUF, :]
    nbuf_ref[:, POOL_BUF - 1, :] = u_ref[...]
    row = lax.broadcasted_iota(jnp.int32, (1, POOL_BUF, 1), 1)
    for gi, w in enumerate(POOL_WINDOWS):
        cols = pl.ds(gi * POOL_GROUP_DIM, POOL_GROUP_DIM)
        u = u_ref[:, cols]
        past = jnp.where(row >= POOL_BUF - (w - 1), buf_ref[:, :, cols], 0.0)
        s = u + jnp.sum(past, axis=1)
        pooled = s / float(w) - u
        mixed = jnp.dot(pooled.astype(BF16), wp_ref[gi], preferred_element_type=F32)
        o_ref[:, cols] = (mixed * sc_ref[:, cols]).astype(BF16)


def _pool_sample(proj, buf_all, li, w_pool_bf, scale, mix, new_buf_all):
    n = proj.shape[0]
    u_col = 4 * RET_WIDTH // POOL_WIDTH
    buf_spec = pl.BlockSpec((None, SAMPLE_NB, POOL_BUF, POOL_WIDTH), lambda i: (li, i, 0, 0))
    in_specs = [
        pl.BlockSpec((SAMPLE_NB, POOL_WIDTH), lambda i: (i, u_col)),
        buf_spec,
        pl.BlockSpec(w_pool_bf.shape, lambda i: (0, 0, 0)),
        pl.BlockSpec((1, POOL_WIDTH), lambda i: (0, 0)),
        pl.BlockSpec(memory_space=pl.ANY),
    ]
    args = [proj, buf_all, w_pool_bf, scale, mix]
    aliases = {4: 0}
    if new_buf_all is not None:
        in_specs.append(pl.BlockSpec(memory_space=pl.ANY))
        args.append(new_buf_all)
        aliases[5] = 1
    return pl.pallas_call(
        _pool_sample_kernel,
        grid=(n // SAMPLE_NB,),
        in_specs=in_specs,
        out_specs=[pl.BlockSpec((SAMPLE_NB, POOL_WIDTH), lambda i: (i, 1)), buf_spec],
        out_shape=[jax.ShapeDtypeStruct(mix.shape, BF16),
                   jax.ShapeDtypeStruct(buf_all.shape, F32)],
        input_output_aliases=aliases,
        compiler_params=_params("parallel"),
        name="pool_sample",
    )(*args)


def _out_proj_kernel(mix_ref, w_ref, x_ref, g_ref, b_ref, y_hbm, ybf_hbm, acc_even, acc_odd,
                     ybf_scr, sems, *, slabs_per_step):
    i = pl.program_id(0)
    j = pl.program_id(1)
    nt = pl.num_programs(0) - 1
    tm = mix_ref.shape[0]
    tn = w_ref.shape[1]
    g = g_ref[...]
    b = b_ref[...]

    @pl.when((i == 0) & (j == 0))
    def _():
        acc_odd[...] = jnp.zeros_like(acc_odd)

    def step(acc, prev):
        @pl.when((j == 0) & (i >= 2))
        def _():
            _layer_norm_wait(acc, ybf_scr, y_hbm, ybf_hbm, sems, (i - 2) * tm)

        @pl.when(i < nt)
        def _():
            for c in range(tn // SUB_TN):
                sub = pl.ds(c * SUB_TN, SUB_TN)
                cols = pl.ds(pl.multiple_of(j * tn + c * SUB_TN, SUB_TN), SUB_TN)
                mix = jnp.dot(mix_ref[...], w_ref[:, sub], preferred_element_type=F32)
                acc[:, cols] = DN_ALPHA * x_ref[:, sub] + mix

            for s in range(slabs_per_step):
                r = j * slabs_per_step + s
                _layer_norm_slab(prev, ybf_scr, g, b, r)

                @pl.when(i > 0)
                def _():
                    for c in _ln_copies(prev, ybf_scr, y_hbm, ybf_hbm, sems, (i - 1) * tm, r):
                        c.start()

        @pl.when((i == nt) & (j == 0))
        def _():
            _layer_norm_start(prev, ybf_scr, g_ref, b_ref, y_hbm, ybf_hbm, sems, (nt - 1) * tm)
            _layer_norm_wait(prev, ybf_scr, y_hbm, ybf_hbm, sems, (nt - 1) * tm)

    @pl.when(i % 2 == 0)
    def _():
        step(acc_even, acc_odd)

    @pl.when(i % 2 == 1)
    def _():
        step(acc_odd, acc_even)


def _out_proj(mix, w_o_bf, x, g, b, tm):
    m = x.shape[0]
    nt = m // tm
    nj = D_MODEL // OUT_TN
    n_slabs = tm // LN_ROWS
    slabs_per_step = n_slabs // nj if nt > 1 else 0
    assert nt == 1 or slabs_per_step * nj == n_slabs
    row = lambda i: jnp.minimum(i, nt - 1)
    col = lambda i, j: jnp.where(i < nt, j, nj - 1)
    kern = functools.partial(_out_proj_kernel, slabs_per_step=slabs_per_step)
    return pl.pallas_call(
        kern,
        grid=(nt + 1, nj),
        in_specs=[
            pl.BlockSpec((tm, D_MODEL), lambda i, j: (row(i), 0)),
            pl.BlockSpec((D_MODEL, OUT_TN), lambda i, j: (0, col(i, j))),
            pl.BlockSpec((tm, OUT_TN), lambda i, j: (row(i), col(i, j))),
            pl.BlockSpec((1, D_MODEL), lambda i, j: (0, 0)),
            pl.BlockSpec((1, D_MODEL), lambda i, j: (0, 0)),
        ],
        out_specs=[pl.BlockSpec(memory_space=pl.ANY), pl.BlockSpec(memory_space=pl.ANY)],
        out_shape=[jax.ShapeDtypeStruct((m, D_MODEL), F32),
                   jax.ShapeDtypeStruct((m, D_MODEL), BF16)],
        scratch_shapes=_ln_scratch(tm),
        compiler_params=_params("arbitrary", "arbitrary"),
        name="out_proj",
    )(mix, w_o_bf, x, g, b)


def _shift_rows(x, carry_row, first8):
    rolled = pltpu.roll(x, 1, axis=0)
    fill = 0.0 if carry_row is None else carry_row
    head = jnp.where(first8 == 0, fill, rolled[0:SUBLANES, :])
    return jnp.concatenate([head, rolled[SUBLANES:, :]], axis=0)


def _ffn_up_kernel(xp_ref, xs_ref, wa_ref, wb_ref, cw_ref, cb_ref, sc_ref, wd_ref, *rest):
    hp_ref, tail_ref, hs_ref, nsc_ref, wdbf_ref, w_scr = rest[-6:]
    i = pl.program_id(0)
    tm = xp_ref.shape[0]
    tn = wa_ref.shape[1]
    w_scr[:, 0:tn] = wa_ref[...].astype(BF16)
    w_scr[:, tn:2 * tn] = wb_ref[...].astype(BF16)
    w = w_scr[...]
    cw0 = cw_ref[0:1, :]
    cw1 = cw_ref[1:2, :]
    cw2 = cw_ref[2:3, :]
    cb = cb_ref[...]

    wdbf_ref[...] = wd_ref[...].astype(BF16)

    first8 = lax.broadcasted_iota(jnp.int32, (SUBLANES, tn), 0)
    p_last = None
    q_last = None
    for r in range(tm // DOT_ROWS):
        r0 = r * DOT_ROWS
        ab = jnp.dot(xp_ref[pl.ds(r0, DOT_ROWS), :], w, preferred_element_type=F32)
        a = ab[:, 0:tn]
        p = a * cw0
        q = a * cw1 + _shift_rows(p, p_last, first8)
        conv = (cb + a * cw2) + _shift_rows(q, q_last, first8)
        p_last = p[DOT_ROWS - 1:DOT_ROWS, :]
        q_last = q[DOT_ROWS - 1:DOT_ROWS, :]
        hp_ref[pl.ds(r0, DOT_ROWS), :] = (_gelu_exact(conv) * ab[:, tn:2 * tn]).astype(BF16)
    tail_ref[0] = a[DOT_ROWS - SUBLANES:DOT_ROWS, :]

    @pl.when(i == 0)
    def _():
        ab_s = jnp.dot(xs_ref[...], w_scr[...], preferred_element_type=F32)
        a_s = ab_s[:, 0:tn]
        b_s = ab_s[:, tn:2 * tn]
        s1 = sc_ref[:, 1, :]
        conv_s = cb + sc_ref[:, 0, :] * cw0
        conv_s = conv_s + s1 * cw1
        conv_s = conv_s + a_s * cw2
        hs_ref[...] = (_gelu_exact(conv_s) * b_s).astype(BF16)
        nsc_ref[:, 0, :] = s1
        nsc_ref[:, 1, :] = a_s


def _ffn_up(xp_bf, xs_bf, w_up, conv_w, conv_b, state_all, new_state_all, w_down, li, seq):
    mp = xp_bf.shape[0]
    ns = xs_bf.shape[0]
    nj = D_FF // FFN_TN
    tm = seq
    assert (mp // tm) * nj * CAST_ROWS == D_FF
    first = lambda i, j: jnp.where(i == 0, j, nj - 1)
    state_spec = pl.BlockSpec((None, ns, CONV_WIDTH - 1, FFN_TN),
                              lambda i, j: (li, 0, 0, first(i, j)))
    in_specs = [
        pl.BlockSpec((tm, D_MODEL), lambda i, j: (i, 0), pipeline_mode=pl.Buffered(1)),
        pl.BlockSpec((ns, D_MODEL), lambda i, j: (0, 0)),
        pl.BlockSpec((None, D_MODEL, FFN_TN), lambda i, j: (li, 0, j)),
        pl.BlockSpec((None, D_MODEL, FFN_TN), lambda i, j: (li, 0, nj + j)),
        pl.BlockSpec((None, CONV_WIDTH, FFN_TN), lambda i, j: (li, 0, j)),
        pl.BlockSpec((None, 1, FFN_TN), lambda i, j: (li, 0, j)),
        state_spec,
        pl.BlockSpec((None, CAST_ROWS, D_MODEL), lambda i, j: (li, i * nj + j, 0)),
    ]
    args = [xp_bf, xs_bf, w_up, w_up, conv_w, conv_b, state_all, w_down]
    aliases = {}
    if new_state_all is not None:
        in_specs.append(pl.BlockSpec(memory_space=pl.ANY))
        args.append(new_state_all)
        aliases = {len(args) - 1: 3}
    return pl.pallas_call(
        _ffn_up_kernel,
        grid=(mp // tm, nj),
        in_specs=in_specs,
        out_specs=[
            pl.BlockSpec((tm, FFN_TN), lambda i, j: (i, j)),
            pl.BlockSpec((1, SUBLANES, FFN_TN), lambda i, j: (i, 0, j)),
            pl.BlockSpec((ns, FFN_TN), lambda i, j: (0, first(i, j))),
            state_spec,
            pl.BlockSpec((CAST_ROWS, D_MODEL), lambda i, j: (i * nj + j, 0)),
        ],
        out_shape=[jax.ShapeDtypeStruct((mp, D_FF), BF16),
                   jax.ShapeDtypeStruct((mp // tm, SUBLANES, D_FF), F32),
                   jax.ShapeDtypeStruct((ns, D_FF), BF16),
                   jax.ShapeDtypeStruct(state_all.shape, F32),
                   jax.ShapeDtypeStruct((D_FF, D_MODEL), BF16)],
        input_output_aliases=aliases,
        scratch_shapes=[pltpu.VMEM((D_MODEL, 2 * FFN_TN), BF16)],
        compiler_params=_params("arbitrary", "arbitrary"),
        name="ffn_up",
    )(*args)


def _ffn_down_kernel(h_ref, w_ref, x_ref, g_ref, b_ref, y_hbm, ybf_hbm, acc_even, acc_odd,
                     ybf_scr, sems, *, slabs_per_step):
    i = pl.program_id(0)
    k = pl.program_id(1)
    j = pl.program_id(2)
    nt = pl.num_programs(0) - 1
    nj = pl.num_programs(2)
    tm = h_ref.shape[0]
    tn = w_ref.shape[1]
    first = (k == 0) & (j == 0)
    g = g_ref[...]
    b = b_ref[...]

    @pl.when(first & (i == 0))
    def _():
        acc_even[...] = jnp.zeros_like(acc_even)
        acc_odd[...] = jnp.zeros_like(acc_odd)

    def step(acc, prev):
        @pl.when(first & (i >= 2))
        def _():
            _layer_norm_wait(acc, ybf_scr, y_hbm, ybf_hbm, sems, (i - 2) * tm)

        @pl.when(i < nt)
        def _():
            for c in range(tn // SUB_TN):
                sub = pl.ds(c * SUB_TN, SUB_TN)
                cols = pl.ds(pl.multiple_of(j * tn + c * SUB_TN, SUB_TN), SUB_TN)
                part = jnp.dot(h_ref[...], w_ref[:, sub], preferred_element_type=F32)
                base = jnp.where(k == 0, DN_ALPHA * x_ref[:, sub], acc[:, cols])
                acc[:, cols] = base + part

            for s in range(slabs_per_step):
                r = (k * nj + j) * slabs_per_step + s
                _layer_norm_slab(prev, ybf_scr, g, b, r)

                @pl.when(i > 0)
                def _():
                    for c in _ln_copies(prev, ybf_scr, y_hbm, ybf_hbm, sems, (i - 1) * tm, r):
                        c.start()

        @pl.when((i == nt) & first)
        def _():
            _layer_norm_start(prev, ybf_scr, g_ref, b_ref, y_hbm, ybf_hbm, sems, (nt - 1) * tm)
            _layer_norm_wait(prev, ybf_scr, y_hbm, ybf_hbm, sems, (nt - 1) * tm)

    @pl.when(i % 2 == 0)
    def _():
        step(acc_even, acc_odd)

    @pl.when(i % 2 == 1)
    def _():
        step(acc_odd, acc_even)


def _ffn_down(h, w_down_bf, x, g, b, tm):
    m = x.shape[0]
    nt = m // tm
    nk = D_FF // DOWN_TK
    nj = D_MODEL // DOWN_TN
    n_slabs = tm // LN_ROWS
    slabs_per_step = n_slabs // (nk * nj) if nt > 1 else 0
    assert nt == 1 or slabs_per_step * nk * nj == n_slabs
    row = lambda i: jnp.minimum(i, nt - 1)
    kk = lambda i, k: jnp.where(i < nt, k, nk - 1)
    jj = lambda i, j: jnp.where(i < nt, j, nj - 1)
    kern = functools.partial(_ffn_down_kernel, slabs_per_step=slabs_per_step)
    return pl.pallas_call(
        kern,
        grid=(nt + 1, nk, nj),
        in_specs=[
            pl.BlockSpec((tm, DOWN_TK), lambda i, k, j: (row(i), kk(i, k))),
            pl.BlockSpec((DOWN_TK, DOWN_TN), lambda i, k, j: (kk(i, k), jj(i, j))),
            pl.BlockSpec((tm, DOWN_TN),
                         lambda i, k, j: (row(i), jnp.where((k == 0) & (i < nt), j, nj - 1))),
            pl.BlockSpec((1, D_MODEL), lambda i, k, j: (0, 0)),
            pl.BlockSpec((1, D_MODEL), lambda i, k, j: (0, 0)),
        ],
        out_specs=[pl.BlockSpec(memory_space=pl.ANY), pl.BlockSpec(memory_space=pl.ANY)],
        out_shape=[jax.ShapeDtypeStruct((m, D_MODEL), F32),
                   jax.ShapeDtypeStruct((m, D_MODEL), BF16)],
        scratch_shapes=_ln_scratch(tm),
        compiler_params=_params("arbitrary", "arbitrary", "arbitrary"),
        name="ffn_down",
    )(h, w_down_bf, x, g, b)


def _rotary_tables(pos):
    inv = ROPE_BASE ** (-jnp.arange(HALF_HEAD, dtype=F32) / HALF_HEAD)
    ang = pos.astype(F32)[:, None] * inv[None, :]
    return jnp.cos(ang), jnp.sin(ang)


def _retention_tables(l):
    log_g = jnp.log1p(-(2.0 ** (-5.0 - jnp.arange(RET_HEADS, dtype=F32))))
    i = jnp.arange(l)
    diff = i[:, None] - i[None, :]
    decay = jnp.where(diff[None] >= 0,
                      jnp.exp(jnp.maximum(diff, 0)[None].astype(F32) * log_g[:, None, None]), 0.0)
    xi = jnp.exp((i + 1)[None].astype(F32) * log_g[:, None])
    zeta = jnp.exp((l - 1 - i)[None].astype(F32) * log_g[:, None])
    g_l = jnp.exp(l * log_g)
    return decay, xi, zeta, g_l


def kernel(x_prompt, x_sample, state_ret, state_pool, state_conv, w_in, w_pool, pool_scale, w_o,
           ln1_g, ln1_b, w_up, conv_w, conv_b, w_down, ln2_g, ln2_b):
    batch, seq, _ = x_prompt.shape
    n_s = x_sample.shape[0]
    assert x_sample.shape[1] == 1 and seq % RET_CHUNK == 0
    mp = batch * seq

    cos_p, sin_p = _rotary_tables(jnp.arange(seq))
    cos_s, sin_s = _rotary_tables(PAST_LEN + jnp.arange(1))
    cos_s = jnp.broadcast_to(cos_s, (n_s, HALF_HEAD))
    sin_s = jnp.broadcast_to(sin_s, (n_s, HALF_HEAD))
    decay, xi, zeta, g_l = _retention_tables(RET_CHUNK)
    xi = xi[:, :, None]
    zeta = zeta[:, :, None]
    gl_b = jnp.broadcast_to(g_l[:, None, None], (RET_HEADS, 1, HEAD_DIM))
    _, xi_s, _, _ = _retention_tables(1)
    gam_s = jnp.broadcast_to(xi_s[:, :, None], (RET_HEADS, 1, HEAD_DIM))

    xp = x_prompt.reshape(mp, D_MODEL)
    xs = x_sample.reshape(n_s, D_MODEL)
    xp_bf = xp.astype(BF16)
    xs_bf = xs.astype(BF16)
    conv_b3 = conv_b[:, None, :]

    pool_p, conv_p = [], []
    new_ret_prompt = new_ret_sample = new_pool_sample = new_conv_sample = None
    for li in range(DEPTH):
        w_pool_bf = w_pool[li].astype(BF16)
        scale = pool_scale[li][None, :]
        g1, b1 = ln1_g[li][None, :], ln1_b[li][None, :]
        g2, b2 = ln2_g[li][None, :], ln2_b[li][None, :]

        proj_p, proj_s, w_o_bf = _proj(xp_bf, xs_bf, w_in, w_o, li, cos_p, sin_p, cos_s, sin_s, seq)

        mix, new_ret_prompt = _ret_prompt(proj_p, decay, xi, zeta, gl_b, li, new_ret_prompt,
                                          batch, seq)
        mix = _pool_prompt(proj_p, w_pool_bf, scale, mix, seq)
        xp, xp_bf = _out_proj(mix, w_o_bf, xp, g1, b1, LN_TM)
        pool_p.append(proj_p.reshape(batch, seq, IN_WIDTH)[:, seq - POOL_BUF:, 4 * RET_WIDTH:])

        mix, new_ret_sample = _ret_sample(proj_s, state_ret, li, gam_s, new_ret_sample)
        mix, new_pool_sample = _pool_sample(proj_s, state_pool, li, w_pool_bf, scale, mix,
                                            new_pool_sample)
        xs, xs_bf = _out_proj(mix, w_o_bf, xs, g1, b1, n_s)

        h_p, tail, h_s, new_conv_sample, w_down_bf = _ffn_up(
            xp_bf, xs_bf, w_up, conv_w, conv_b3, state_conv, new_conv_sample, w_down, li, seq)
        xp, xp_bf = _ffn_down(h_p, w_down_bf, xp, g2, b2, LN_TM)
        xs, xs_bf = _ffn_down(h_s, w_down_bf, xs, g2, b2, n_s)
        conv_p.append(tail[:, SUBLANES - (CONV_WIDTH - 1):, :])

    return (xp.reshape(batch, seq, D_MODEL), xs.reshape(n_s, 1, D_MODEL),
            new_ret_prompt, new_ret_sample, jnp.stack(pool_p), new_pool_sample,
            jnp.stack(conv_p), new_conv_sample)
```

```python
import functools

import jax
import jax.numpy as jnp
from jax import lax
from jax.experimental import pallas as pl
from jax.experimental.pallas import tpu as pltpu

F32 = jnp.float32
BF16 = jnp.bfloat16

D_MODEL = 4096
DEPTH = 2
PAST_LEN = 16384
RET_WIDTH = D_MODEL // 2
RET_HEADS = 8
HEAD_DIM = RET_WIDTH // RET_HEADS
HALF_HEAD = HEAD_DIM // 2
RET_CHUNK = 128
ROPE_BASE = 10000.0
POOL_WIDTH = D_MODEL - RET_WIDTH
POOL_WINDOWS = (2, 4, 8, 16)
POOL_GROUP_DIM = POOL_WIDTH // len(POOL_WINDOWS)
POOL_BUF = max(POOL_WINDOWS) - 1
IN_WIDTH = 4 * RET_WIDTH + POOL_WIDTH
D_FF = ((8 * D_MODEL // 3) + 255) // 256 * 256
CONV_WIDTH = 3
DN_ALPHA = (2.0 * DEPTH) ** 0.25
LN_EPS = 1e-5
K_SCALE = HEAD_DIM ** -0.5

V7X_VMEM_BYTES = 64 * 1024 * 1024
VMEM_LIMIT = V7X_VMEM_BYTES - 3 * 1024 * 1024
SUBLANES = 8
HALO_ROWS = 2 * SUBLANES

PROJ_TN = 512
FFN_TN = 256
LN_TM = 512
OUT_TN = 1024
SUB_TN = 512
LN_ROWS = 64
DOWN_TK = D_FF // 2
DOWN_TN = 1024
POOL_TM = 512
SAMPLE_NB = 16
DOT_ROWS = 1024
CAST_ROWS = 64


def _params(*semantics):
    return pltpu.CompilerParams(dimension_semantics=semantics, vmem_limit_bytes=VMEM_LIMIT)


def _layer_norm(y, g, b):
    mu = jnp.mean(y, axis=-1, keepdims=True)
    d = y - mu
    var = jnp.mean(d * d, axis=-1, keepdims=True)
    return d * lax.rsqrt(var + LN_EPS) * g + b


def _ln_copies(acc_ref, ybf_scr, y_hbm, ybf_hbm, sems, row0, r):
    src = pl.ds(pl.multiple_of(r * LN_ROWS, LN_ROWS), LN_ROWS)
    dst = pl.ds(pl.multiple_of(row0 + r * LN_ROWS, LN_ROWS), LN_ROWS)
    return (pltpu.make_async_copy(acc_ref.at[src, :], y_hbm.at[dst, :], sems.at[0, r]),
            pltpu.make_async_copy(ybf_scr.at[src, :], ybf_hbm.at[dst, :], sems.at[1, r]))


def _layer_norm_slab(acc_ref, ybf_scr, g, b, r):
    rows = pl.ds(pl.multiple_of(r * LN_ROWS, LN_ROWS), LN_ROWS)
    y = _layer_norm(acc_ref[rows, :], g, b)
    acc_ref[rows, :] = y
    ybf_scr[rows, :] = y.astype(BF16)


def _layer_norm_start(acc_ref, ybf_scr, g_ref, b_ref, y_hbm, ybf_hbm, sems, row0):
    g = g_ref[...]
    b = b_ref[...]

    def body(r, carry):
        _layer_norm_slab(acc_ref, ybf_scr, g, b, r)
        for c in _ln_copies(acc_ref, ybf_scr, y_hbm, ybf_hbm, sems, row0, r):
            c.start()
        return carry

    lax.fori_loop(0, acc_ref.shape[0] // LN_ROWS, body, 0)


def _layer_norm_wait(acc_ref, ybf_scr, y_hbm, ybf_hbm, sems, row0):
    for r in range(acc_ref.shape[0] // LN_ROWS):
        for c in _ln_copies(acc_ref, ybf_scr, y_hbm, ybf_hbm, sems, row0, r):
            c.wait()


def _ln_scratch(tm):
    return [pltpu.VMEM((tm, D_MODEL), F32),
            pltpu.VMEM((tm, D_MODEL), F32),
            pltpu.VMEM((tm, D_MODEL), BF16),
            pltpu.SemaphoreType.DMA((2, tm // LN_ROWS))]


def _gelu_exact(x):
    return 0.5 * x * (1.0 + lax.erf(x * (0.5 ** 0.5)))


def _proj_kernel(xp_ref, xs_ref, w_ref, wo_ref, op_ref, os_ref, wobf_ref, *, cast_steps):
    i = pl.program_id(0)
    j = pl.program_id(1)
    w = w_ref[...].astype(BF16)
    for r in range(xp_ref.shape[0] // DOT_ROWS):
        rows = pl.ds(r * DOT_ROWS, DOT_ROWS)
        op_ref[rows, :] = jnp.dot(xp_ref[rows, :], w, preferred_element_type=F32)

    @pl.when(i * pl.num_programs(1) + j < cast_steps)
    def _():
        wobf_ref[...] = wo_ref[...].astype(BF16)

    @pl.when(i == 0)
    def _():
        os_ref[...] = jnp.dot(xs_ref[...], w, preferred_element_type=F32)


def _proj(xp_bf, xs_bf, w_in, w_o, li, tm):
    mp = xp_bf.shape[0]
    ns = xs_bf.shape[0]
    nj = IN_WIDTH // PROJ_TN
    cast_steps = D_MODEL // CAST_ROWS
    assert cast_steps <= (mp // tm) * nj
    kern = functools.partial(_proj_kernel, cast_steps=cast_steps)
    whole = lambda a: pl.BlockSpec(a.shape, lambda i, j: (0, 0))
    slab = lambda i, j: jnp.minimum(i * nj + j, cast_steps - 1)
    return pl.pallas_call(
        kern,
        grid=(mp // tm, nj),
        in_specs=[
            pl.BlockSpec((tm, D_MODEL), lambda i, j: (i, 0), pipeline_mode=pl.Buffered(1)),
            whole(xs_bf),
            pl.BlockSpec((None, D_MODEL, PROJ_TN), lambda i, j: (li, 0, j)),
            pl.BlockSpec((None, CAST_ROWS, D_MODEL), lambda i, j: (li, slab(i, j), 0)),
        ],
        out_specs=[
            pl.BlockSpec((tm, PROJ_TN), lambda i, j: (i, j)),
            pl.BlockSpec((ns, PROJ_TN), lambda i, j: (0, jnp.where(i == 0, j, nj - 1))),
            pl.BlockSpec((CAST_ROWS, D_MODEL), lambda i, j: (slab(i, j), 0)),
        ],
        out_shape=[jax.ShapeDtypeStruct((mp, IN_WIDTH), F32),
                   jax.ShapeDtypeStruct((ns, IN_WIDTH), F32),
                   jax.ShapeDtypeStruct((D_MODEL, D_MODEL), BF16)],
        compiler_params=_params("arbitrary", "arbitrary"),
        name="proj",
    )(xp_bf, xs_bf, w_in, w_o)


def _group_norm_gate(o, g):
    mu = jnp.mean(o, axis=-1, keepdims=True)
    d = o - mu
    var = jnp.mean(d * d, axis=-1, keepdims=True)
    return d * lax.rsqrt(var + LN_EPS) * (g * jax.nn.sigmoid(g))


def _rotary(x, cos, sin):
    x1 = x[:, 0:HALF_HEAD]
    x2 = x[:, HALF_HEAD:HEAD_DIM]
    return jnp.concatenate([x1 * cos - x2 * sin, x2 * cos + x1 * sin], axis=-1)


def _ret_prompt_kernel(q_ref, k_ref, v_ref, g_ref, cos_ref, sin_ref, decay_ref, xi_ref, zeta_ref,
                       gl_ref, *rest):
    o_ref, snew_ref, s_scr = rest[-3:]
    c = pl.program_id(1)

    @pl.when(c == 0)
    def _():
        s_scr[...] = jnp.zeros_like(s_scr)

    cos = cos_ref[...]
    sin = sin_ref[...]
    for h in range(RET_HEADS):
        cols = pl.ds(h * HEAD_DIM, HEAD_DIM)
        q = _rotary(q_ref[:, cols], cos, sin)
        k = _rotary(k_ref[:, cols], cos, sin) * K_SCALE
        vb = v_ref[:, cols].astype(BF16)
        s = s_scr[h]
        scores = lax.dot_general(q.astype(BF16), k.astype(BF16), (((1,), (1,)), ((), ())),
                                 preferred_element_type=F32) * decay_ref[h]
        o = jnp.dot(scores.astype(BF16), vb, preferred_element_type=F32)
        o = o + jnp.dot((q * xi_ref[h]).astype(BF16), s.astype(BF16), preferred_element_type=F32)
        kz_t = (k * zeta_ref[h]).T.astype(BF16)
        s_scr[h] = gl_ref[h] * s + jnp.dot(kz_t, vb, preferred_element_type=F32)
        o_ref[:, cols] = _group_norm_gate(o, g_ref[:, cols]).astype(BF16)

    @pl.when(c == pl.num_programs(1) - 1)
    def _():
        snew_ref[...] = s_scr[...]


def _ret_prompt(proj, cos, sin, decay, xi, zeta, gl, li, new_state_all, batch, seq):
    nc = seq // RET_CHUNK

    def col(cb):
        return pl.BlockSpec((RET_CHUNK, RET_WIDTH), lambda b, c: (b * nc + c, cb))

    def whole(a):
        return pl.BlockSpec(a.shape, lambda b, c: (0,) * a.ndim)

    table = pl.BlockSpec((RET_CHUNK, HALF_HEAD), lambda b, c: (c, 0))
    in_specs = [col(0), col(1), col(2), col(3), table, table,
                whole(decay), whole(xi), whole(zeta), whole(gl)]
    args = [proj, proj, proj, proj, cos, sin, decay, xi, zeta, gl]
    aliases = {}
    if new_state_all is not None:
        in_specs.append(pl.BlockSpec(memory_space=pl.ANY))
        args.append(new_state_all)
        aliases = {len(args) - 1: 1}
    return pl.pallas_call(
        _ret_prompt_kernel,
        grid=(batch, nc),
        in_specs=in_specs,
        out_specs=[
            pl.BlockSpec((RET_CHUNK, RET_WIDTH), lambda b, c: (b * nc + c, 0)),
            pl.BlockSpec((None, None, RET_HEADS, HEAD_DIM, HEAD_DIM), lambda b, c: (li, b, 0, 0, 0)),
        ],
        out_shape=[
            jax.ShapeDtypeStruct((batch * seq, D_MODEL), BF16),
            jax.ShapeDtypeStruct((DEPTH, batch, RET_HEADS, HEAD_DIM, HEAD_DIM), F32),
        ],
        input_output_aliases=aliases,
        scratch_shapes=[pltpu.VMEM((RET_HEADS, HEAD_DIM, HEAD_DIM), F32)],
        compiler_params=_params("parallel", "arbitrary"),
        name="ret_prompt",
    )(*args)


def _ret_sample_kernel(q_ref, k_ref, v_ref, g_ref, cos_ref, sin_ref, s0_ref, gam_ref, *rest):
    o_ref, snew_ref = rest[-2:]
    q = _rotary(q_ref[...], cos_ref[...], sin_ref[...])
    k = _rotary(k_ref[...], cos_ref[...], sin_ref[...]) * K_SCALE
    v = v_ref[...]
    vb = v.astype(BF16)
    gam = gam_ref[0]
    qg = (q * gam).astype(BF16)
    rows = lax.broadcasted_iota(jnp.int32, (SAMPLE_NB, 1), 0)
    o_state = jnp.zeros((SAMPLE_NB, HEAD_DIM), F32)
    for n in range(SAMPLE_NB):
        s = s0_ref[n, 0]
        r = jnp.dot(qg, s.astype(BF16), preferred_element_type=F32)
        o_state = jnp.where(rows == n, r, o_state)
        k_n_t = jnp.where(rows == n, k, 0.0).T.astype(BF16)
        snew_ref[n, 0] = gam * s + jnp.dot(k_n_t, vb, preferred_element_type=F32)
    qk = jnp.sum(q * k, axis=-1, keepdims=True)
    o = qk * v + o_state
    o_ref[...] = _group_norm_gate(o, g_ref[...]).astype(BF16)


def _ret_sample(proj, cos, sin, state_all, li, gam, new_state_all):
    n = proj.shape[0]

    def col(cb):
        return pl.BlockSpec((SAMPLE_NB, HEAD_DIM), lambda i, h: (i, cb * RET_HEADS + h))

    state_spec = pl.BlockSpec((None, SAMPLE_NB, 1, HEAD_DIM, HEAD_DIM),
                              lambda i, h: (li, i, h, 0, 0))
    table = pl.BlockSpec((SAMPLE_NB, HALF_HEAD), lambda i, h: (i, 0))
    in_specs = [col(0), col(1), col(2), col(3), table, table, state_spec,
                pl.BlockSpec((1, 1, HEAD_DIM), lambda i, h: (h, 0, 0))]
    args = [proj, proj, proj, proj, cos, sin, state_all, gam]
    aliases = {}
    if new_state_all is not None:
        in_specs.append(pl.BlockSpec(memory_space=pl.ANY))
        args.append(new_state_all)
        aliases = {len(args) - 1: 1}
    return pl.pallas_call(
        _ret_sample_kernel,
        grid=(n // SAMPLE_NB, RET_HEADS),
        in_specs=in_specs,
        out_specs=[pl.BlockSpec((SAMPLE_NB, HEAD_DIM), lambda i, h: (i, h)), state_spec],
        out_shape=[jax.ShapeDtypeStruct((n, D_MODEL), BF16),
                   jax.ShapeDtypeStruct(state_all.shape, F32)],
        input_output_aliases=aliases,
        compiler_params=_params("parallel", "arbitrary"),
        name="ret_sample",
    )(*args)


def _pool_prompt_kernel(u_ref, halo_ref, wp_ref, sc_ref, mix_hbm, o_ref, ext_scr, *,
                        tiles_per_seq):
    del mix_hbm
    i = pl.program_id(0)
    tm = u_ref.shape[0]
    t0 = (i % tiles_per_seq) * tm
    ext_scr[0:HALO_ROWS, :] = jnp.where(t0 == 0, 0.0, halo_ref[...])
    ext_scr[HALO_ROWS:HALO_ROWS + tm, :] = u_ref[...]
    pos = t0 + lax.broadcasted_iota(jnp.int32, (tm, 1), 0)
    for gi, w in enumerate(POOL_WINDOWS):
        cols = pl.ds(gi * POOL_GROUP_DIM, POOL_GROUP_DIM)
        u = u_ref[:, cols]
        s = u
        for back in range(1, w):
            s = s + ext_scr[pl.ds(HALO_ROWS - back, tm), cols]
        cnt = jnp.minimum(w, pos + 1).astype(F32)
        pooled = s / cnt - u
        mixed = jnp.dot(pooled.astype(BF16), wp_ref[gi], preferred_element_type=F32)
        o_ref[:, cols] = (mixed * sc_ref[:, cols]).astype(BF16)


def _pool_prompt(proj, w_pool_bf, scale, mix, seq):
    m = proj.shape[0]
    tiles_per_seq = seq // POOL_TM
    u_col = 4 * RET_WIDTH // POOL_WIDTH
    halo_per_tile = POOL_TM // HALO_ROWS
    kern = functools.partial(_pool_prompt_kernel, tiles_per_seq=tiles_per_seq)
    return pl.pallas_call(
        kern,
        grid=(m // POOL_TM,),
        in_specs=[
            pl.BlockSpec((POOL_TM, POOL_WIDTH), lambda i: (i, u_col)),
            pl.BlockSpec((HALO_ROWS, POOL_WIDTH),
                         lambda i: (jnp.maximum(i * halo_per_tile - 1, 0), u_col)),
            pl.BlockSpec(w_pool_bf.shape, lambda i: (0, 0, 0)),
            pl.BlockSpec((1, POOL_WIDTH), lambda i: (0, 0)),
            pl.BlockSpec(memory_space=pl.ANY),
        ],
        out_specs=pl.BlockSpec((POOL_TM, POOL_WIDTH), lambda i: (i, 1)),
        out_shape=jax.ShapeDtypeStruct(mix.shape, BF16),
        input_output_aliases={4: 0},
        scratch_shapes=[pltpu.VMEM((HALO_ROWS + POOL_TM, POOL_WIDTH), F32)],
        compiler_params=_params("parallel"),
        name="pool_prompt",
    )(proj, proj, w_pool_bf, scale, mix)


def _pool_sample_kernel(u_ref, buf_ref, wp_ref, sc_ref, *rest):
    o_ref, nbuf_ref = rest[-2:]
    nbuf_ref[:, 0:POOL_BUF - 1, :] = buf_ref[:, 1:POOL_BUF, :]
    nbuf_ref[:, POOL_BUF - 1, :] = u_ref[...]
    row = lax.broadcasted_iota(jnp.int32, (1, POOL_BUF, 1), 1)
    for gi, w in enumerate(POOL_WINDOWS):
        cols = pl.ds(gi * POOL_GROUP_DIM, POOL_GROUP_DIM)
        u = u_ref[:, cols]
        past = jnp.where(row >= POOL_BUF - (w - 1), buf_ref[:, :, cols], 0.0)
        s = u + jnp.sum(past, axis=1)
        pooled = s / float(w) - u
        mixed = jnp.dot(pooled.astype(BF16), wp_ref[gi], preferred_element_type=F32)
        o_ref[:, cols] = (mixed * sc_ref[:, cols]).astype(BF16)


def _pool_sample(proj, buf_all, li, w_pool_bf, scale, mix, new_buf_all):
    n = proj.shape[0]
    u_col = 4 * RET_WIDTH // POOL_WIDTH
    buf_spec = pl.BlockSpec((None, SAMPLE_NB, POOL_BUF, POOL_WIDTH), lambda i: (li, i, 0, 0))
    in_specs = [
        pl.BlockSpec((SAMPLE_NB, POOL_WIDTH), lambda i: (i, u_col)),
        buf_spec,
        pl.BlockSpec(w_pool_bf.shape, lambda i: (0, 0, 0)),
        pl.BlockSpec((1, POOL_WIDTH), lambda i: (0, 0)),
        pl.BlockSpec(memory_space=pl.ANY),
    ]
    args = [proj, buf_all, w_pool_bf, scale, mix]
    aliases = {4: 0}
    if new_buf_all is not None:
        in_specs.append(pl.BlockSpec(memory_space=pl.ANY))
        args.append(new_buf_all)
        aliases[5] = 1
    return pl.pallas_call(
        _pool_sample_kernel,
        grid=(n // SAMPLE_NB,),
        in_specs=in_specs,
        out_specs=[pl.BlockSpec((SAMPLE_NB, POOL_WIDTH), lambda i: (i, 1)), buf_spec],
        out_shape=[jax.ShapeDtypeStruct(mix.shape, BF16),
                   jax.ShapeDtypeStruct(buf_all.shape, F32)],
        input_output_aliases=aliases,
        compiler_params=_params("parallel"),
        name="pool_sample",
    )(*args)


def _out_proj_kernel(mix_ref, w_ref, x_ref, g_ref, b_ref, y_hbm, ybf_hbm, acc_even, acc_odd,
                     ybf_scr, sems, *, slabs_per_step):
    i = pl.program_id(0)
    j = pl.program_id(1)
    nt = pl.num_programs(0) - 1
    tm = mix_ref.shape[0]
    tn = w_ref.shape[1]
    g = g_ref[...]
    b = b_ref[...]

    @pl.when((i == 0) & (j == 0))
    def _():
        acc_odd[...] = jnp.zeros_like(acc_odd)

    def step(acc, prev):
        @pl.when((j == 0) & (i >= 2))
        def _():
            _layer_norm_wait(acc, ybf_scr, y_hbm, ybf_hbm, sems, (i - 2) * tm)

        @pl.when(i < nt)
        def _():
            for c in range(tn // SUB_TN):
                sub = pl.ds(c * SUB_TN, SUB_TN)
                cols = pl.ds(pl.multiple_of(j * tn + c * SUB_TN, SUB_TN), SUB_TN)
                mix = jnp.dot(mix_ref[...], w_ref[:, sub], preferred_element_type=F32)
                acc[:, cols] = DN_ALPHA * x_ref[:, sub] + mix

            for s in range(slabs_per_step):
                r = j * slabs_per_step + s
                _layer_norm_slab(prev, ybf_scr, g, b, r)

                @pl.when(i > 0)
                def _():
                    for c in _ln_copies(prev, ybf_scr, y_hbm, ybf_hbm, sems, (i - 1) * tm, r):
                        c.start()

        @pl.when((i == nt) & (j == 0))
        def _():
            _layer_norm_start(prev, ybf_scr, g_ref, b_ref, y_hbm, ybf_hbm, sems, (nt - 1) * tm)
            _layer_norm_wait(prev, ybf_scr, y_hbm, ybf_hbm, sems, (nt - 1) * tm)

    @pl.when(i % 2 == 0)
    def _():
        step(acc_even, acc_odd)

    @pl.when(i % 2 == 1)
    def _():
        step(acc_odd, acc_even)


def _out_proj(mix, w_o_bf, x, g, b, tm):
    m = x.shape[0]
    nt = m // tm
    nj = D_MODEL // OUT_TN
    n_slabs = tm // LN_ROWS
    slabs_per_step = n_slabs // nj if nt > 1 else 0
    assert nt == 1 or slabs_per_step * nj == n_slabs
    row = lambda i: jnp.minimum(i, nt - 1)
    col = lambda i, j: jnp.where(i < nt, j, nj - 1)
    kern = functools.partial(_out_proj_kernel, slabs_per_step=slabs_per_step)
    return pl.pallas_call(
        kern,
        grid=(nt + 1, nj),
        in_specs=[
            pl.BlockSpec((tm, D_MODEL), lambda i, j: (row(i), 0)),
            pl.BlockSpec((D_MODEL, OUT_TN), lambda i, j: (0, col(i, j))),
            pl.BlockSpec((tm, OUT_TN), lambda i, j: (row(i), col(i, j))),
            pl.BlockSpec((1, D_MODEL), lambda i, j: (0, 0)),
            pl.BlockSpec((1, D_MODEL), lambda i, j: (0, 0)),
        ],
        out_specs=[pl.BlockSpec(memory_space=pl.ANY), pl.BlockSpec(memory_space=pl.ANY)],
        out_shape=[jax.ShapeDtypeStruct((m, D_MODEL), F32),
                   jax.ShapeDtypeStruct((m, D_MODEL), BF16)],
        scratch_shapes=_ln_scratch(tm),
        compiler_params=_params("arbitrary", "arbitrary"),
        name="out_proj",
    )(mix, w_o_bf, x, g, b)


def _shift_rows(x, carry_row, first8):
    rolled = pltpu.roll(x, 1, axis=0)
    fill = 0.0 if carry_row is None else carry_row
    head = jnp.where(first8 == 0, fill, rolled[0:SUBLANES, :])
    return jnp.concatenate([head, rolled[SUBLANES:, :]], axis=0)


def _ffn_up_kernel(xp_ref, xs_ref, wa_ref, wb_ref, cw_ref, cb_ref, sc_ref, wd_ref, *rest):
    hp_ref, tail_ref, hs_ref, nsc_ref, wdbf_ref, w_scr = rest[-6:]
    i = pl.program_id(0)
    tm = xp_ref.shape[0]
    tn = wa_ref.shape[1]
    w_scr[:, 0:tn] = wa_ref[...].astype(BF16)
    w_scr[:, tn:2 * tn] = wb_ref[...].astype(BF16)
    w = w_scr[...]
    cw0 = cw_ref[0:1, :]
    cw1 = cw_ref[1:2, :]
    cw2 = cw_ref[2:3, :]
    cb = cb_ref[...]

    wdbf_ref[...] = wd_ref[...].astype(BF16)

    first8 = lax.broadcasted_iota(jnp.int32, (SUBLANES, tn), 0)
    p_last = None
    q_last = None
    for r in range(tm // DOT_ROWS):
        r0 = r * DOT_ROWS
        ab = jnp.dot(xp_ref[pl.ds(r0, DOT_ROWS), :], w, preferred_element_type=F32)
        a = ab[:, 0:tn]
        p = a * cw0
        q = a * cw1 + _shift_rows(p, p_last, first8)
        conv = (cb + a * cw2) + _shift_rows(q, q_last, first8)
        p_last = p[DOT_ROWS - 1:DOT_ROWS, :]
        q_last = q[DOT_ROWS - 1:DOT_ROWS, :]
        hp_ref[pl.ds(r0, DOT_ROWS), :] = (_gelu_exact(conv) * ab[:, tn:2 * tn]).astype(BF16)
    tail_ref[0] = a[DOT_ROWS - SUBLANES:DOT_ROWS, :]

    @pl.when(i == 0)
    def _():
        ab_s = jnp.dot(xs_ref[...], w_scr[...], preferred_element_type=F32)
        a_s = ab_s[:, 0:tn]
        b_s = ab_s[:, tn:2 * tn]
        s1 = sc_ref[:, 1, :]
        conv_s = cb + sc_ref[:, 0, :] * cw0
        conv_s = conv_s + s1 * cw1
        conv_s = conv_s + a_s * cw2
        hs_ref[...] = (_gelu_exact(conv_s) * b_s).astype(BF16)
        nsc_ref[:, 0, :] = s1
        nsc_ref[:, 1, :] = a_s


def _ffn_up(xp_bf, xs_bf, w_up, conv_w, conv_b, state_all, new_state_all, w_down, li, seq):
    mp = xp_bf.shape[0]
    ns = xs_bf.shape[0]
    nj = D_FF // FFN_TN
    tm = seq
    assert (mp // tm) * nj * CAST_ROWS == D_FF
    first = lambda i, j: jnp.where(i == 0, j, nj - 1)
    state_spec = pl.BlockSpec((None, ns, CONV_WIDTH - 1, FFN_TN),
                              lambda i, j: (li, 0, 0, first(i, j)))
    in_specs = [
        pl.BlockSpec((tm, D_MODEL), lambda i, j: (i, 0), pipeline_mode=pl.Buffered(1)),
        pl.BlockSpec((ns, D_MODEL), lambda i, j: (0, 0)),
        pl.BlockSpec((None, D_MODEL, FFN_TN), lambda i, j: (li, 0, j)),
        pl.BlockSpec((None, D_MODEL, FFN_TN), lambda i, j: (li, 0, nj + j)),
        pl.BlockSpec((None, CONV_WIDTH, FFN_TN), lambda i, j: (li, 0, j)),
        pl.BlockSpec((None, 1, FFN_TN), lambda i, j: (li, 0, j)),
        state_spec,
        pl.BlockSpec((None, CAST_ROWS, D_MODEL), lambda i, j: (li, i * nj + j, 0)),
    ]
    args = [xp_bf, xs_bf, w_up, w_up, conv_w, conv_b, state_all, w_down]
    aliases = {}
    if new_state_all is not None:
        in_specs.append(pl.BlockSpec(memory_space=pl.ANY))
        args.append(new_state_all)
        aliases = {len(args) - 1: 3}
    return pl.pallas_call(
        _ffn_up_kernel,
        grid=(mp // tm, nj),
        in_specs=in_specs,
        out_specs=[
            pl.BlockSpec((tm, FFN_TN), lambda i, j: (i, j)),
            pl.BlockSpec((1, SUBLANES, FFN_TN), lambda i, j: (i, 0, j)),
            pl.BlockSpec((ns, FFN_TN), lambda i, j: (0, first(i, j))),
            state_spec,
            pl.BlockSpec((CAST_ROWS, D_MODEL), lambda i, j: (i * nj + j, 0)),
        ],
        out_shape=[jax.ShapeDtypeStruct((mp, D_FF), BF16),
                   jax.ShapeDtypeStruct((mp // tm, SUBLANES, D_FF), F32),
                   jax.ShapeDtypeStruct((ns, D_FF), BF16),
                   jax.ShapeDtypeStruct(state_all.shape, F32),
                   jax.ShapeDtypeStruct((D_FF, D_MODEL), BF16)],
        input_output_aliases=aliases,
        scratch_shapes=[pltpu.VMEM((D_MODEL, 2 * FFN_TN), BF16)],
        compiler_params=_params("arbitrary", "arbitrary"),
        name="ffn_up",
    )(*args)


def _ffn_down_kernel(h_ref, w_ref, x_ref, g_ref, b_ref, y_hbm, ybf_hbm, acc_even, acc_odd,
                     ybf_scr, sems, *, slabs_per_step):
    i = pl.program_id(0)
    k = pl.program_id(1)
    j = pl.program_id(2)
    nt = pl.num_programs(0) - 1
    nj = pl.num_programs(2)
    tm = h_ref.shape[0]
    tn = w_ref.shape[1]
    first = (k == 0) & (j == 0)
    g = g_ref[...]
    b = b_ref[...]

    @pl.when(first & (i == 0))
    def _():
        acc_even[...] = jnp.zeros_like(acc_even)
        acc_odd[...] = jnp.zeros_like(acc_odd)

    def step(acc, prev):
        @pl.when(first & (i >= 2))
        def _():
            _layer_norm_wait(acc, ybf_scr, y_hbm, ybf_hbm, sems, (i - 2) * tm)

        @pl.when(i < nt)
        def _():
            for c in range(tn // SUB_TN):
                sub = pl.ds(c * SUB_TN, SUB_TN)
                cols = pl.ds(pl.multiple_of(j * tn + c * SUB_TN, SUB_TN), SUB_TN)
                part = jnp.dot(h_ref[...], w_ref[:, sub], preferred_element_type=F32)
                base = jnp.where(k == 0, DN_ALPHA * x_ref[:, sub], acc[:, cols])
                acc[:, cols] = base + part

            for s in range(slabs_per_step):
                r = (k * nj + j) * slabs_per_step + s
                _layer_norm_slab(prev, ybf_scr, g, b, r)

                @pl.when(i > 0)
                def _():
                    for c in _ln_copies(prev, ybf_scr, y_hbm, ybf_hbm, sems, (i - 1) * tm, r):
                        c.start()

        @pl.when((i == nt) & first)
        def _():
            _layer_norm_start(prev, ybf_scr, g_ref, b_ref, y_hbm, ybf_hbm, sems, (nt - 1) * tm)
            _layer_norm_wait(prev, ybf_scr, y_hbm, ybf_hbm, sems, (nt - 1) * tm)

    @pl.when(i % 2 == 0)
    def _():
        step(acc_even, acc_odd)

    @pl.when(i % 2 == 1)
    def _():
        step(acc_odd, acc_even)


def _ffn_down(h, w_down_bf, x, g, b, tm):
    m = x.shape[0]
    nt = m // tm
    nk = D_FF // DOWN_TK
    nj = D_MODEL // DOWN_TN
    n_slabs = tm // LN_ROWS
    slabs_per_step = n_slabs // (nk * nj) if nt > 1 else 0
    assert nt == 1 or slabs_per_step * nk * nj == n_slabs
    row = lambda i: jnp.minimum(i, nt - 1)
    kk = lambda i, k: jnp.where(i < nt, k, nk - 1)
    jj = lambda i, j: jnp.where(i < nt, j, nj - 1)
    kern = functools.partial(_ffn_down_kernel, slabs_per_step=slabs_per_step)
    return pl.pallas_call(
        kern,
        grid=(nt + 1, nk, nj),
        in_specs=[
            pl.BlockSpec((tm, DOWN_TK), lambda i, k, j: (row(i), kk(i, k))),
            pl.BlockSpec((DOWN_TK, DOWN_TN), lambda i, k, j: (kk(i, k), jj(i, j))),
            pl.BlockSpec((tm, DOWN_TN),
                         lambda i, k, j: (row(i), jnp.where((k == 0) & (i < nt), j, nj - 1))),
            pl.BlockSpec((1, D_MODEL), lambda i, k, j: (0, 0)),
            pl.BlockSpec((1, D_MODEL), lambda i, k, j: (0, 0)),
        ],
        out_specs=[pl.BlockSpec(memory_space=pl.ANY), pl.BlockSpec(memory_space=pl.ANY)],
        out_shape=[jax.ShapeDtypeStruct((m, D_MODEL), F32),
                   jax.ShapeDtypeStruct((m, D_MODEL), BF16)],
        scratch_shapes=_ln_scratch(tm),
        compiler_params=_params("arbitrary", "arbitrary", "arbitrary"),
        name="ffn_down",
    )(h, w_down_bf, x, g, b)


def _rotary_tables(pos):
    inv = ROPE_BASE ** (-jnp.arange(HALF_HEAD, dtype=F32) / HALF_HEAD)
    ang = pos.astype(F32)[:, None] * inv[None, :]
    return jnp.cos(ang), jnp.sin(ang)


def _retention_tables(l):
    log_g = jnp.log1p(-(2.0 ** (-5.0 - jnp.arange(RET_HEADS, dtype=F32))))
    i = jnp.arange(l)
    diff = i[:, None] - i[None, :]
    decay = jnp.where(diff[None] >= 0,
                      jnp.exp(jnp.maximum(diff, 0)[None].astype(F32) * log_g[:, None, None]), 0.0)
    xi = jnp.exp((i + 1)[None].astype(F32) * log_g[:, None])
    zeta = jnp.exp((l - 1 - i)[None].astype(F32) * log_g[:, None])
    g_l = jnp.exp(l * log_g)
    return decay, xi, zeta, g_l


def kernel(x_prompt, x_sample, state_ret, state_pool, state_conv, w_in, w_pool, pool_scale, w_o,
           ln1_g, ln1_b, w_up, conv_w, conv_b, w_down, ln2_g, ln2_b):
    batch, seq, _ = x_prompt.shape
    n_s = x_sample.shape[0]
    assert x_sample.shape[1] == 1 and seq % RET_CHUNK == 0
    mp = batch * seq

    cos_p, sin_p = _rotary_tables(jnp.arange(seq))
    cos_s, sin_s = _rotary_tables(PAST_LEN + jnp.arange(1))
    cos_s = jnp.broadcast_to(cos_s, (n_s, HALF_HEAD))
    sin_s = jnp.broadcast_to(sin_s, (n_s, HALF_HEAD))
    decay, xi, zeta, g_l = _retention_tables(RET_CHUNK)
    xi = xi[:, :, None]
    zeta = zeta[:, :, None]
    gl_b = jnp.broadcast_to(g_l[:, None, None], (RET_HEADS, 1, HEAD_DIM))
    _, xi_s, _, _ = _retention_tables(1)
    gam_s = jnp.broadcast_to(xi_s[:, :, None], (RET_HEADS, 1, HEAD_DIM))

    xp = x_prompt.reshape(mp, D_MODEL)
    xs = x_sample.reshape(n_s, D_MODEL)
    xp_bf = xp.astype(BF16)
    xs_bf = xs.astype(BF16)
    conv_b3 = conv_b[:, None, :]

    pool_p, conv_p = [], []
    new_ret_prompt = new_ret_sample = new_pool_sample = new_conv_sample = None
    for li in range(DEPTH):
        w_pool_bf = w_pool[li].astype(BF16)
        scale = pool_scale[li][None, :]
        g1, b1 = ln1_g[li][None, :], ln1_b[li][None, :]
        g2, b2 = ln2_g[li][None, :], ln2_b[li][None, :]

        proj_p, proj_s, w_o_bf = _proj(xp_bf, xs_bf, w_in, w_o, li, seq)

        mix, new_ret_prompt = _ret_prompt(proj_p, cos_p, sin_p, decay, xi, zeta, gl_b, li,
                                          new_ret_prompt, batch, seq)
        mix = _pool_prompt(proj_p, w_pool_bf, scale, mix, seq)
        xp, xp_bf = _out_proj(mix, w_o_bf, xp, g1, b1, LN_TM)
        pool_p.append(proj_p.reshape(batch, seq, IN_WIDTH)[:, seq - POOL_BUF:, 4 * RET_WIDTH:])

        mix, new_ret_sample = _ret_sample(proj_s, cos_s, sin_s, state_ret, li, gam_s,
                                          new_ret_sample)
        mix, new_pool_sample = _pool_sample(proj_s, state_pool, li, w_pool_bf, scale, mix,
                                            new_pool_sample)
        xs, xs_bf = _out_proj(mix, w_o_bf, xs, g1, b1, n_s)

        h_p, tail, h_s, new_conv_sample, w_down_bf = _ffn_up(
            xp_bf, xs_bf, w_up, conv_w, conv_b3, state_conv, new_conv_sample, w_down, li, seq)
        xp, xp_bf = _ffn_down(h_p, w_down_bf, xp, g2, b2, LN_TM)
        xs, xs_bf = _ffn_down(h_s, w_down_bf, xs, g2, b2, n_s)
        conv_p.append(tail[:, SUBLANES - (CONV_WIDTH - 1):, :])

    return (xp.reshape(batch, seq, D_MODEL), xs.reshape(n_s, 1, D_MODEL),
            new_ret_prompt, new_ret_sample, jnp.stack(pool_p), new_pool_sample,
            jnp.stack(conv_p), new_conv_sample)
```

```python
import functools

import jax
import jax.numpy as jnp
from jax import lax
from jax.experimental import pallas as pl
from jax.experimental.pallas import tpu as pltpu

F32 = jnp.float32
BF16 = jnp.bfloat16

D_MODEL = 4096
DEPTH = 2
PAST_LEN = 16384
RET_WIDTH = D_MODEL // 2
RET_HEADS = 8
HEAD_DIM = RET_WIDTH // RET_HEADS
HALF_HEAD = HEAD_DIM // 2
RET_CHUNK = 128
ROPE_BASE = 10000.0
POOL_WIDTH = D_MODEL - RET_WIDTH
POOL_WINDOWS = (2, 4, 8, 16)
POOL_GROUP_DIM = POOL_WIDTH // len(POOL_WINDOWS)
POOL_BUF = max(POOL_WINDOWS) - 1
IN_WIDTH = 4 * RET_WIDTH + POOL_WIDTH
D_FF = ((8 * D_MODEL // 3) + 255) // 256 * 256
CONV_WIDTH = 3
DN_ALPHA = (2.0 * DEPTH) ** 0.25
LN_EPS = 1e-5
K_SCALE = HEAD_DIM ** -0.5

V7X_VMEM_BYTES = 64 * 1024 * 1024
VMEM_LIMIT = V7X_VMEM_BYTES - 3 * 1024 * 1024
SUBLANES = 8
HALO_ROWS = 2 * SUBLANES

PROJ_TN = 512
FFN_TN = 256
LN_TM = 512
OUT_TN = 1024
SUB_TN = 512
LN_ROWS = 64
DOWN_TK = D_FF // 2
DOWN_TN = 1024
POOL_TM = 512
SAMPLE_NB = 16
DOT_ROWS = 1024
CAST_ROWS = 64


def _params(*semantics):
    return pltpu.CompilerParams(dimension_semantics=semantics, vmem_limit_bytes=VMEM_LIMIT)


def _layer_norm(y, g, b):
    mu = jnp.mean(y, axis=-1, keepdims=True)
    d = y - mu
    var = jnp.mean(d * d, axis=-1, keepdims=True)
    return d * lax.rsqrt(var + LN_EPS) * g + b


def _ln_copies(acc_ref, ybf_scr, y_hbm, ybf_hbm, sems, row0, r):
    src = pl.ds(pl.multiple_of(r * LN_ROWS, LN_ROWS), LN_ROWS)
    dst = pl.ds(pl.multiple_of(row0 + r * LN_ROWS, LN_ROWS), LN_ROWS)
    return (pltpu.make_async_copy(acc_ref.at[src, :], y_hbm.at[dst, :], sems.at[0, r]),
            pltpu.make_async_copy(ybf_scr.at[src, :], ybf_hbm.at[dst, :], sems.at[1, r]))


def _layer_norm_slab(acc_ref, ybf_scr, g, b, r):
    rows = pl.ds(pl.multiple_of(r * LN_ROWS, LN_ROWS), LN_ROWS)
    y = _layer_norm(acc_ref[rows, :], g, b)
    acc_ref[rows, :] = y
    ybf_scr[rows, :] = y.astype(BF16)


def _layer_norm_start(acc_ref, ybf_scr, g_ref, b_ref, y_hbm, ybf_hbm, sems, row0):
    g = g_ref[...]
    b = b_ref[...]

    def body(r, carry):
        _layer_norm_slab(acc_ref, ybf_scr, g, b, r)
        for c in _ln_copies(acc_ref, ybf_scr, y_hbm, ybf_hbm, sems, row0, r):
            c.start()
        return carry

    lax.fori_loop(0, acc_ref.shape[0] // LN_ROWS, body, 0)


def _layer_norm_wait(acc_ref, ybf_scr, y_hbm, ybf_hbm, sems, row0):
    for r in range(acc_ref.shape[0] // LN_ROWS):
        for c in _ln_copies(acc_ref, ybf_scr, y_hbm, ybf_hbm, sems, row0, r):
            c.wait()


def _ln_scratch(tm):
    return [pltpu.VMEM((tm, D_MODEL), F32),
            pltpu.VMEM((tm, D_MODEL), F32),
            pltpu.VMEM((tm, D_MODEL), BF16),
            pltpu.SemaphoreType.DMA((2, tm // LN_ROWS))]


def _gelu_exact(x):
    return 0.5 * x * (1.0 + lax.erf(x * (0.5 ** 0.5)))


def _proj_kernel(xp_ref, xs_ref, w_ref, wo_ref, op_ref, os_ref, wobf_ref, *, cast_steps):
    i = pl.program_id(0)
    j = pl.program_id(1)
    w = w_ref[...].astype(BF16)
    for r in range(xp_ref.shape[0] // DOT_ROWS):
        rows = pl.ds(r * DOT_ROWS, DOT_ROWS)
        op_ref[rows, :] = jnp.dot(xp_ref[rows, :], w, preferred_element_type=F32)

    @pl.when(i * pl.num_programs(1) + j < cast_steps)
    def _():
        wobf_ref[...] = wo_ref[...].astype(BF16)

    @pl.when(i == 0)
    def _():
        os_ref[...] = jnp.dot(xs_ref[...], w, preferred_element_type=F32)


def _proj(xp_bf, xs_bf, w_in, w_o, li, tm):
    mp = xp_bf.shape[0]
    ns = xs_bf.shape[0]
    nj = IN_WIDTH // PROJ_TN
    cast_steps = D_MODEL // CAST_ROWS
    assert cast_steps <= (mp // tm) * nj
    kern = functools.partial(_proj_kernel, cast_steps=cast_steps)
    whole = lambda a: pl.BlockSpec(a.shape, lambda i, j: (0, 0))
    slab = lambda i, j: jnp.minimum(i * nj + j, cast_steps - 1)
    return pl.pallas_call(
        kern,
        grid=(mp // tm, nj),
        in_specs=[
            pl.BlockSpec((tm, D_MODEL), lambda i, j: (i, 0), pipeline_mode=pl.Buffered(1)),
            whole(xs_bf),
            pl.BlockSpec((None, D_MODEL, PROJ_TN), lambda i, j: (li, 0, j)),
            pl.BlockSpec((None, CAST_ROWS, D_MODEL), lambda i, j: (li, slab(i, j), 0)),
        ],
        out_specs=[
            pl.BlockSpec((tm, PROJ_TN), lambda i, j: (i, j)),
            pl.BlockSpec((ns, PROJ_TN), lambda i, j: (0, jnp.where(i == 0, j, nj - 1))),
            pl.BlockSpec((CAST_ROWS, D_MODEL), lambda i, j: (slab(i, j), 0)),
        ],
        out_shape=[jax.ShapeDtypeStruct((mp, IN_WIDTH), F32),
                   jax.ShapeDtypeStruct((ns, IN_WIDTH), F32),
                   jax.ShapeDtypeStruct((D_MODEL, D_MODEL), BF16)],
        compiler_params=_params("arbitrary", "arbitrary"),
        name="proj",
    )(xp_bf, xs_bf, w_in, w_o)


def _group_norm_gate(o, g):
    mu = jnp.mean(o, axis=-1, keepdims=True)
    d = o - mu
    var = jnp.mean(d * d, axis=-1, keepdims=True)
    return d * lax.rsqrt(var + LN_EPS) * (g * jax.nn.sigmoid(g))


def _rotary(x, cos, sin):
    x1 = x[:, 0:HALF_HEAD]
    x2 = x[:, HALF_HEAD:HEAD_DIM]
    return jnp.concatenate([x1 * cos - x2 * sin, x2 * cos + x1 * sin], axis=-1)


def _ret_prompt_kernel(q_ref, k_ref, v_ref, g_ref, cos_ref, sin_ref, decay_ref, xi_ref, zeta_ref,
                       gl_ref, *rest):
    o_ref, snew_ref, s_scr = rest[-3:]
    c = pl.program_id(1)

    @pl.when(c == 0)
    def _():
        s_scr[...] = jnp.zeros_like(s_scr)

    cos = cos_ref[...]
    sin = sin_ref[...]
    for h in range(RET_HEADS):
        cols = pl.ds(h * HEAD_DIM, HEAD_DIM)
        q = _rotary(q_ref[:, cols], cos, sin)
        k = _rotary(k_ref[:, cols], cos, sin) * K_SCALE
        vb = v_ref[:, cols].astype(BF16)
        s = s_scr[h]
        scores = lax.dot_general(q.astype(BF16), k.astype(BF16), (((1,), (1,)), ((), ())),
                                 preferred_element_type=F32) * decay_ref[h]
        o = jnp.dot(scores.astype(BF16), vb, preferred_element_type=F32)
        o = o + jnp.dot((q * xi_ref[h]).astype(BF16), s.astype(BF16), preferred_element_type=F32)
        kz_t = (k * zeta_ref[h]).T.astype(BF16)
        s_scr[h] = gl_ref[h] * s + jnp.dot(kz_t, vb, preferred_element_type=F32)
        o_ref[:, cols] = _group_norm_gate(o, g_ref[:, cols]).astype(BF16)

    @pl.when(c == pl.num_programs(1) - 1)
    def _():
        snew_ref[...] = s_scr[...]


def _ret_prompt(proj, cos, sin, decay, xi, zeta, gl, li, new_state_all, batch, seq):
    nc = seq // RET_CHUNK

    def col(cb):
        return pl.BlockSpec((RET_CHUNK, RET_WIDTH), lambda b, c: (b * nc + c, cb))

    def whole(a):
        return pl.BlockSpec(a.shape, lambda b, c: (0,) * a.ndim)

    table = pl.BlockSpec((RET_CHUNK, HALF_HEAD), lambda b, c: (c, 0))
    in_specs = [col(0), col(1), col(2), col(3), table, table,
                whole(decay), whole(xi), whole(zeta), whole(gl)]
    args = [proj, proj, proj, proj, cos, sin, decay, xi, zeta, gl]
    aliases = {}
    if new_state_all is not None:
        in_specs.append(pl.BlockSpec(memory_space=pl.ANY))
        args.append(new_state_all)
        aliases = {len(args) - 1: 1}
    return pl.pallas_call(
        _ret_prompt_kernel,
        grid=(batch, nc),
        in_specs=in_specs,
        out_specs=[
            pl.BlockSpec((RET_CHUNK, RET_WIDTH), lambda b, c: (b * nc + c, 0)),
            pl.BlockSpec((None, None, RET_HEADS, HEAD_DIM, HEAD_DIM), lambda b, c: (li, b, 0, 0, 0)),
        ],
        out_shape=[
            jax.ShapeDtypeStruct((batch * seq, D_MODEL), BF16),
            jax.ShapeDtypeStruct((DEPTH, batch, RET_HEADS, HEAD_DIM, HEAD_DIM), F32),
        ],
        input_output_aliases=aliases,
        scratch_shapes=[pltpu.VMEM((RET_HEADS, HEAD_DIM, HEAD_DIM), F32)],
        compiler_params=_params("parallel", "arbitrary"),
        name="ret_prompt",
    )(*args)


def _ret_sample_kernel(q_ref, k_ref, v_ref, g_ref, cos_ref, sin_ref, s0_ref, gam_ref, *rest):
    o_ref, snew_ref = rest[-2:]
    q = _rotary(q_ref[...], cos_ref[...], sin_ref[...])
    k = _rotary(k_ref[...], cos_ref[...], sin_ref[...]) * K_SCALE
    v = v_ref[...]
    vb = v.astype(BF16)
    gam = gam_ref[0]
    qg = (q * gam).astype(BF16)
    rows = lax.broadcasted_iota(jnp.int32, (SAMPLE_NB, 1), 0)
    o_state = jnp.zeros((SAMPLE_NB, HEAD_DIM), F32)
    for n in range(SAMPLE_NB):
        s = s0_ref[n, 0]
        r = jnp.dot(qg, s.astype(BF16), preferred_element_type=F32)
        o_state = jnp.where(rows == n, r, o_state)
        k_n_t = jnp.where(rows == n, k, 0.0).T.astype(BF16)
        snew_ref[n, 0] = gam * s + jnp.dot(k_n_t, vb, preferred_element_type=F32)
    qk = jnp.sum(q * k, axis=-1, keepdims=True)
    o = qk * v + o_state
    o_ref[...] = _group_norm_gate(o, g_ref[...]).astype(BF16)


def _ret_sample(proj, cos, sin, state_all, li, gam, new_state_all):
    n = proj.shape[0]

    def col(cb):
        return pl.BlockSpec((SAMPLE_NB, HEAD_DIM), lambda i, h: (i, cb * RET_HEADS + h))

    state_spec = pl.BlockSpec((None, SAMPLE_NB, 1, HEAD_DIM, HEAD_DIM),
                              lambda i, h: (li, i, h, 0, 0))
    table = pl.BlockSpec((SAMPLE_NB, HALF_HEAD), lambda i, h: (i, 0))
    in_specs = [col(0), col(1), col(2), col(3), table, table, state_spec,
                pl.BlockSpec((1, 1, HEAD_DIM), lambda i, h: (h, 0, 0))]
    args = [proj, proj, proj, proj, cos, sin, state_all, gam]
    aliases = {}
    if new_state_all is not None:
        in_specs.append(pl.BlockSpec(memory_space=pl.ANY))
        args.append(new_state_all)
        aliases = {len(args) - 1: 1}
    return pl.pallas_call(
        _ret_sample_kernel,
        grid=(n // SAMPLE_NB, RET_HEADS),
        in_specs=in_specs,
        out_specs=[pl.BlockSpec((SAMPLE_NB, HEAD_DIM), lambda i, h: (i, h)), state_spec],
        out_shape=[jax.ShapeDtypeStruct((n, D_MODEL), BF16),
                   jax.ShapeDtypeStruct(state_all.shape, F32)],
        input_output_aliases=aliases,
        compiler_params=_params("parallel", "arbitrary"),
        name="ret_sample",
    )(*args)


def _pool_prompt_kernel(u_ref, halo_ref, wp_ref, sc_ref, mix_hbm, o_ref, ext_scr, *,
                        tiles_per_seq):
    del mix_hbm
    i = pl.program_id(0)
    tm = u_ref.shape[0]
    t0 = (i % tiles_per_seq) * tm
    ext_scr[0:HALO_ROWS, :] = jnp.where(t0 == 0, 0.0, halo_ref[...])
    ext_scr[HALO_ROWS:HALO_ROWS + tm, :] = u_ref[...]
    pos = t0 + lax.broadcasted_iota(jnp.int32, (tm, 1), 0)
    for gi, w in enumerate(POOL_WINDOWS):
        cols = pl.ds(gi * POOL_GROUP_DIM, POOL_GROUP_DIM)
        u = u_ref[:, cols]
        s = ext_scr[:, cols]
        shift = 1
        while shift < w:
            s = s + pltpu.roll(s, shift, axis=0)
            shift *= 2
        s = s[HALO_ROWS:, :]
        cnt = jnp.minimum(w, pos + 1).astype(F32)
        pooled = s / cnt - u
        mixed = jnp.dot(pooled.astype(BF16), wp_ref[gi], preferred_element_type=F32)
        o_ref[:, cols] = (mixed * sc_ref[:, cols]).astype(BF16)


def _pool_prompt(proj, w_pool_bf, scale, mix, seq):
    m = proj.shape[0]
    tiles_per_seq = seq // POOL_TM
    u_col = 4 * RET_WIDTH // POOL_WIDTH
    halo_per_tile = POOL_TM // HALO_ROWS
    kern = functools.partial(_pool_prompt_kernel, tiles_per_seq=tiles_per_seq)
    return pl.pallas_call(
        kern,
        grid=(m // POOL_TM,),
        in_specs=[
            pl.BlockSpec((POOL_TM, POOL_WIDTH), lambda i: (i, u_col)),
            pl.BlockSpec((HALO_ROWS, POOL_WIDTH),
                         lambda i: (jnp.maximum(i * halo_per_tile - 1, 0), u_col)),
            pl.BlockSpec(w_pool_bf.shape, lambda i: (0, 0, 0)),
            pl.BlockSpec((1, POOL_WIDTH), lambda i: (0, 0)),
            pl.BlockSpec(memory_space=pl.ANY),
        ],
        out_specs=pl.BlockSpec((POOL_TM, POOL_WIDTH), lambda i: (i, 1)),
        out_shape=jax.ShapeDtypeStruct(mix.shape, BF16),
        input_output_aliases={4: 0},
        scratch_shapes=[pltpu.VMEM((HALO_ROWS + POOL_TM, POOL_WIDTH), F32)],
        compiler_params=_params("parallel"),
        name="pool_prompt",
    )(proj, proj, w_pool_bf, scale, mix)


def _pool_sample_kernel(u_ref, buf_ref, wp_ref, sc_ref, *rest):
    o_ref, nbuf_ref = rest[-2:]
    nbuf_ref[:, 0:POOL_BUF - 1, :] = buf_ref[:, 1:POOL_BUF, :]
    nbuf_ref[:, POOL_BUF - 1, :] = u_ref[...]
    row = lax.broadcasted_iota(jnp.int32, (1, POOL_BUF, 1), 1)
    for gi, w in enumerate(POOL_WINDOWS):
        cols = pl.ds(gi * POOL_GROUP_DIM, POOL_GROUP_DIM)
        u = u_ref[:, cols]
        past = jnp.where(row >= POOL_BUF - (w - 1), buf_ref[:, :, cols], 0.0)
        s = u + jnp.sum(past, axis=1)
        pooled = s / float(w) - u
        mixed = jnp.dot(pooled.astype(BF16), wp_ref[gi], preferred_element_type=F32)
        o_ref[:, cols] = (mixed * sc_ref[:, cols]).astype(BF16)


def _pool_sample(proj, buf_all, li, w_pool_bf, scale, mix, new_buf_all):
    n = proj.shape[0]
    u_col = 4 * RET_WIDTH // POOL_WIDTH
    buf_spec = pl.BlockSpec((None, SAMPLE_NB, POOL_BUF, POOL_WIDTH), lambda i: (li, i, 0, 0))
    in_specs = [
        pl.BlockSpec((SAMPLE_NB, POOL_WIDTH), lambda i: (i, u_col)),
        buf_spec,
        pl.BlockSpec(w_pool_bf.shape, lambda i: (0, 0, 0)),
        pl.BlockSpec((1, POOL_WIDTH), lambda i: (0, 0)),
        pl.BlockSpec(memory_space=pl.ANY),
    ]
    args = [proj, buf_all, w_pool_bf, scale, mix]
    aliases = {4: 0}
    if new_buf_all is not None:
        in_specs.append(pl.BlockSpec(memory_space=pl.ANY))
        args.append(new_buf_all)
        aliases[5] = 1
    return pl.pallas_call(
        _pool_sample_kernel,
        grid=(n // SAMPLE_NB,),
        in_specs=in_specs,
        out_specs=[pl.BlockSpec((SAMPLE_NB, POOL_WIDTH), lambda i: (i, 1)), buf_spec],
        out_shape=[jax.ShapeDtypeStruct(mix.shape, BF16),
                   jax.ShapeDtypeStruct(buf_all.shape, F32)],
        input_output_aliases=aliases,
        compiler_params=_params("parallel"),
        name="pool_sample",
    )(*args)


def _out_proj_kernel(mix_ref, w_ref, x_ref, g_ref, b_ref, y_hbm, ybf_hbm, acc_even, acc_odd,
                     ybf_scr, sems, *, slabs_per_step):
    i = pl.program_id(0)
    j = pl.program_id(1)
    nt = pl.num_programs(0) - 1
    tm = mix_ref.shape[0]
    tn = w_ref.shape[1]
    g = g_ref[...]
    b = b_ref[...]

    @pl.when((i == 0) & (j == 0))
    def _():
        acc_odd[...] = jnp.zeros_like(acc_odd)

    def step(acc, prev):
        @pl.when((j == 0) & (i >= 2))
        def _():
            _layer_norm_wait(acc, ybf_scr, y_hbm, ybf_hbm, sems, (i - 2) * tm)

        @pl.when(i < nt)
        def _():
            for c in range(tn // SUB_TN):
                sub = pl.ds(c * SUB_TN, SUB_TN)
                cols = pl.ds(pl.multiple_of(j * tn + c * SUB_TN, SUB_TN), SUB_TN)
                mix = jnp.dot(mix_ref[...], w_ref[:, sub], preferred_element_type=F32)
                acc[:, cols] = DN_ALPHA * x_ref[:, sub] + mix

            for s in range(slabs_per_step):
                r = j * slabs_per_step + s
                _layer_norm_slab(prev, ybf_scr, g, b, r)

                @pl.when(i > 0)
                def _():
                    for c in _ln_copies(prev, ybf_scr, y_hbm, ybf_hbm, sems, (i - 1) * tm, r):
                        c.start()

        @pl.when((i == nt) & (j == 0))
        def _():
            _layer_norm_start(prev, ybf_scr, g_ref, b_ref, y_hbm, ybf_hbm, sems, (nt - 1) * tm)
            _layer_norm_wait(prev, ybf_scr, y_hbm, ybf_hbm, sems, (nt - 1) * tm)

    @pl.when(i % 2 == 0)
    def _():
        step(acc_even, acc_odd)

    @pl.when(i % 2 == 1)
    def _():
        step(acc_odd, acc_even)


def _out_proj(mix, w_o_bf, x, g, b, tm):
    m = x.shape[0]
    nt = m // tm
    nj = D_MODEL // OUT_TN
    n_slabs = tm // LN_ROWS
    slabs_per_step = n_slabs // nj if nt > 1 else 0
    assert nt == 1 or slabs_per_step * nj == n_slabs
    row = lambda i: jnp.minimum(i, nt - 1)
    col = lambda i, j: jnp.where(i < nt, j, nj - 1)
    kern = functools.partial(_out_proj_kernel, slabs_per_step=slabs_per_step)
    return pl.pallas_call(
        kern,
        grid=(nt + 1, nj),
        in_specs=[
            pl.BlockSpec((tm, D_MODEL), lambda i, j: (row(i), 0)),
            pl.BlockSpec((D_MODEL, OUT_TN), lambda i, j: (0, col(i, j))),
            pl.BlockSpec((tm, OUT_TN), lambda i, j: (row(i), col(i, j))),
            pl.BlockSpec((1, D_MODEL), lambda i, j: (0, 0)),
            pl.BlockSpec((1, D_MODEL), lambda i, j: (0, 0)),
        ],
        out_specs=[pl.BlockSpec(memory_space=pl.ANY), pl.BlockSpec(memory_space=pl.ANY)],
        out_shape=[jax.ShapeDtypeStruct((m, D_MODEL), F32),
                   jax.ShapeDtypeStruct((m, D_MODEL), BF16)],
        scratch_shapes=_ln_scratch(tm),
        compiler_params=_params("arbitrary", "arbitrary"),
        name="out_proj",
    )(mix, w_o_bf, x, g, b)


def _shift_rows(x, carry_row, first8):
    rolled = pltpu.roll(x, 1, axis=0)
    fill = 0.0 if carry_row is None else carry_row
    head = jnp.where(first8 == 0, fill, rolled[0:SUBLANES, :])
    return jnp.concatenate([head, rolled[SUBLANES:, :]], axis=0)


def _ffn_up_kernel(xp_ref, xs_ref, wa_ref, wb_ref, cw_ref, cb_ref, sc_ref, wd_ref, *rest):
    hp_ref, tail_ref, hs_ref, nsc_ref, wdbf_ref, w_scr = rest[-6:]
    i = pl.program_id(0)
    tm = xp_ref.shape[0]
    tn = wa_ref.shape[1]
    w_scr[:, 0:tn] = wa_ref[...].astype(BF16)
    w_scr[:, tn:2 * tn] = wb_ref[...].astype(BF16)
    w = w_scr[...]
    cw0 = cw_ref[0:1, :]
    cw1 = cw_ref[1:2, :]
    cw2 = cw_ref[2:3, :]
    cb = cb_ref[...]

    wdbf_ref[...] = wd_ref[...].astype(BF16)

    first8 = lax.broadcasted_iota(jnp.int32, (SUBLANES, tn), 0)
    p_last = None
    q_last = None
    for r in range(tm // DOT_ROWS):
        r0 = r * DOT_ROWS
        ab = jnp.dot(xp_ref[pl.ds(r0, DOT_ROWS), :], w, preferred_element_type=F32)
        a = ab[:, 0:tn]
        p = a * cw0
        q = a * cw1 + _shift_rows(p, p_last, first8)
        conv = (cb + a * cw2) + _shift_rows(q, q_last, first8)
        p_last = p[DOT_ROWS - 1:DOT_ROWS, :]
        q_last = q[DOT_ROWS - 1:DOT_ROWS, :]
        hp_ref[pl.ds(r0, DOT_ROWS), :] = (_gelu_exact(conv) * ab[:, tn:2 * tn]).astype(BF16)
    tail_ref[0] = a[DOT_ROWS - SUBLANES:DOT_ROWS, :]

    @pl.when(i == 0)
    def _():
        ab_s = jnp.dot(xs_ref[...], w_scr[...], preferred_element_type=F32)
        a_s = ab_s[:, 0:tn]
        b_s = ab_s[:, tn:2 * tn]
        s1 = sc_ref[:, 1, :]
        conv_s = cb + sc_ref[:, 0, :] * cw0
        conv_s = conv_s + s1 * cw1
        conv_s = conv_s + a_s * cw2
        hs_ref[...] = (_gelu_exact(conv_s) * b_s).astype(BF16)
        nsc_ref[:, 0, :] = s1
        nsc_ref[:, 1, :] = a_s


def _ffn_up(xp_bf, xs_bf, w_up, conv_w, conv_b, state_all, new_state_all, w_down, li, seq):
    mp = xp_bf.shape[0]
    ns = xs_bf.shape[0]
    nj = D_FF // FFN_TN
    tm = seq
    assert (mp // tm) * nj * CAST_ROWS == D_FF
    first = lambda i, j: jnp.where(i == 0, j, nj - 1)
    state_spec = pl.BlockSpec((None, ns, CONV_WIDTH - 1, FFN_TN),
                              lambda i, j: (li, 0, 0, first(i, j)))
    in_specs = [
        pl.BlockSpec((tm, D_MODEL), lambda i, j: (i, 0), pipeline_mode=pl.Buffered(1)),
        pl.BlockSpec((ns, D_MODEL), lambda i, j: (0, 0)),
        pl.BlockSpec((None, D_MODEL, FFN_TN), lambda i, j: (li, 0, j)),
        pl.BlockSpec((None, D_MODEL, FFN_TN), lambda i, j: (li, 0, nj + j)),
        pl.BlockSpec((None, CONV_WIDTH, FFN_TN), lambda i, j: (li, 0, j)),
        pl.BlockSpec((None, 1, FFN_TN), lambda i, j: (li, 0, j)),
        state_spec,
        pl.BlockSpec((None, CAST_ROWS, D_MODEL), lambda i, j: (li, i * nj + j, 0)),
    ]
    args = [xp_bf, xs_bf, w_up, w_up, conv_w, conv_b, state_all, w_down]
    aliases = {}
    if new_state_all is not None:
        in_specs.append(pl.BlockSpec(memory_space=pl.ANY))
        args.append(new_state_all)
        aliases = {len(args) - 1: 3}
    return pl.pallas_call(
        _ffn_up_kernel,
        grid=(mp // tm, nj),
        in_specs=in_specs,
        out_specs=[
            pl.BlockSpec((tm, FFN_TN), lambda i, j: (i, j)),
            pl.BlockSpec((1, SUBLANES, FFN_TN), lambda i, j: (i, 0, j)),
            pl.BlockSpec((ns, FFN_TN), lambda i, j: (0, first(i, j))),
            state_spec,
            pl.BlockSpec((CAST_ROWS, D_MODEL), lambda i, j: (i * nj + j, 0)),
        ],
        out_shape=[jax.ShapeDtypeStruct((mp, D_FF), BF16),
                   jax.ShapeDtypeStruct((mp // tm, SUBLANES, D_FF), F32),
                   jax.ShapeDtypeStruct((ns, D_FF), BF16),
                   jax.ShapeDtypeStruct(state_all.shape, F32),
                   jax.ShapeDtypeStruct((D_FF, D_MODEL), BF16)],
        input_output_aliases=aliases,
        scratch_shapes=[pltpu.VMEM((D_MODEL, 2 * FFN_TN), BF16)],
        compiler_params=_params("arbitrary", "arbitrary"),
        name="ffn_up",
    )(*args)


def _ffn_down_kernel(h_ref, w_ref, x_ref, g_ref, b_ref, y_hbm, ybf_hbm, acc_even, acc_odd,
                     ybf_scr, sems, *, slabs_per_step):
    i = pl.program_id(0)
    k = pl.program_id(1)
    j = pl.program_id(2)
    nt = pl.num_programs(0) - 1
    nj = pl.num_programs(2)
    tm = h_ref.shape[0]
    tn = w_ref.shape[1]
    first = (k == 0) & (j == 0)
    g = g_ref[...]
    b = b_ref[...]

    @pl.when(first & (i == 0))
    def _():
        acc_even[...] = jnp.zeros_like(acc_even)
        acc_odd[...] = jnp.zeros_like(acc_odd)

    def step(acc, prev):
        @pl.when(first & (i >= 2))
        def _():
            _layer_norm_wait(acc, ybf_scr, y_hbm, ybf_hbm, sems, (i - 2) * tm)

        @pl.when(i < nt)
        def _():
            for c in range(tn // SUB_TN):
                sub = pl.ds(c * SUB_TN, SUB_TN)
                cols = pl.ds(pl.multiple_of(j * tn + c * SUB_TN, SUB_TN), SUB_TN)
                part = jnp.dot(h_ref[...], w_ref[:, sub], preferred_element_type=F32)
                base = jnp.where(k == 0, DN_ALPHA * x_ref[:, sub], acc[:, cols])
                acc[:, cols] = base + part

            for s in range(slabs_per_step):
                r = (k * nj + j) * slabs_per_step + s
                _layer_norm_slab(prev, ybf_scr, g, b, r)

                @pl.when(i > 0)
                def _():
                    for c in _ln_copies(prev, ybf_scr, y_hbm, ybf_hbm, sems, (i - 1) * tm, r):
                        c.start()

        @pl.when((i == nt) & first)
        def _():
            _layer_norm_start(prev, ybf_scr, g_ref, b_ref, y_hbm, ybf_hbm, sems, (nt - 1) * tm)
            _layer_norm_wait(prev, ybf_scr, y_hbm, ybf_hbm, sems, (nt - 1) * tm)

    @pl.when(i % 2 == 0)
    def _():
        step(acc_even, acc_odd)

    @pl.when(i % 2 == 1)
    def _():
        step(acc_odd, acc_even)


def _ffn_down(h, w_down_bf, x, g, b, tm):
    m = x.shape[0]
    nt = m // tm
    nk = D_FF // DOWN_TK
    nj = D_MODEL // DOWN_TN
    n_slabs = tm // LN_ROWS
    slabs_per_step = n_slabs // (nk * nj) if nt > 1 else 0
    assert nt == 1 or slabs_per_step * nk * nj == n_slabs
    row = lambda i: jnp.minimum(i, nt - 1)
    kk = lambda i, k: jnp.where(i < nt, k, nk - 1)
    jj = lambda i, j: jnp.where(i < nt, j, nj - 1)
    kern = functools.partial(_ffn_down_kernel, slabs_per_step=slabs_per_step)
    return pl.pallas_call(
        kern,
        grid=(nt + 1, nk, nj),
        in_specs=[
            pl.BlockSpec((tm, DOWN_TK), lambda i, k, j: (row(i), kk(i, k))),
            pl.BlockSpec((DOWN_TK, DOWN_TN), lambda i, k, j: (kk(i, k), jj(i, j))),
            pl.BlockSpec((tm, DOWN_TN),
                         lambda i, k, j: (row(i), jnp.where((k == 0) & (i < nt), j, nj - 1))),
            pl.BlockSpec((1, D_MODEL), lambda i, k, j: (0, 0)),
            pl.BlockSpec((1, D_MODEL), lambda i, k, j: (0, 0)),
        ],
        out_specs=[pl.BlockSpec(memory_space=pl.ANY), pl.BlockSpec(memory_space=pl.ANY)],
        out_shape=[jax.ShapeDtypeStruct((m, D_MODEL), F32),
                   jax.ShapeDtypeStruct((m, D_MODEL), BF16)],
        scratch_shapes=_ln_scratch(tm),
        compiler_params=_params("arbitrary", "arbitrary", "arbitrary"),
        name="ffn_down",
    )(h, w_down_bf, x, g, b)


def _rotary_tables(pos):
    inv = ROPE_BASE ** (-jnp.arange(HALF_HEAD, dtype=F32) / HALF_HEAD)
    ang = pos.astype(F32)[:, None] * inv[None, :]
    return jnp.cos(ang), jnp.sin(ang)


def _retention_tables(l):
    log_g = jnp.log1p(-(2.0 ** (-5.0 - jnp.arange(RET_HEADS, dtype=F32))))
    i = jnp.arange(l)
    diff = i[:, None] - i[None, :]
    decay = jnp.where(diff[None] >= 0,
                      jnp.exp(jnp.maximum(diff, 0)[None].astype(F32) * log_g[:, None, None]), 0.0)
    xi = jnp.exp((i + 1)[None].astype(F32) * log_g[:, None])
    zeta = jnp.exp((l - 1 - i)[None].astype(F32) * log_g[:, None])
    g_l = jnp.exp(l * log_g)
    return decay, xi, zeta, g_l


def kernel(x_prompt, x_sample, state_ret, state_pool, state_conv, w_in, w_pool, pool_scale, w_o,
           ln1_g, ln1_b, w_up, conv_w, conv_b, w_down, ln2_g, ln2_b):
    batch, seq, _ = x_prompt.shape
    n_s = x_sample.shape[0]
    assert x_sample.shape[1] == 1 and seq % RET_CHUNK == 0
    mp = batch * seq

    cos_p, sin_p = _rotary_tables(jnp.arange(seq))
    cos_s, sin_s = _rotary_tables(PAST_LEN + jnp.arange(1))
    cos_s = jnp.broadcast_to(cos_s, (n_s, HALF_HEAD))
    sin_s = jnp.broadcast_to(sin_s, (n_s, HALF_HEAD))
    decay, xi, zeta, g_l = _retention_tables(RET_CHUNK)
    xi = xi[:, :, None]
    zeta = zeta[:, :, None]
    gl_b = jnp.broadcast_to(g_l[:, None, None], (RET_HEADS, 1, HEAD_DIM))
    _, xi_s, _, _ = _retention_tables(1)
    gam_s = jnp.broadcast_to(xi_s[:, :, None], (RET_HEADS, 1, HEAD_DIM))

    xp = x_prompt.reshape(mp, D_MODEL)
    xs = x_sample.reshape(n_s, D_MODEL)
    xp_bf = xp.astype(BF16)
    xs_bf = xs.astype(BF16)
    conv_b3 = conv_b[:, None, :]

    pool_p, conv_p = [], []
    new_ret_prompt = new_ret_sample = new_pool_sample = new_conv_sample = None
    for li in range(DEPTH):
        w_pool_bf = w_pool[li].astype(BF16)
        scale = pool_scale[li][None, :]
        g1, b1 = ln1_g[li][None, :], ln1_b[li][None, :]
        g2, b2 = ln2_g[li][None, :], ln2_b[li][None, :]

        proj_p, proj_s, w_o_bf = _proj(xp_bf, xs_bf, w_in, w_o, li, seq)

        mix, new_ret_prompt = _ret_prompt(proj_p, cos_p, sin_p, decay, xi, zeta, gl_b, li,
                                          new_ret_prompt, batch, seq)
        mix = _pool_prompt(proj_p, w_pool_bf, scale, mix, seq)
        xp, xp_bf = _out_proj(mix, w_o_bf, xp, g1, b1, LN_TM)
        pool_p.append(proj_p.reshape(batch, seq, IN_WIDTH)[:, seq - POOL_BUF:, 4 * RET_WIDTH:])

        mix, new_ret_sample = _ret_sample(proj_s, cos_s, sin_s, state_ret, li, gam_s,
                                          new_ret_sample)
        mix, new_pool_sample = _pool_sample(proj_s, state_pool, li, w_pool_bf, scale, mix,
                                            new_pool_sample)
        xs, xs_bf = _out_proj(mix, w_o_bf, xs, g1, b1, n_s)

        h_p, tail, h_s, new_conv_sample, w_down_bf = _ffn_up(
            xp_bf, xs_bf, w_up, conv_w, conv_b3, state_conv, new_conv_sample, w_down, li, seq)
        xp, xp_bf = _ffn_down(h_p, w_down_bf, xp, g2, b2, LN_TM)
        xs, xs_bf = _ffn_down(h_s, w_down_bf, xs, g2, b2, n_s)
        conv_p.append(tail[:, SUBLANES - (CONV_WIDTH - 1):, :])

    return (xp.reshape(batch, seq, D_MODEL), xs.reshape(n_s, 1, D_MODEL),
            new_ret_prompt, new_ret_sample, jnp.stack(pool_p), new_pool_sample,
            jnp.stack(conv_p), new_conv_sample)
```

```python
import functools

import jax
import jax.numpy as jnp
from jax import lax
from jax.experimental import pallas as pl
from jax.experimental.pallas import tpu as pltpu

F32 = jnp.float32
BF16 = jnp.bfloat16

D_MODEL = 4096
DEPTH = 2
PAST_LEN = 16384
RET_WIDTH = D_MODEL // 2
RET_HEADS = 8
HEAD_DIM = RET_WIDTH // RET_HEADS
HALF_HEAD = HEAD_DIM // 2
RET_CHUNK = 128
ROPE_BASE = 10000.0
POOL_WIDTH = D_MODEL - RET_WIDTH
POOL_WINDOWS = (2, 4, 8, 16)
POOL_GROUP_DIM = POOL_WIDTH // len(POOL_WINDOWS)
POOL_BUF = max(POOL_WINDOWS) - 1
IN_WIDTH = 4 * RET_WIDTH + POOL_WIDTH
D_FF = ((8 * D_MODEL // 3) + 255) // 256 * 256
CONV_WIDTH = 3
DN_ALPHA = (2.0 * DEPTH) ** 0.25
LN_EPS = 1e-5
K_SCALE = HEAD_DIM ** -0.5

V7X_VMEM_BYTES = 64 * 1024 * 1024
VMEM_LIMIT = V7X_VMEM_BYTES - 3 * 1024 * 1024
SUBLANES = 8
HALO_ROWS = 2 * SUBLANES

PROJ_TN = 512
FFN_TN = 256
LN_TM = 512
OUT_TN = 1024
SUB_TN = 512
LN_ROWS = 64
DOWN_TK = D_FF // 2
DOWN_TN = 1024
POOL_TM = 512
RET_STEP_CHUNKS = 2
SAMPLE_NB = 16
DOT_ROWS = 1024
CAST_ROWS = 64


def _params(*semantics):
    return pltpu.CompilerParams(dimension_semantics=semantics, vmem_limit_bytes=VMEM_LIMIT)


def _layer_norm(y, g, b):
    mu = jnp.mean(y, axis=-1, keepdims=True)
    d = y - mu
    var = jnp.mean(d * d, axis=-1, keepdims=True)
    return d * lax.rsqrt(var + LN_EPS) * g + b


def _ln_copies(acc_ref, ybf_scr, y_hbm, ybf_hbm, sems, row0, r):
    src = pl.ds(pl.multiple_of(r * LN_ROWS, LN_ROWS), LN_ROWS)
    dst = pl.ds(pl.multiple_of(row0 + r * LN_ROWS, LN_ROWS), LN_ROWS)
    return (pltpu.make_async_copy(acc_ref.at[src, :], y_hbm.at[dst, :], sems.at[0, r]),
            pltpu.make_async_copy(ybf_scr.at[src, :], ybf_hbm.at[dst, :], sems.at[1, r]))


def _layer_norm_slab(acc_ref, ybf_scr, g, b, r):
    rows = pl.ds(pl.multiple_of(r * LN_ROWS, LN_ROWS), LN_ROWS)
    y = _layer_norm(acc_ref[rows, :], g, b)
    acc_ref[rows, :] = y
    ybf_scr[rows, :] = y.astype(BF16)


def _layer_norm_start(acc_ref, ybf_scr, g_ref, b_ref, y_hbm, ybf_hbm, sems, row0):
    g = g_ref[...]
    b = b_ref[...]

    def body(r, carry):
        _layer_norm_slab(acc_ref, ybf_scr, g, b, r)
        for c in _ln_copies(acc_ref, ybf_scr, y_hbm, ybf_hbm, sems, row0, r):
            c.start()
        return carry

    lax.fori_loop(0, acc_ref.shape[0] // LN_ROWS, body, 0)


def _layer_norm_wait(acc_ref, ybf_scr, y_hbm, ybf_hbm, sems, row0):
    for r in range(acc_ref.shape[0] // LN_ROWS):
        for c in _ln_copies(acc_ref, ybf_scr, y_hbm, ybf_hbm, sems, row0, r):
            c.wait()


def _ln_scratch(tm):
    return [pltpu.VMEM((tm, D_MODEL), F32),
            pltpu.VMEM((tm, D_MODEL), F32),
            pltpu.VMEM((tm, D_MODEL), BF16),
            pltpu.SemaphoreType.DMA((2, tm // LN_ROWS))]


def _gelu_exact(x):
    return 0.5 * x * (1.0 + lax.erf(x * (0.5 ** 0.5)))


def _proj_kernel(xp_ref, xs_ref, w_ref, wo_ref, op_ref, os_ref, wobf_ref, *, cast_steps):
    i = pl.program_id(0)
    j = pl.program_id(1)
    w = w_ref[...].astype(BF16)
    for r in range(xp_ref.shape[0] // DOT_ROWS):
        rows = pl.ds(r * DOT_ROWS, DOT_ROWS)
        op_ref[rows, :] = jnp.dot(xp_ref[rows, :], w, preferred_element_type=F32)

    @pl.when(i * pl.num_programs(1) + j < cast_steps)
    def _():
        wobf_ref[...] = wo_ref[...].astype(BF16)

    @pl.when(i == 0)
    def _():
        os_ref[...] = jnp.dot(xs_ref[...], w, preferred_element_type=F32)


def _proj(xp_bf, xs_bf, w_in, w_o, li, tm):
    mp = xp_bf.shape[0]
    ns = xs_bf.shape[0]
    nj = IN_WIDTH // PROJ_TN
    cast_steps = D_MODEL // CAST_ROWS
    assert cast_steps <= (mp // tm) * nj
    kern = functools.partial(_proj_kernel, cast_steps=cast_steps)
    whole = lambda a: pl.BlockSpec(a.shape, lambda i, j: (0, 0))
    slab = lambda i, j: jnp.minimum(i * nj + j, cast_steps - 1)
    return pl.pallas_call(
        kern,
        grid=(mp // tm, nj),
        in_specs=[
            pl.BlockSpec((tm, D_MODEL), lambda i, j: (i, 0), pipeline_mode=pl.Buffered(1)),
            whole(xs_bf),
            pl.BlockSpec((None, D_MODEL, PROJ_TN), lambda i, j: (li, 0, j)),
            pl.BlockSpec((None, CAST_ROWS, D_MODEL), lambda i, j: (li, slab(i, j), 0)),
        ],
        out_specs=[
            pl.BlockSpec((tm, PROJ_TN), lambda i, j: (i, j)),
            pl.BlockSpec((ns, PROJ_TN), lambda i, j: (0, jnp.where(i == 0, j, nj - 1))),
            pl.BlockSpec((CAST_ROWS, D_MODEL), lambda i, j: (slab(i, j), 0)),
        ],
        out_shape=[jax.ShapeDtypeStruct((mp, IN_WIDTH), F32),
                   jax.ShapeDtypeStruct((ns, IN_WIDTH), F32),
                   jax.ShapeDtypeStruct((D_MODEL, D_MODEL), BF16)],
        compiler_params=_params("arbitrary", "arbitrary"),
        name="proj",
    )(xp_bf, xs_bf, w_in, w_o)


def _group_norm_gate(o, g):
    mu = jnp.mean(o, axis=-1, keepdims=True)
    d = o - mu
    var = jnp.mean(d * d, axis=-1, keepdims=True)
    return d * lax.rsqrt(var + LN_EPS) * (g * jax.nn.sigmoid(g))


def _rotary(x, cos, sin):
    x1 = x[:, 0:HALF_HEAD]
    x2 = x[:, HALF_HEAD:HEAD_DIM]
    return jnp.concatenate([x1 * cos - x2 * sin, x2 * cos + x1 * sin], axis=-1)


def _ret_prompt_kernel(q_ref, k_ref, v_ref, g_ref, cos_ref, sin_ref, decay_ref, xi_ref, zeta_ref,
                       gl_ref, *rest):
    o_ref, snew_ref, s_scr = rest[-3:]
    c = pl.program_id(1)

    @pl.when(c == 0)
    def _():
        s_scr[...] = jnp.zeros_like(s_scr)

    for cc in range(q_ref.shape[0] // RET_CHUNK):
        rows = pl.ds(cc * RET_CHUNK, RET_CHUNK)
        cos = cos_ref[rows, :]
        sin = sin_ref[rows, :]
        for h in range(RET_HEADS):
            cols = pl.ds(h * HEAD_DIM, HEAD_DIM)
            q = _rotary(q_ref[rows, cols], cos, sin)
            k = _rotary(k_ref[rows, cols], cos, sin) * K_SCALE
            vb = v_ref[rows, cols].astype(BF16)
            s = s_scr[h]
            scores = lax.dot_general(q.astype(BF16), k.astype(BF16), (((1,), (1,)), ((), ())),
                                     preferred_element_type=F32) * decay_ref[h]
            o = jnp.dot(scores.astype(BF16), vb, preferred_element_type=F32)
            o = o + jnp.dot((q * xi_ref[h]).astype(BF16), s.astype(BF16),
                            preferred_element_type=F32)
            kz_t = (k * zeta_ref[h]).T.astype(BF16)
            s_scr[h] = gl_ref[h] * s + jnp.dot(kz_t, vb, preferred_element_type=F32)
            o_ref[rows, cols] = _group_norm_gate(o, g_ref[rows, cols]).astype(BF16)

    @pl.when(c == pl.num_programs(1) - 1)
    def _():
        snew_ref[...] = s_scr[...]


def _ret_prompt(proj, cos, sin, decay, xi, zeta, gl, li, new_state_all, batch, seq):
    rows = RET_STEP_CHUNKS * RET_CHUNK
    nc = seq // rows

    def col(cb):
        return pl.BlockSpec((rows, RET_WIDTH), lambda b, c: (b * nc + c, cb))

    def whole(a):
        return pl.BlockSpec(a.shape, lambda b, c: (0,) * a.ndim)

    table = pl.BlockSpec((rows, HALF_HEAD), lambda b, c: (c, 0))
    in_specs = [col(0), col(1), col(2), col(3), table, table,
                whole(decay), whole(xi), whole(zeta), whole(gl)]
    args = [proj, proj, proj, proj, cos, sin, decay, xi, zeta, gl]
    aliases = {}
    if new_state_all is not None:
        in_specs.append(pl.BlockSpec(memory_space=pl.ANY))
        args.append(new_state_all)
        aliases = {len(args) - 1: 1}
    return pl.pallas_call(
        _ret_prompt_kernel,
        grid=(batch, nc),
        in_specs=in_specs,
        out_specs=[
            pl.BlockSpec((rows, RET_WIDTH), lambda b, c: (b * nc + c, 0)),
            pl.BlockSpec((None, None, RET_HEADS, HEAD_DIM, HEAD_DIM), lambda b, c: (li, b, 0, 0, 0)),
        ],
        out_shape=[
            jax.ShapeDtypeStruct((batch * seq, D_MODEL), BF16),
            jax.ShapeDtypeStruct((DEPTH, batch, RET_HEADS, HEAD_DIM, HEAD_DIM), F32),
        ],
        input_output_aliases=aliases,
        scratch_shapes=[pltpu.VMEM((RET_HEADS, HEAD_DIM, HEAD_DIM), F32)],
        compiler_params=_params("parallel", "arbitrary"),
        name="ret_prompt",
    )(*args)


def _ret_sample_kernel(q_ref, k_ref, v_ref, g_ref, cos_ref, sin_ref, s0_ref, gam_ref, *rest):
    o_ref, snew_ref = rest[-2:]
    q = _rotary(q_ref[...], cos_ref[...], sin_ref[...])
    k = _rotary(k_ref[...], cos_ref[...], sin_ref[...]) * K_SCALE
    v = v_ref[...]
    vb = v.astype(BF16)
    gam = gam_ref[0]
    qg = (q * gam).astype(BF16)
    rows = lax.broadcasted_iota(jnp.int32, (SAMPLE_NB, 1), 0)
    o_state = jnp.zeros((SAMPLE_NB, HEAD_DIM), F32)
    for n in range(SAMPLE_NB):
        s = s0_ref[n, 0]
        r = jnp.dot(qg, s.astype(BF16), preferred_element_type=F32)
        o_state = jnp.where(rows == n, r, o_state)
        k_n_t = jnp.where(rows == n, k, 0.0).T.astype(BF16)
        snew_ref[n, 0] = gam * s + jnp.dot(k_n_t, vb, preferred_element_type=F32)
    qk = jnp.sum(q * k, axis=-1, keepdims=True)
    o = qk * v + o_state
    o_ref[...] = _group_norm_gate(o, g_ref[...]).astype(BF16)


def _ret_sample(proj, cos, sin, state_all, li, gam, new_state_all):
    n = proj.shape[0]

    def col(cb):
        return pl.BlockSpec((SAMPLE_NB, HEAD_DIM), lambda i, h: (i, cb * RET_HEADS + h))

    state_spec = pl.BlockSpec((None, SAMPLE_NB, 1, HEAD_DIM, HEAD_DIM),
                              lambda i, h: (li, i, h, 0, 0))
    table = pl.BlockSpec((SAMPLE_NB, HALF_HEAD), lambda i, h: (i, 0))
    in_specs = [col(0), col(1), col(2), col(3), table, table, state_spec,
                pl.BlockSpec((1, 1, HEAD_DIM), lambda i, h: (h, 0, 0))]
    args = [proj, proj, proj, proj, cos, sin, state_all, gam]
    aliases = {}
    if new_state_all is not None:
        in_specs.append(pl.BlockSpec(memory_space=pl.ANY))
        args.append(new_state_all)
        aliases = {len(args) - 1: 1}
    return pl.pallas_call(
        _ret_sample_kernel,
        grid=(n // SAMPLE_NB, RET_HEADS),
        in_specs=in_specs,
        out_specs=[pl.BlockSpec((SAMPLE_NB, HEAD_DIM), lambda i, h: (i, h)), state_spec],
        out_shape=[jax.ShapeDtypeStruct((n, D_MODEL), BF16),
                   jax.ShapeDtypeStruct(state_all.shape, F32)],
        input_output_aliases=aliases,
        compiler_params=_params("parallel", "arbitrary"),
        name="ret_sample",
    )(*args)


def _pool_prompt_kernel(u_ref, halo_ref, wp_ref, sc_ref, mix_hbm, o_ref, ext_scr, *,
                        tiles_per_seq):
    del mix_hbm
    i = pl.program_id(0)
    tm = u_ref.shape[0]
    t0 = (i % tiles_per_seq) * tm
    ext_scr[0:HALO_ROWS, :] = jnp.where(t0 == 0, 0.0, halo_ref[...])
    ext_scr[HALO_ROWS:HALO_ROWS + tm, :] = u_ref[...]
    pos = t0 + lax.broadcasted_iota(jnp.int32, (tm, 1), 0)
    for gi, w in enumerate(POOL_WINDOWS):
        cols = pl.ds(gi * POOL_GROUP_DIM, POOL_GROUP_DIM)
        u = u_ref[:, cols]
        s = ext_scr[:, cols]
        shift = 1
        while shift < w:
            s = s + pltpu.roll(s, shift, axis=0)
            shift *= 2
        s = s[HALO_ROWS:, :]
        cnt = jnp.minimum(w, pos + 1).astype(F32)
        pooled = s / cnt - u
        mixed = jnp.dot(pooled.astype(BF16), wp_ref[gi], preferred_element_type=F32)
        o_ref[:, cols] = (mixed * sc_ref[:, cols]).astype(BF16)


def _pool_prompt(proj, w_pool_bf, scale, mix, seq):
    m = proj.shape[0]
    tiles_per_seq = seq // POOL_TM
    u_col = 4 * RET_WIDTH // POOL_WIDTH
    halo_per_tile = POOL_TM // HALO_ROWS
    kern = functools.partial(_pool_prompt_kernel, tiles_per_seq=tiles_per_seq)
    return pl.pallas_call(
        kern,
        grid=(m // POOL_TM,),
        in_specs=[
            pl.BlockSpec((POOL_TM, POOL_WIDTH), lambda i: (i, u_col)),
            pl.BlockSpec((HALO_ROWS, POOL_WIDTH),
                         lambda i: (jnp.maximum(i * halo_per_tile - 1, 0), u_col)),
            pl.BlockSpec(w_pool_bf.shape, lambda i: (0, 0, 0)),
            pl.BlockSpec((1, POOL_WIDTH), lambda i: (0, 0)),
            pl.BlockSpec(memory_space=pl.ANY),
        ],
        out_specs=pl.BlockSpec((POOL_TM, POOL_WIDTH), lambda i: (i, 1)),
        out_shape=jax.ShapeDtypeStruct(mix.shape, BF16),
        input_output_aliases={4: 0},
        scratch_shapes=[pltpu.VMEM((HALO_ROWS + POOL_TM, POOL_WIDTH), F32)],
        compiler_params=_params("parallel"),
        name="pool_prompt",
    )(proj, proj, w_pool_bf, scale, mix)


def _pool_sample_kernel(u_ref, buf_ref, wp_ref, sc_ref, *rest):
    o_ref, nbuf_ref = rest[-2:]
    nbuf_ref[:, 0:POOL_BUF - 1, :] = buf_ref[:, 1:POOL_BUF, :]
    nbuf_ref[:, POOL_BUF - 1, :] = u_ref[...]
    row = lax.broadcasted_iota(jnp.int32, (1, POOL_BUF, 1), 1)
    for gi, w in enumerate(POOL_WINDOWS):
        cols = pl.ds(gi * POOL_GROUP_DIM, POOL_GROUP_DIM)
        u = u_ref[:, cols]
        past = jnp.where(row >= POOL_BUF - (w - 1), buf_ref[:, :, cols], 0.0)
        s = u + jnp.sum(past, axis=1)
        pooled = s / float(w) - u
        mixed = jnp.dot(pooled.astype(BF16), wp_ref[gi], preferred_element_type=F32)
        o_ref[:, cols] = (mixed * sc_ref[:, cols]).astype(BF16)


def _pool_sample(proj, buf_all, li, w_pool_bf, scale, mix, new_buf_all):
    n = proj.shape[0]
    u_col = 4 * RET_WIDTH // POOL_WIDTH
    buf_spec = pl.BlockSpec((None, SAMPLE_NB, POOL_BUF, POOL_WIDTH), lambda i: (li, i, 0, 0))
    in_specs = [
        pl.BlockSpec((SAMPLE_NB, POOL_WIDTH), lambda i: (i, u_col)),
        buf_spec,
        pl.BlockSpec(w_pool_bf.shape, lambda i: (0, 0, 0)),
        pl.BlockSpec((1, POOL_WIDTH), lambda i: (0, 0)),
        pl.BlockSpec(memory_space=pl.ANY),
    ]
    args = [proj, buf_all, w_pool_bf, scale, mix]
    aliases = {4: 0}
    if new_buf_all is not None:
        in_specs.append(pl.BlockSpec(memory_space=pl.ANY))
        args.append(new_buf_all)
        aliases[5] = 1
    return pl.pallas_call(
        _pool_sample_kernel,
        grid=(n // SAMPLE_NB,),
        in_specs=in_specs,
        out_specs=[pl.BlockSpec((SAMPLE_NB, POOL_WIDTH), lambda i: (i, 1)), buf_spec],
        out_shape=[jax.ShapeDtypeStruct(mix.shape, BF16),
                   jax.ShapeDtypeStruct(buf_all.shape, F32)],
        input_output_aliases=aliases,
        compiler_params=_params("parallel"),
        name="pool_sample",
    )(*args)


def _out_proj_kernel(mix_ref, w_ref, x_ref, g_ref, b_ref, y_hbm, ybf_hbm, acc_even, acc_odd,
                     ybf_scr, sems, *, slabs_per_step):
    i = pl.program_id(0)
    j = pl.program_id(1)
    nt = pl.num_programs(0) - 1
    tm = mix_ref.shape[0]
    tn = w_ref.shape[1]
    g = g_ref[...]
    b = b_ref[...]

    @pl.when((i == 0) & (j == 0))
    def _():
        acc_odd[...] = jnp.zeros_like(acc_odd)

    def step(acc, prev):
        @pl.when((j == 0) & (i >= 2))
        def _():
            _layer_norm_wait(acc, ybf_scr, y_hbm, ybf_hbm, sems, (i - 2) * tm)

        @pl.when(i < nt)
        def _():
            for c in range(tn // SUB_TN):
                sub = pl.ds(c * SUB_TN, SUB_TN)
                cols = pl.ds(pl.multiple_of(j * tn + c * SUB_TN, SUB_TN), SUB_TN)
                mix = jnp.dot(mix_ref[...], w_ref[:, sub], preferred_element_type=F32)
                acc[:, cols] = DN_ALPHA * x_ref[:, sub] + mix

            for s in range(slabs_per_step):
                r = j * slabs_per_step + s
                _layer_norm_slab(prev, ybf_scr, g, b, r)

                @pl.when(i > 0)
                def _():
                    for c in _ln_copies(prev, ybf_scr, y_hbm, ybf_hbm, sems, (i - 1) * tm, r):
                        c.start()

        @pl.when((i == nt) & (j == 0))
        def _():
            _layer_norm_start(prev, ybf_scr, g_ref, b_ref, y_hbm, ybf_hbm, sems, (nt - 1) * tm)
            _layer_norm_wait(prev, ybf_scr, y_hbm, ybf_hbm, sems, (nt - 1) * tm)

    @pl.when(i % 2 == 0)
    def _():
        step(acc_even, acc_odd)

    @pl.when(i % 2 == 1)
    def _():
        step(acc_odd, acc_even)


def _out_proj(mix, w_o_bf, x, g, b, tm):
    m = x.shape[0]
    nt = m // tm
    nj = D_MODEL // OUT_TN
    n_slabs = tm // LN_ROWS
    slabs_per_step = n_slabs // nj if nt > 1 else 0
    assert nt == 1 or slabs_per_step * nj == n_slabs
    row = lambda i: jnp.minimum(i, nt - 1)
    col = lambda i, j: jnp.where(i < nt, j, nj - 1)
    kern = functools.partial(_out_proj_kernel, slabs_per_step=slabs_per_step)
    return pl.pallas_call(
        kern,
        grid=(nt + 1, nj),
        in_specs=[
            pl.BlockSpec((tm, D_MODEL), lambda i, j: (row(i), 0)),
            pl.BlockSpec((D_MODEL, OUT_TN), lambda i, j: (0, col(i, j))),
            pl.BlockSpec((tm, OUT_TN), lambda i, j: (row(i), col(i, j))),
            pl.BlockSpec((1, D_MODEL), lambda i, j: (0, 0)),
            pl.BlockSpec((1, D_MODEL), lambda i, j: (0, 0)),
        ],
        out_specs=[pl.BlockSpec(memory_space=pl.ANY), pl.BlockSpec(memory_space=pl.ANY)],
        out_shape=[jax.ShapeDtypeStruct((m, D_MODEL), F32),
                   jax.ShapeDtypeStruct((m, D_MODEL), BF16)],
        scratch_shapes=_ln_scratch(tm),
        compiler_params=_params("arbitrary", "arbitrary"),
        name="out_proj",
    )(mix, w_o_bf, x, g, b)


def _shift_rows(x, carry_row, first8):
    rolled = pltpu.roll(x, 1, axis=0)
    fill = 0.0 if carry_row is None else carry_row
    head = jnp.where(first8 == 0, fill, rolled[0:SUBLANES, :])
    return jnp.concatenate([head, rolled[SUBLANES:, :]], axis=0)


def _ffn_up_kernel(xp_ref, xs_ref, wa_ref, wb_ref, cw_ref, cb_ref, sc_ref, wd_ref, *rest):
    hp_ref, tail_ref, hs_ref, nsc_ref, wdbf_ref, w_scr = rest[-6:]
    i = pl.program_id(0)
    tm = xp_ref.shape[0]
    tn = wa_ref.shape[1]
    w_scr[:, 0:tn] = wa_ref[...].astype(BF16)
    w_scr[:, tn:2 * tn] = wb_ref[...].astype(BF16)
    w = w_scr[...]
    cw0 = cw_ref[0:1, :]
    cw1 = cw_ref[1:2, :]
    cw2 = cw_ref[2:3, :]
    cb = cb_ref[...]

    wdbf_ref[...] = wd_ref[...].astype(BF16)

    first8 = lax.broadcasted_iota(jnp.int32, (SUBLANES, tn), 0)
    p_last = None
    q_last = None
    for r in range(tm // DOT_ROWS):
        r0 = r * DOT_ROWS
        ab = jnp.dot(xp_ref[pl.ds(r0, DOT_ROWS), :], w, preferred_element_type=F32)
        a = ab[:, 0:tn]
        p = a * cw0
        q = a * cw1 + _shift_rows(p, p_last, first8)
        conv = (cb + a * cw2) + _shift_rows(q, q_last, first8)
        p_last = p[DOT_ROWS - 1:DOT_ROWS, :]
        q_last = q[DOT_ROWS - 1:DOT_ROWS, :]
        hp_ref[pl.ds(r0, DOT_ROWS), :] = (_gelu_exact(conv) * ab[:, tn:2 * tn]).astype(BF16)
    tail_ref[0] = a[DOT_ROWS - SUBLANES:DOT_ROWS, :]

    @pl.when(i == 0)
    def _():
        ab_s = jnp.dot(xs_ref[...], w_scr[...], preferred_element_type=F32)
        a_s = ab_s[:, 0:tn]
        b_s = ab_s[:, tn:2 * tn]
        s1 = sc_ref[:, 1, :]
        conv_s = cb + sc_ref[:, 0, :] * cw0
        conv_s = conv_s + s1 * cw1
        conv_s = conv_s + a_s * cw2
        hs_ref[...] = (_gelu_exact(conv_s) * b_s).astype(BF16)
        nsc_ref[:, 0, :] = s1
        nsc_ref[:, 1, :] = a_s


def _ffn_up(xp_bf, xs_bf, w_up, conv_w, conv_b, state_all, new_state_all, w_down, li, seq):
    mp = xp_bf.shape[0]
    ns = xs_bf.shape[0]
    nj = D_FF // FFN_TN
    tm = seq
    assert (mp // tm) * nj * CAST_ROWS == D_FF
    first = lambda i, j: jnp.where(i == 0, j, nj - 1)
    state_spec = pl.BlockSpec((None, ns, CONV_WIDTH - 1, FFN_TN),
                              lambda i, j: (li, 0, 0, first(i, j)))
    in_specs = [
        pl.BlockSpec((tm, D_MODEL), lambda i, j: (i, 0), pipeline_mode=pl.Buffered(1)),
        pl.BlockSpec((ns, D_MODEL), lambda i, j: (0, 0)),
        pl.BlockSpec((None, D_MODEL, FFN_TN), lambda i, j: (li, 0, j)),
        pl.BlockSpec((None, D_MODEL, FFN_TN), lambda i, j: (li, 0, nj + j)),
        pl.BlockSpec((None, CONV_WIDTH, FFN_TN), lambda i, j: (li, 0, j)),
        pl.BlockSpec((None, 1, FFN_TN), lambda i, j: (li, 0, j)),
        state_spec,
        pl.BlockSpec((None, CAST_ROWS, D_MODEL), lambda i, j: (li, i * nj + j, 0)),
    ]
    args = [xp_bf, xs_bf, w_up, w_up, conv_w, conv_b, state_all, w_down]
    aliases = {}
    if new_state_all is not None:
        in_specs.append(pl.BlockSpec(memory_space=pl.ANY))
        args.append(new_state_all)
        aliases = {len(args) - 1: 3}
    return pl.pallas_call(
        _ffn_up_kernel,
        grid=(mp // tm, nj),
        in_specs=in_specs,
        out_specs=[
            pl.BlockSpec((tm, FFN_TN), lambda i, j: (i, j)),
            pl.BlockSpec((1, SUBLANES, FFN_TN), lambda i, j: (i, 0, j)),
            pl.BlockSpec((ns, FFN_TN), lambda i, j: (0, first(i, j))),
            state_spec,
            pl.BlockSpec((CAST_ROWS, D_MODEL), lambda i, j: (i * nj + j, 0)),
        ],
        out_shape=[jax.ShapeDtypeStruct((mp, D_FF), BF16),
                   jax.ShapeDtypeStruct((mp // tm, SUBLANES, D_FF), F32),
                   jax.ShapeDtypeStruct((ns, D_FF), BF16),
                   jax.ShapeDtypeStruct(state_all.shape, F32),
                   jax.ShapeDtypeStruct((D_FF, D_MODEL), BF16)],
        input_output_aliases=aliases,
        scratch_shapes=[pltpu.VMEM((D_MODEL, 2 * FFN_TN), BF16)],
        compiler_params=_params("arbitrary", "arbitrary"),
        name="ffn_up",
    )(*args)


def _ffn_down_kernel(h_ref, w_ref, x_ref, g_ref, b_ref, y_hbm, ybf_hbm, acc_even, acc_odd,
                     ybf_scr, sems, *, slabs_per_step):
    i = pl.program_id(0)
    k = pl.program_id(1)
    j = pl.program_id(2)
    nt = pl.num_programs(0) - 1
    nj = pl.num_programs(2)
    tm = h_ref.shape[0]
    tn = w_ref.shape[1]
    first = (k == 0) & (j == 0)
    g = g_ref[...]
    b = b_ref[...]

    @pl.when(first & (i == 0))
    def _():
        acc_even[...] = jnp.zeros_like(acc_even)
        acc_odd[...] = jnp.zeros_like(acc_odd)

    def step(acc, prev):
        @pl.when(first & (i >= 2))
        def _():
            _layer_norm_wait(acc, ybf_scr, y_hbm, ybf_hbm, sems, (i - 2) * tm)

        @pl.when(i < nt)
        def _():
            for c in range(tn // SUB_TN):
                sub = pl.ds(c * SUB_TN, SUB_TN)
                cols = pl.ds(pl.multiple_of(j * tn + c * SUB_TN, SUB_TN), SUB_TN)
                part = jnp.dot(h_ref[...], w_ref[:, sub], preferred_element_type=F32)
                base = jnp.where(k == 0, DN_ALPHA * x_ref[:, sub], acc[:, cols])
                acc[:, cols] = base + part

            for s in range(slabs_per_step):
                r = (k * nj + j) * slabs_per_step + s
                _layer_norm_slab(prev, ybf_scr, g, b, r)

                @pl.when(i > 0)
                def _():
                    for c in _ln_copies(prev, ybf_scr, y_hbm, ybf_hbm, sems, (i - 1) * tm, r):
                        c.start()

        @pl.when((i == nt) & first)
        def _():
            _layer_norm_start(prev, ybf_scr, g_ref, b_ref, y_hbm, ybf_hbm, sems, (nt - 1) * tm)
            _layer_norm_wait(prev, ybf_scr, y_hbm, ybf_hbm, sems, (nt - 1) * tm)

    @pl.when(i % 2 == 0)
    def _():
        step(acc_even, acc_odd)

    @pl.when(i % 2 == 1)
    def _():
        step(acc_odd, acc_even)


def _ffn_down(h, w_down_bf, x, g, b, tm):
    m = x.shape[0]
    nt = m // tm
    nk = D_FF // DOWN_TK
    nj = D_MODEL // DOWN_TN
    n_slabs = tm // LN_ROWS
    slabs_per_step = n_slabs // (nk * nj) if nt > 1 else 0
    assert nt == 1 or slabs_per_step * nk * nj == n_slabs
    row = lambda i: jnp.minimum(i, nt - 1)
    kk = lambda i, k: jnp.where(i < nt, k, nk - 1)
    jj = lambda i, j: jnp.where(i < nt, j, nj - 1)
    kern = functools.partial(_ffn_down_kernel, slabs_per_step=slabs_per_step)
    return pl.pallas_call(
        kern,
        grid=(nt + 1, nk, nj),
        in_specs=[
            pl.BlockSpec((tm, DOWN_TK), lambda i, k, j: (row(i), kk(i, k))),
            pl.BlockSpec((DOWN_TK, DOWN_TN), lambda i, k, j: (kk(i, k), jj(i, j))),
            pl.BlockSpec((tm, DOWN_TN),
                         lambda i, k, j: (row(i), jnp.where((k == 0) & (i < nt), j, nj - 1))),
            pl.BlockSpec((1, D_MODEL), lambda i, k, j: (0, 0)),
            pl.BlockSpec((1, D_MODEL), lambda i, k, j: (0, 0)),
        ],
        out_specs=[pl.BlockSpec(memory_space=pl.ANY), pl.BlockSpec(memory_space=pl.ANY)],
        out_shape=[jax.ShapeDtypeStruct((m, D_MODEL), F32),
                   jax.ShapeDtypeStruct((m, D_MODEL), BF16)],
        scratch_shapes=_ln_scratch(tm),
        compiler_params=_params("arbitrary", "arbitrary", "arbitrary"),
        name="ffn_down",
    )(h, w_down_bf, x, g, b)


def _rotary_tables(pos):
    inv = ROPE_BASE ** (-jnp.arange(HALF_HEAD, dtype=F32) / HALF_HEAD)
    ang = pos.astype(F32)[:, None] * inv[None, :]
    return jnp.cos(ang), jnp.sin(ang)


def _retention_tables(l):
    log_g = jnp.log1p(-(2.0 ** (-5.0 - jnp.arange(RET_HEADS, dtype=F32))))
    i = jnp.arange(l)
    diff = i[:, None] - i[None, :]
    decay = jnp.where(diff[None] >= 0,
                      jnp.exp(jnp.maximum(diff, 0)[None].astype(F32) * log_g[:, None, None]), 0.0)
    xi = jnp.exp((i + 1)[None].astype(F32) * log_g[:, None])
    zeta = jnp.exp((l - 1 - i)[None].astype(F32) * log_g[:, None])
    g_l = jnp.exp(l * log_g)
    return decay, xi, zeta, g_l


def kernel(x_prompt, x_sample, state_ret, state_pool, state_conv, w_in, w_pool, pool_scale, w_o,
           ln1_g, ln1_b, w_up, conv_w, conv_b, w_down, ln2_g, ln2_b):
    batch, seq, _ = x_prompt.shape
    n_s = x_sample.shape[0]
    assert x_sample.shape[1] == 1 and seq % RET_CHUNK == 0
    mp = batch * seq

    cos_p, sin_p = _rotary_tables(jnp.arange(seq))
    cos_s, sin_s = _rotary_tables(PAST_LEN + jnp.arange(1))
    cos_s = jnp.broadcast_to(cos_s, (n_s, HALF_HEAD))
    sin_s = jnp.broadcast_to(sin_s, (n_s, HALF_HEAD))
    decay, xi, zeta, g_l = _retention_tables(RET_CHUNK)
    xi = xi[:, :, None]
    zeta = zeta[:, :, None]
    gl_b = jnp.broadcast_to(g_l[:, None, None], (RET_HEADS, 1, HEAD_DIM))
    _, xi_s, _, _ = _retention_tables(1)
    gam_s = jnp.broadcast_to(xi_s[:, :, None], (RET_HEADS, 1, HEAD_DIM))

    xp = x_prompt.reshape(mp, D_MODEL)
    xs = x_sample.reshape(n_s, D_MODEL)
    xp_bf = xp.astype(BF16)
    xs_bf = xs.astype(BF16)
    conv_b3 = conv_b[:, None, :]

    pool_p, conv_p = [], []
    new_ret_prompt = new_ret_sample = new_pool_sample = new_conv_sample = None
    for li in range(DEPTH):
        w_pool_bf = w_pool[li].astype(BF16)
        scale = pool_scale[li][None, :]
        g1, b1 = ln1_g[li][None, :], ln1_b[li][None, :]
        g2, b2 = ln2_g[li][None, :], ln2_b[li][None, :]

        proj_p, proj_s, w_o_bf = _proj(xp_bf, xs_bf, w_in, w_o, li, seq)

        mix, new_ret_prompt = _ret_prompt(proj_p, cos_p, sin_p, decay, xi, zeta, gl_b, li,
                                          new_ret_prompt, batch, seq)
        mix = _pool_prompt(proj_p, w_pool_bf, scale, mix, seq)
        xp, xp_bf = _out_proj(mix, w_o_bf, xp, g1, b1, LN_TM)
        pool_p.append(proj_p.reshape(batch, seq, IN_WIDTH)[:, seq - POOL_BUF:, 4 * RET_WIDTH:])

        mix, new_ret_sample = _ret_sample(proj_s, cos_s, sin_s, state_ret, li, gam_s,
                                          new_ret_sample)
        mix, new_pool_sample = _pool_sample(proj_s, state_pool, li, w_pool_bf, scale, mix,
                                            new_pool_sample)
        xs, xs_bf = _out_proj(mix, w_o_bf, xs, g1, b1, n_s)

        h_p, tail, h_s, new_conv_sample, w_down_bf = _ffn_up(
            xp_bf, xs_bf, w_up, conv_w, conv_b3, state_conv, new_conv_sample, w_down, li, seq)
        xp, xp_bf = _ffn_down(h_p, w_down_bf, xp, g2, b2, LN_TM)
        xs, xs_bf = _ffn_down(h_s, w_down_bf, xs, g2, b2, n_s)
        conv_p.append(tail[:, SUBLANES - (CONV_WIDTH - 1):, :])

    return (xp.reshape(batch, seq, D_MODEL), xs.reshape(n_s, 1, D_MODEL),
            new_ret_prompt, new_ret_sample, jnp.stack(pool_p), new_pool_sample,
            jnp.stack(conv_p), new_conv_sample)
```

```python
import functools

import jax
import jax.numpy as jnp
from jax import lax
from jax.experimental import pallas as pl
from jax.experimental.pallas import tpu as pltpu

F32 = jnp.float32
BF16 = jnp.bfloat16

D_MODEL = 4096
DEPTH = 2
PAST_LEN = 16384
RET_WIDTH = D_MODEL // 2
RET_HEADS = 8
HEAD_DIM = RET_WIDTH // RET_HEADS
HALF_HEAD = HEAD_DIM // 2
RET_CHUNK = 128
ROPE_BASE = 10000.0
POOL_WIDTH = D_MODEL - RET_WIDTH
POOL_WINDOWS = (2, 4, 8, 16)
POOL_GROUP_DIM = POOL_WIDTH // len(POOL_WINDOWS)
POOL_BUF = max(POOL_WINDOWS) - 1
IN_WIDTH = 4 * RET_WIDTH + POOL_WIDTH
D_FF = ((8 * D_MODEL // 3) + 255) // 256 * 256
CONV_WIDTH = 3
DN_ALPHA = (2.0 * DEPTH) ** 0.25
LN_EPS = 1e-5
K_SCALE = HEAD_DIM ** -0.5

V7X_VMEM_BYTES = 64 * 1024 * 1024
VMEM_LIMIT = V7X_VMEM_BYTES - 3 * 1024 * 1024
SUBLANES = 8
HALO_ROWS = 2 * SUBLANES

PROJ_TN = 512
FFN_TN = 256
LN_TM = 512
OUT_TN = 1024
SUB_TN = 512
LN_ROWS = 64
DOWN_TK = D_FF // 2
DOWN_TN = 1024
POOL_TM = 512
RET_STEP_CHUNKS = 4
SAMPLE_NB = 16
DOT_ROWS = 1024
CAST_ROWS = 64


def _params(*semantics):
    return pltpu.CompilerParams(dimension_semantics=semantics, vmem_limit_bytes=VMEM_LIMIT)


def _layer_norm(y, g, b):
    mu = jnp.mean(y, axis=-1, keepdims=True)
    d = y - mu
    var = jnp.mean(d * d, axis=-1, keepdims=True)
    return d * lax.rsqrt(var + LN_EPS) * g + b


def _ln_copies(acc_ref, ybf_scr, y_hbm, ybf_hbm, sems, row0, r):
    src = pl.ds(pl.multiple_of(r * LN_ROWS, LN_ROWS), LN_ROWS)
    dst = pl.ds(pl.multiple_of(row0 + r * LN_ROWS, LN_ROWS), LN_ROWS)
    return (pltpu.make_async_copy(acc_ref.at[src, :], y_hbm.at[dst, :], sems.at[0, r]),
            pltpu.make_async_copy(ybf_scr.at[src, :], ybf_hbm.at[dst, :], sems.at[1, r]))


def _layer_norm_slab(acc_ref, ybf_scr, g, b, r):
    rows = pl.ds(pl.multiple_of(r * LN_ROWS, LN_ROWS), LN_ROWS)
    y = _layer_norm(acc_ref[rows, :], g, b)
    acc_ref[rows, :] = y
    ybf_scr[rows, :] = y.astype(BF16)


def _layer_norm_start(acc_ref, ybf_scr, g_ref, b_ref, y_hbm, ybf_hbm, sems, row0):
    g = g_ref[...]
    b = b_ref[...]

    def body(r, carry):
        _layer_norm_slab(acc_ref, ybf_scr, g, b, r)
        for c in _ln_copies(acc_ref, ybf_scr, y_hbm, ybf_hbm, sems, row0, r):
            c.start()
        return carry

    lax.fori_loop(0, acc_ref.shape[0] // LN_ROWS, body, 0)


def _layer_norm_wait(acc_ref, ybf_scr, y_hbm, ybf_hbm, sems, row0):
    for r in range(acc_ref.shape[0] // LN_ROWS):
        for c in _ln_copies(acc_ref, ybf_scr, y_hbm, ybf_hbm, sems, row0, r):
            c.wait()


def _ln_scratch(tm):
    return [pltpu.VMEM((tm, D_MODEL), F32),
            pltpu.VMEM((tm, D_MODEL), F32),
            pltpu.VMEM((tm, D_MODEL), BF16),
            pltpu.SemaphoreType.DMA((2, tm // LN_ROWS))]


def _gelu_exact(x):
    return 0.5 * x * (1.0 + lax.erf(x * (0.5 ** 0.5)))


def _proj_kernel(xp_ref, xs_ref, w_ref, wo_ref, op_ref, os_ref, wobf_ref, *, cast_steps):
    i = pl.program_id(0)
    j = pl.program_id(1)
    w = w_ref[...].astype(BF16)
    for r in range(xp_ref.shape[0] // DOT_ROWS):
        rows = pl.ds(r * DOT_ROWS, DOT_ROWS)
        op_ref[rows, :] = jnp.dot(xp_ref[rows, :], w, preferred_element_type=F32)

    @pl.when(i * pl.num_programs(1) + j < cast_steps)
    def _():
        wobf_ref[...] = wo_ref[...].astype(BF16)

    @pl.when(i == 0)
    def _():
        os_ref[...] = jnp.dot(xs_ref[...], w, preferred_element_type=F32)


def _proj(xp_bf, xs_bf, w_in, w_o, li, tm):
    mp = xp_bf.shape[0]
    ns = xs_bf.shape[0]
    nj = IN_WIDTH // PROJ_TN
    cast_steps = D_MODEL // CAST_ROWS
    assert cast_steps <= (mp // tm) * nj
    kern = functools.partial(_proj_kernel, cast_steps=cast_steps)
    whole = lambda a: pl.BlockSpec(a.shape, lambda i, j: (0, 0))
    slab = lambda i, j: jnp.minimum(i * nj + j, cast_steps - 1)
    return pl.pallas_call(
        kern,
        grid=(mp // tm, nj),
        in_specs=[
            pl.BlockSpec((tm, D_MODEL), lambda i, j: (i, 0), pipeline_mode=pl.Buffered(1)),
            whole(xs_bf),
            pl.BlockSpec((None, D_MODEL, PROJ_TN), lambda i, j: (li, 0, j)),
            pl.BlockSpec((None, CAST_ROWS, D_MODEL), lambda i, j: (li, slab(i, j), 0)),
        ],
        out_specs=[
            pl.BlockSpec((tm, PROJ_TN), lambda i, j: (i, j)),
            pl.BlockSpec((ns, PROJ_TN), lambda i, j: (0, jnp.where(i == 0, j, nj - 1))),
            pl.BlockSpec((CAST_ROWS, D_MODEL), lambda i, j: (slab(i, j), 0)),
        ],
        out_shape=[jax.ShapeDtypeStruct((mp, IN_WIDTH), F32),
                   jax.ShapeDtypeStruct((ns, IN_WIDTH), F32),
                   jax.ShapeDtypeStruct((D_MODEL, D_MODEL), BF16)],
        compiler_params=_params("arbitrary", "arbitrary"),
        name="proj",
    )(xp_bf, xs_bf, w_in, w_o)


def _group_norm_gate(o, g):
    mu = jnp.mean(o, axis=-1, keepdims=True)
    d = o - mu
    var = jnp.mean(d * d, axis=-1, keepdims=True)
    return d * lax.rsqrt(var + LN_EPS) * (g * jax.nn.sigmoid(g))


def _rotary(x, cos, sin):
    x1 = x[:, 0:HALF_HEAD]
    x2 = x[:, HALF_HEAD:HEAD_DIM]
    return jnp.concatenate([x1 * cos - x2 * sin, x2 * cos + x1 * sin], axis=-1)


def _ret_prompt_kernel(q_ref, k_ref, v_ref, g_ref, cos_ref, sin_ref, decay_ref, xi_ref, zeta_ref,
                       gl_ref, *rest):
    o_ref, snew_ref, s_scr = rest[-3:]
    c = pl.program_id(1)

    @pl.when(c == 0)
    def _():
        s_scr[...] = jnp.zeros_like(s_scr)

    for cc in range(q_ref.shape[0] // RET_CHUNK):
        rows = pl.ds(cc * RET_CHUNK, RET_CHUNK)
        cos = cos_ref[rows, :]
        sin = sin_ref[rows, :]
        for h in range(RET_HEADS):
            cols = pl.ds(h * HEAD_DIM, HEAD_DIM)
            q = _rotary(q_ref[rows, cols], cos, sin)
            k = _rotary(k_ref[rows, cols], cos, sin) * K_SCALE
            vb = v_ref[rows, cols].astype(BF16)
            s = s_scr[h]
            scores = lax.dot_general(q.astype(BF16), k.astype(BF16), (((1,), (1,)), ((), ())),
                                     preferred_element_type=F32) * decay_ref[h]
            o = jnp.dot(scores.astype(BF16), vb, preferred_element_type=F32)
            o = o + jnp.dot((q * xi_ref[h]).astype(BF16), s.astype(BF16),
                            preferred_element_type=F32)
            kz_t = (k * zeta_ref[h]).T.astype(BF16)
            s_scr[h] = gl_ref[h] * s + jnp.dot(kz_t, vb, preferred_element_type=F32)
            o_ref[rows, cols] = _group_norm_gate(o, g_ref[rows, cols]).astype(BF16)

    @pl.when(c == pl.num_programs(1) - 1)
    def _():
        snew_ref[...] = s_scr[...]


def _ret_prompt(proj, cos, sin, decay, xi, zeta, gl, li, new_state_all, batch, seq):
    rows = RET_STEP_CHUNKS * RET_CHUNK
    nc = seq // rows

    def col(cb):
        return pl.BlockSpec((rows, RET_WIDTH), lambda b, c: (b * nc + c, cb))

    def whole(a):
        return pl.BlockSpec(a.shape, lambda b, c: (0,) * a.ndim)

    table = pl.BlockSpec((rows, HALF_HEAD), lambda b, c: (c, 0))
    in_specs = [col(0), col(1), col(2), col(3), table, table,
                whole(decay), whole(xi), whole(zeta), whole(gl)]
    args = [proj, proj, proj, proj, cos, sin, decay, xi, zeta, gl]
    aliases = {}
    if new_state_all is not None:
        in_specs.append(pl.BlockSpec(memory_space=pl.ANY))
        args.append(new_state_all)
        aliases = {len(args) - 1: 1}
    return pl.pallas_call(
        _ret_prompt_kernel,
        grid=(batch, nc),
        in_specs=in_specs,
        out_specs=[
            pl.BlockSpec((rows, RET_WIDTH), lambda b, c: (b * nc + c, 0)),
            pl.BlockSpec((None, None, RET_HEADS, HEAD_DIM, HEAD_DIM), lambda b, c: (li, b, 0, 0, 0)),
        ],
        out_shape=[
            jax.ShapeDtypeStruct((batch * seq, D_MODEL), BF16),
            jax.ShapeDtypeStruct((DEPTH, batch, RET_HEADS, HEAD_DIM, HEAD_DIM), F32),
        ],
        input_output_aliases=aliases,
        scratch_shapes=[pltpu.VMEM((RET_HEADS, HEAD_DIM, HEAD_DIM), F32)],
        compiler_params=_params("parallel", "arbitrary"),
        name="ret_prompt",
    )(*args)


def _ret_sample_kernel(q_ref, k_ref, v_ref, g_ref, cos_ref, sin_ref, s0_ref, gam_ref, *rest):
    o_ref, snew_ref = rest[-2:]
    q = _rotary(q_ref[...], cos_ref[...], sin_ref[...])
    k = _rotary(k_ref[...], cos_ref[...], sin_ref[...]) * K_SCALE
    v = v_ref[...]
    vb = v.astype(BF16)
    gam = gam_ref[0]
    qg = (q * gam).astype(BF16)
    rows = lax.broadcasted_iota(jnp.int32, (SAMPLE_NB, 1), 0)
    o_state = jnp.zeros((SAMPLE_NB, HEAD_DIM), F32)
    for n in range(SAMPLE_NB):
        s = s0_ref[n, 0]
        r = jnp.dot(qg, s.astype(BF16), preferred_element_type=F32)
        o_state = jnp.where(rows == n, r, o_state)
        k_n_t = jnp.where(rows == n, k, 0.0).T.astype(BF16)
        snew_ref[n, 0] = gam * s + jnp.dot(k_n_t, vb, preferred_element_type=F32)
    qk = jnp.sum(q * k, axis=-1, keepdims=True)
    o = qk * v + o_state
    o_ref[...] = _group_norm_gate(o, g_ref[...]).astype(BF16)


def _ret_sample(proj, cos, sin, state_all, li, gam, new_state_all):
    n = proj.shape[0]

    def col(cb):
        return pl.BlockSpec((SAMPLE_NB, HEAD_DIM), lambda i, h: (i, cb * RET_HEADS + h))

    state_spec = pl.BlockSpec((None, SAMPLE_NB, 1, HEAD_DIM, HEAD_DIM),
                              lambda i, h: (li, i, h, 0, 0))
    table = pl.BlockSpec((SAMPLE_NB, HALF_HEAD), lambda i, h: (i, 0))
    in_specs = [col(0), col(1), col(2), col(3), table, table, state_spec,
                pl.BlockSpec((1, 1, HEAD_DIM), lambda i, h: (h, 0, 0))]
    args = [proj, proj, proj, proj, cos, sin, state_all, gam]
    aliases = {}
    if new_state_all is not None:
        in_specs.append(pl.BlockSpec(memory_space=pl.ANY))
        args.append(new_state_all)
        aliases = {len(args) - 1: 1}
    return pl.pallas_call(
        _ret_sample_kernel,
        grid=(n // SAMPLE_NB, RET_HEADS),
        in_specs=in_specs,
        out_specs=[pl.BlockSpec((SAMPLE_NB, HEAD_DIM), lambda i, h: (i, h)), state_spec],
        out_shape=[jax.ShapeDtypeStruct((n, D_MODEL), BF16),
                   jax.ShapeDtypeStruct(state_all.shape, F32)],
        input_output_aliases=aliases,
        compiler_params=_params("parallel", "arbitrary"),
        name="ret_sample",
    )(*args)


def _pool_prompt_kernel(u_ref, halo_ref, wp_ref, sc_ref, mix_hbm, o_ref, ext_scr, *,
                        tiles_per_seq):
    del mix_hbm
    i = pl.program_id(0)
    tm = u_ref.shape[0]
    t0 = (i % tiles_per_seq) * tm
    ext_scr[0:HALO_ROWS, :] = jnp.where(t0 == 0, 0.0, halo_ref[...])
    ext_scr[HALO_ROWS:HALO_ROWS + tm, :] = u_ref[...]
    pos = t0 + lax.broadcasted_iota(jnp.int32, (tm, 1), 0)
    for gi, w in enumerate(POOL_WINDOWS):
        cols = pl.ds(gi * POOL_GROUP_DIM, POOL_GROUP_DIM)
        u = u_ref[:, cols]
        s = ext_scr[:, cols]
        shift = 1
        while shift < w:
            s = s + pltpu.roll(s, shift, axis=0)
            shift *= 2
        s = s[HALO_ROWS:, :]
        cnt = jnp.minimum(w, pos + 1).astype(F32)
        pooled = s / cnt - u
        mixed = jnp.dot(pooled.astype(BF16), wp_ref[gi], preferred_element_type=F32)
        o_ref[:, cols] = (mixed * sc_ref[:, cols]).astype(BF16)


def _pool_prompt(proj, w_pool_bf, scale, mix, seq):
    m = proj.shape[0]
    tiles_per_seq = seq // POOL_TM
    u_col = 4 * RET_WIDTH // POOL_WIDTH
    halo_per_tile = POOL_TM // HALO_ROWS
    kern = functools.partial(_pool_prompt_kernel, tiles_per_seq=tiles_per_seq)
    return pl.pallas_call(
        kern,
        grid=(m // POOL_TM,),
        in_specs=[
            pl.BlockSpec((POOL_TM, POOL_WIDTH), lambda i: (i, u_col)),
            pl.BlockSpec((HALO_ROWS, POOL_WIDTH),
                         lambda i: (jnp.maximum(i * halo_per_tile - 1, 0), u_col)),
            pl.BlockSpec(w_pool_bf.shape, lambda i: (0, 0, 0)),
            pl.BlockSpec((1, POOL_WIDTH), lambda i: (0, 0)),
            pl.BlockSpec(memory_space=pl.ANY),
        ],
        out_specs=pl.BlockSpec((POOL_TM, POOL_WIDTH), lambda i: (i, 1)),
        out_shape=jax.ShapeDtypeStruct(mix.shape, BF16),
        input_output_aliases={4: 0},
        scratch_shapes=[pltpu.VMEM((HALO_ROWS + POOL_TM, POOL_WIDTH), F32)],
        compiler_params=_params("parallel"),
        name="pool_prompt",
    )(proj, proj, w_pool_bf, scale, mix)


def _pool_sample_kernel(u_ref, buf_ref, wp_ref, sc_ref, *rest):
    o_ref, nbuf_ref = rest[-2:]
    nbuf_ref[:, 0:POOL_BUF - 1, :] = buf_ref[:, 1:POOL_BUF, :]
    nbuf_ref[:, POOL_BUF - 1, :] = u_ref[...]
    row = lax.broadcasted_iota(jnp.int32, (1, POOL_BUF, 1), 1)
    for gi, w in enumerate(POOL_WINDOWS):
        cols = pl.ds(gi * POOL_GROUP_DIM, POOL_GROUP_DIM)
        u = u_ref[:, cols]
        past = jnp.where(row >= POOL_BUF - (w - 1), buf_ref[:, :, cols], 0.0)
        s = u + jnp.sum(past, axis=1)
        pooled = s / float(w) - u
        mixed = jnp.dot(pooled.astype(BF16), wp_ref[gi], preferred_element_type=F32)
        o_ref[:, cols] = (mixed * sc_ref[:, cols]).astype(BF16)


def _pool_sample(proj, buf_all, li, w_pool_bf, scale, mix, new_buf_all):
    n = proj.shape[0]
    u_col = 4 * RET_WIDTH // POOL_WIDTH
    buf_spec = pl.BlockSpec((None, SAMPLE_NB, POOL_BUF, POOL_WIDTH), lambda i: (li, i, 0, 0))
    in_specs = [
        pl.BlockSpec((SAMPLE_NB, POOL_WIDTH), lambda i: (i, u_col)),
        buf_spec,
        pl.BlockSpec(w_pool_bf.shape, lambda i: (0, 0, 0)),
        pl.BlockSpec((1, POOL_WIDTH), lambda i: (0, 0)),
        pl.BlockSpec(memory_space=pl.ANY),
    ]
    args = [proj, buf_all, w_pool_bf, scale, mix]
    aliases = {4: 0}
    if new_buf_all is not None:
        in_specs.append(pl.BlockSpec(memory_space=pl.ANY))
        args.append(new_buf_all)
        aliases[5] = 1
    return pl.pallas_call(
        _pool_sample_kernel,
        grid=(n // SAMPLE_NB,),
        in_specs=in_specs,
        out_specs=[pl.BlockSpec((SAMPLE_NB, POOL_WIDTH), lambda i: (i, 1)), buf_spec],
        out_shape=[jax.ShapeDtypeStruct(mix.shape, BF16),
                   jax.ShapeDtypeStruct(buf_all.shape, F32)],
        input_output_aliases=aliases,
        compiler_params=_params("parallel"),
        name="pool_sample",
    )(*args)


def _out_proj_kernel(mix_ref, w_ref, x_ref, g_ref, b_ref, y_hbm, ybf_hbm, acc_even, acc_odd,
                     ybf_scr, sems, *, slabs_per_step):
    i = pl.program_id(0)
    j = pl.program_id(1)
    nt = pl.num_programs(0) - 1
    tm = mix_ref.shape[0]
    tn = w_ref.shape[1]
    g = g_ref[...]
    b = b_ref[...]

    @pl.when((i == 0) & (j == 0))
    def _():
        acc_odd[...] = jnp.zeros_like(acc_odd)

    def step(acc, prev):
        @pl.when((j == 0) & (i >= 2))
        def _():
            _layer_norm_wait(acc, ybf_scr, y_hbm, ybf_hbm, sems, (i - 2) * tm)

        @pl.when(i < nt)
        def _():
            for c in range(tn // SUB_TN):
                sub = pl.ds(c * SUB_TN, SUB_TN)
                cols = pl.ds(pl.multiple_of(j * tn + c * SUB_TN, SUB_TN), SUB_TN)
                mix = jnp.dot(mix_ref[...], w_ref[:, sub], preferred_element_type=F32)
                acc[:, cols] = DN_ALPHA * x_ref[:, sub] + mix

            for s in range(slabs_per_step):
                r = j * slabs_per_step + s
                _layer_norm_slab(prev, ybf_scr, g, b, r)

                @pl.when(i > 0)
                def _():
                    for c in _ln_copies(prev, ybf_scr, y_hbm, ybf_hbm, sems, (i - 1) * tm, r):
                        c.start()

        @pl.when((i == nt) & (j == 0))
        def _():
            _layer_norm_start(prev, ybf_scr, g_ref, b_ref, y_hbm, ybf_hbm, sems, (nt - 1) * tm)
            _layer_norm_wait(prev, ybf_scr, y_hbm, ybf_hbm, sems, (nt - 1) * tm)

    @pl.when(i % 2 == 0)
    def _():
        step(acc_even, acc_odd)

    @pl.when(i % 2 == 1)
    def _():
        step(acc_odd, acc_even)


def _out_proj(mix, w_o_bf, x, g, b, tm):
    m = x.shape[0]
    nt = m // tm
    nj = D_MODEL // OUT_TN
    n_slabs = tm // LN_ROWS
    slabs_per_step = n_slabs // nj if nt > 1 else 0
    assert nt == 1 or slabs_per_step * nj == n_slabs
    row = lambda i: jnp.minimum(i, nt - 1)
    col = lambda i, j: jnp.where(i < nt, j, nj - 1)
    kern = functools.partial(_out_proj_kernel, slabs_per_step=slabs_per_step)
    return pl.pallas_call(
        kern,
        grid=(nt + 1, nj),
        in_specs=[
            pl.BlockSpec((tm, D_MODEL), lambda i, j: (row(i), 0)),
            pl.BlockSpec((D_MODEL, OUT_TN), lambda i, j: (0, col(i, j))),
            pl.BlockSpec((tm, OUT_TN), lambda i, j: (row(i), col(i, j))),
            pl.BlockSpec((1, D_MODEL), lambda i, j: (0, 0)),
            pl.BlockSpec((1, D_MODEL), lambda i, j: (0, 0)),
        ],
        out_specs=[pl.BlockSpec(memory_space=pl.ANY), pl.BlockSpec(memory_space=pl.ANY)],
        out_shape=[jax.ShapeDtypeStruct((m, D_MODEL), F32),
                   jax.ShapeDtypeStruct((m, D_MODEL), BF16)],
        scratch_shapes=_ln_scratch(tm),
        compiler_params=_params("arbitrary", "arbitrary"),
        name="out_proj",
    )(mix, w_o_bf, x, g, b)


def _shift_rows(x, carry_row, first8):
    rolled = pltpu.roll(x, 1, axis=0)
    fill = 0.0 if carry_row is None else carry_row
    head = jnp.where(first8 == 0, fill, rolled[0:SUBLANES, :])
    return jnp.concatenate([head, rolled[SUBLANES:, :]], axis=0)


def _ffn_up_kernel(xp_ref, xs_ref, wa_ref, wb_ref, cw_ref, cb_ref, sc_ref, wd_ref, *rest):
    hp_ref, tail_ref, hs_ref, nsc_ref, wdbf_ref, w_scr = rest[-6:]
    i = pl.program_id(0)
    tm = xp_ref.shape[0]
    tn = wa_ref.shape[1]
    w_scr[:, 0:tn] = wa_ref[...].astype(BF16)
    w_scr[:, tn:2 * tn] = wb_ref[...].astype(BF16)
    w = w_scr[...]
    cw0 = cw_ref[0:1, :]
    cw1 = cw_ref[1:2, :]
    cw2 = cw_ref[2:3, :]
    cb = cb_ref[...]

    wdbf_ref[...] = wd_ref[...].astype(BF16)

    first8 = lax.broadcasted_iota(jnp.int32, (SUBLANES, tn), 0)
    p_last = None
    q_last = None
    for r in range(tm // DOT_ROWS):
        r0 = r * DOT_ROWS
        ab = jnp.dot(xp_ref[pl.ds(r0, DOT_ROWS), :], w, preferred_element_type=F32)
        a = ab[:, 0:tn]
        p = a * cw0
        q = a * cw1 + _shift_rows(p, p_last, first8)
        conv = (cb + a * cw2) + _shift_rows(q, q_last, first8)
        p_last = p[DOT_ROWS - 1:DOT_ROWS, :]
        q_last = q[DOT_ROWS - 1:DOT_ROWS, :]
        hp_ref[pl.ds(r0, DOT_ROWS), :] = (_gelu_exact(conv) * ab[:, tn:2 * tn]).astype(BF16)
    tail_ref[0] = a[DOT_ROWS - SUBLANES:DOT_ROWS, :]

    @pl.when(i == 0)
    def _():
        ab_s = jnp.dot(xs_ref[...], w_scr[...], preferred_element_type=F32)
        a_s = ab_s[:, 0:tn]
        b_s = ab_s[:, tn:2 * tn]
        s1 = sc_ref[:, 1, :]
        conv_s = cb + sc_ref[:, 0, :] * cw0
        conv_s = conv_s + s1 * cw1
        conv_s = conv_s + a_s * cw2
        hs_ref[...] = (_gelu_exact(conv_s) * b_s).astype(BF16)
        nsc_ref[:, 0, :] = s1
        nsc_ref[:, 1, :] = a_s


def _ffn_up(xp_bf, xs_bf, w_up, conv_w, conv_b, state_all, new_state_all, w_down, li, seq):
    mp = xp_bf.shape[0]
    ns = xs_bf.shape[0]
    nj = D_FF // FFN_TN
    tm = seq
    assert (mp // tm) * nj * CAST_ROWS == D_FF
    first = lambda i, j: jnp.where(i == 0, j, nj - 1)
    state_spec = pl.BlockSpec((None, ns, CONV_WIDTH - 1, FFN_TN),
                              lambda i, j: (li, 0, 0, first(i, j)))
    in_specs = [
        pl.BlockSpec((tm, D_MODEL), lambda i, j: (i, 0), pipeline_mode=pl.Buffered(1)),
        pl.BlockSpec((ns, D_MODEL), lambda i, j: (0, 0)),
        pl.BlockSpec((None, D_MODEL, FFN_TN), lambda i, j: (li, 0, j)),
        pl.BlockSpec((None, D_MODEL, FFN_TN), lambda i, j: (li, 0, nj + j)),
        pl.BlockSpec((None, CONV_WIDTH, FFN_TN), lambda i, j: (li, 0, j)),
        pl.BlockSpec((None, 1, FFN_TN), lambda i, j: (li, 0, j)),
        state_spec,
        pl.BlockSpec((None, CAST_ROWS, D_MODEL), lambda i, j: (li, i * nj + j, 0)),
    ]
    args = [xp_bf, xs_bf, w_up, w_up, conv_w, conv_b, state_all, w_down]
    aliases = {}
    if new_state_all is not None:
        in_specs.append(pl.BlockSpec(memory_space=pl.ANY))
        args.append(new_state_all)
        aliases = {len(args) - 1: 3}
    return pl.pallas_call(
        _ffn_up_kernel,
        grid=(mp // tm, nj),
        in_specs=in_specs,
        out_specs=[
            pl.BlockSpec((tm, FFN_TN), lambda i, j: (i, j)),
            pl.BlockSpec((1, SUBLANES, FFN_TN), lambda i, j: (i, 0, j)),
            pl.BlockSpec((ns, FFN_TN), lambda i, j: (0, first(i, j))),
            state_spec,
            pl.BlockSpec((CAST_ROWS, D_MODEL), lambda i, j: (i * nj + j, 0)),
        ],
        out_shape=[jax.ShapeDtypeStruct((mp, D_FF), BF16),
                   jax.ShapeDtypeStruct((mp // tm, SUBLANES, D_FF), F32),
                   jax.ShapeDtypeStruct((ns, D_FF), BF16),
                   jax.ShapeDtypeStruct(state_all.shape, F32),
                   jax.ShapeDtypeStruct((D_FF, D_MODEL), BF16)],
        input_output_aliases=aliases,
        scratch_shapes=[pltpu.VMEM((D_MODEL, 2 * FFN_TN), BF16)],
        compiler_params=_params("arbitrary", "arbitrary"),
        name="ffn_up",
    )(*args)


def _ffn_down_kernel(h_ref, w_ref, x_ref, g_ref, b_ref, y_hbm, ybf_hbm, acc_even, acc_odd,
                     ybf_scr, sems, *, slabs_per_step):
    i = pl.program_id(0)
    k = pl.program_id(1)
    j = pl.program_id(2)
    nt = pl.num_programs(0) - 1
    nj = pl.num_programs(2)
    tm = h_ref.shape[0]
    tn = w_ref.shape[1]
    first = (k == 0) & (j == 0)
    g = g_ref[...]
    b = b_ref[...]

    @pl.when(first & (i == 0))
    def _():
        acc_even[...] = jnp.zeros_like(acc_even)
        acc_odd[...] = jnp.zeros_like(acc_odd)

    def step(acc, prev):
        @pl.when(first & (i >= 2))
        def _():
            _layer_norm_wait(acc, ybf_scr, y_hbm, ybf_hbm, sems, (i - 2) * tm)

        @pl.when(i < nt)
        def _():
            for c in range(tn // SUB_TN):
                sub = pl.ds(c * SUB_TN, SUB_TN)
                cols = pl.ds(pl.multiple_of(j * tn + c * SUB_TN, SUB_TN), SUB_TN)
                part = jnp.dot(h_ref[...], w_ref[:, sub], preferred_element_type=F32)
                base = jnp.where(k == 0, DN_ALPHA * x_ref[:, sub], acc[:, cols])
                acc[:, cols] = base + part

            for s in range(slabs_per_step):
                r = (k * nj + j) * slabs_per_step + s
                _layer_norm_slab(prev, ybf_scr, g, b, r)

                @pl.when(i > 0)
                def _():
                    for c in _ln_copies(prev, ybf_scr, y_hbm, ybf_hbm, sems, (i - 1) * tm, r):
                        c.start()

        @pl.when((i == nt) & first)
        def _():
            _layer_norm_start(prev, ybf_scr, g_ref, b_ref, y_hbm, ybf_hbm, sems, (nt - 1) * tm)
            _layer_norm_wait(prev, ybf_scr, y_hbm, ybf_hbm, sems, (nt - 1) * tm)

    @pl.when(i % 2 == 0)
    def _():
        step(acc_even, acc_odd)

    @pl.when(i % 2 == 1)
    def _():
        step(acc_odd, acc_even)


def _ffn_down(h, w_down_bf, x, g, b, tm):
    m = x.shape[0]
    nt = m // tm
    nk = D_FF // DOWN_TK
    nj = D_MODEL // DOWN_TN
    n_slabs = tm // LN_ROWS
    slabs_per_step = n_slabs // (nk * nj) if nt > 1 else 0
    assert nt == 1 or slabs_per_step * nk * nj == n_slabs
    row = lambda i: jnp.minimum(i, nt - 1)
    kk = lambda i, k: jnp.where(i < nt, k, nk - 1)
    jj = lambda i, j: jnp.where(i < nt, j, nj - 1)
    kern = functools.partial(_ffn_down_kernel, slabs_per_step=slabs_per_step)
    return pl.pallas_call(
        kern,
        grid=(nt + 1, nk, nj),
        in_specs=[
            pl.BlockSpec((tm, DOWN_TK), lambda i, k, j: (row(i), kk(i, k))),
            pl.BlockSpec((DOWN_TK, DOWN_TN), lambda i, k, j: (kk(i, k), jj(i, j))),
            pl.BlockSpec((tm, DOWN_TN),
                         lambda i, k, j: (row(i), jnp.where((k == 0) & (i < nt), j, nj - 1))),
            pl.BlockSpec((1, D_MODEL), lambda i, k, j: (0, 0)),
            pl.BlockSpec((1, D_MODEL), lambda i, k, j: (0, 0)),
        ],
        out_specs=[pl.BlockSpec(memory_space=pl.ANY), pl.BlockSpec(memory_space=pl.ANY)],
        out_shape=[jax.ShapeDtypeStruct((m, D_MODEL), F32),
                   jax.ShapeDtypeStruct((m, D_MODEL), BF16)],
        scratch_shapes=_ln_scratch(tm),
        compiler_params=_params("arbitrary", "arbitrary", "arbitrary"),
        name="ffn_down",
    )(h, w_down_bf, x, g, b)


def _rotary_tables(pos):
    inv = ROPE_BASE ** (-jnp.arange(HALF_HEAD, dtype=F32) / HALF_HEAD)
    ang = pos.astype(F32)[:, None] * inv[None, :]
    return jnp.cos(ang), jnp.sin(ang)


def _retention_tables(l):
    log_g = jnp.log1p(-(2.0 ** (-5.0 - jnp.arange(RET_HEADS, dtype=F32))))
    i = jnp.arange(l)
    diff = i[:, None] - i[None, :]
    decay = jnp.where(diff[None] >= 0,
                      jnp.exp(jnp.maximum(diff, 0)[None].astype(F32) * log_g[:, None, None]), 0.0)
    xi = jnp.exp((i + 1)[None].astype(F32) * log_g[:, None])
    zeta = jnp.exp((l - 1 - i)[None].astype(F32) * log_g[:, None])
    g_l = jnp.exp(l * log_g)
    return decay, xi, zeta, g_l


def kernel(x_prompt, x_sample, state_ret, state_pool, state_conv, w_in, w_pool, pool_scale, w_o,
           ln1_g, ln1_b, w_up, conv_w, conv_b, w_down, ln2_g, ln2_b):
    batch, seq, _ = x_prompt.shape
    n_s = x_sample.shape[0]
    assert x_sample.shape[1] == 1 and seq % RET_CHUNK == 0
    mp = batch * seq

    cos_p, sin_p = _rotary_tables(jnp.arange(seq))
    cos_s, sin_s = _rotary_tables(PAST_LEN + jnp.arange(1))
    cos_s = jnp.broadcast_to(cos_s, (n_s, HALF_HEAD))
    sin_s = jnp.broadcast_to(sin_s, (n_s, HALF_HEAD))
    decay, xi, zeta, g_l = _retention_tables(RET_CHUNK)
    xi = xi[:, :, None]
    zeta = zeta[:, :, None]
    gl_b = jnp.broadcast_to(g_l[:, None, None], (RET_HEADS, 1, HEAD_DIM))
    _, xi_s, _, _ = _retention_tables(1)
    gam_s = jnp.broadcast_to(xi_s[:, :, None], (RET_HEADS, 1, HEAD_DIM))

    xp = x_prompt.reshape(mp, D_MODEL)
    xs = x_sample.reshape(n_s, D_MODEL)
    xp_bf = xp.astype(BF16)
    xs_bf = xs.astype(BF16)
    conv_b3 = conv_b[:, None, :]

    pool_p, conv_p = [], []
    new_ret_prompt = new_ret_sample = new_pool_sample = new_conv_sample = None
    for li in range(DEPTH):
        w_pool_bf = w_pool[li].astype(BF16)
        scale = pool_scale[li][None, :]
        g1, b1 = ln1_g[li][None, :], ln1_b[li][None, :]
        g2, b2 = ln2_g[li][None, :], ln2_b[li][None, :]

        proj_p, proj_s, w_o_bf = _proj(xp_bf, xs_bf, w_in, w_o, li, seq)

        mix, new_ret_prompt = _ret_prompt(proj_p, cos_p, sin_p, decay, xi, zeta, gl_b, li,
                                          new_ret_prompt, batch, seq)
        mix = _pool_prompt(proj_p, w_pool_bf, scale, mix, seq)
        xp, xp_bf = _out_proj(mix, w_o_bf, xp, g1, b1, LN_TM)
        pool_p.append(proj_p.reshape(batch, seq, IN_WIDTH)[:, seq - POOL_BUF:, 4 * RET_WIDTH:])

        mix, new_ret_sample = _ret_sample(proj_s, cos_s, sin_s, state_ret, li, gam_s,
                                          new_ret_sample)
        mix, new_pool_sample = _pool_sample(proj_s, state_pool, li, w_pool_bf, scale, mix,
                                            new_pool_sample)
        xs, xs_bf = _out_proj(mix, w_o_bf, xs, g1, b1, n_s)

        h_p, tail, h_s, new_conv_sample, w_down_bf = _ffn_up(
            xp_bf, xs_bf, w_up, conv_w, conv_b3, state_conv, new_conv_sample, w_down, li, seq)
        xp, xp_bf = _ffn_down(h_p, w_down_bf, xp, g2, b2, LN_TM)
        xs, xs_bf = _ffn_down(h_s, w_down_bf, xs, g2, b2, n_s)
        conv_p.append(tail[:, SUBLANES - (CONV_WIDTH - 1):, :])

    return (xp.reshape(batch, seq, D_MODEL), xs.reshape(n_s, 1, D_MODEL),
            new_ret_prompt, new_ret_sample, jnp.stack(pool_p), new_pool_sample,
            jnp.stack(conv_p), new_conv_sample)
```
